```python
import jax
import jax.numpy as jnp
from jax import lax
import numpy as np

D_MODEL = 1024
BATCH = 8
SEQ = 4096
DEPTH = 2

N_MIXERS = 2
N_RWKV_LAYERS = (DEPTH + 1) // 2
N_NSA_LAYERS = DEPTH // 2
HEAD_DIM = 64
N_HEADS = D_MODEL // HEAD_DIM
NORM_EPS = 1e-6
N_SHIFT_MIX = 6
N_RWKV_PROJ = 4
DECAY_LORA = 64
ICLR_LORA = 64
LN_X_EPS = 64e-5
N_KV_GROUPS = 4
HEADS_PER_GROUP = N_HEADS // N_KV_GROUPS
KV_WIDTH = N_KV_GROUPS * HEAD_DIM
N_BRANCHES = 3
CMP_BLOCK = 32
CMP_STRIDE = 16
CMP_HIDDEN = 256
SLC_BLOCK = 64
SLC_TOPK = 16
N_LOCAL_BLOCKS = 2
WINDOW = 512
Q_BLOCK = 32
FORCE_BONUS = 1e4
NEG_INF = -1e30
NSA_IN_WIDTH = D_MODEL + 6 * KV_WIDTH + D_MODEL + N_BRANCHES * N_HEADS

kernel_name = "hybrid_rwkv7_nsa_interleaved"


def rms_norm(x, g, eps=NORM_EPS):
    xf = x.astype(jnp.float32)
    y = xf * lax.rsqrt(jnp.mean(xf * xf, axis=-1, keepdims=True) + eps)
    return (y * g.astype(jnp.float32)).astype(x.dtype)


def masked_softmax(s, mask):
    s = jnp.where(mask, s.astype(jnp.float32), NEG_INF)
    return jnp.where(mask, jax.nn.softmax(s, axis=-1), 0.0)


def wkv7_scan(r, w, k, v, kk, a):
    B, T, H, N = r.shape

    def step(S, inp):
        r_t, w_t, k_t, v_t, kk_t, a_t = inp
        sa = jnp.einsum('bhvk,bhk->bhv', S, -kk_t)
        S = (S * w_t[:, :, None, :]
             + sa[..., None] * (kk_t * a_t)[:, :, None, :]
             + v_t[..., None] * k_t[:, :, None, :])
        return S, jnp.einsum('bhvk,bhk->bhv', S, r_t)

    xs = tuple(jnp.moveaxis(u, 1, 0) for u in (r, w, k, v, kk, a))
    S0 = jnp.zeros((B, H, N, N), jnp.float32)
    _, o = lax.scan(step, S0, xs)
    return jnp.moveaxis(o, 0, 1)


def rwkv7_mixer(h, mu, w_in, w0, w1, w2, a0, a1, a2, k_k, k_a, r_k, lnx_g, lnx_b, w_out):
    B, T, D = h.shape
    H, N = N_HEADS, HEAD_DIM
    f32 = jnp.float32
    dh = jnp.pad(h, ((0, 0), (1, 0), (0, 0)))[:, :-1] - h
    xm = h[None] + dh[None] * mu[:, None, None, :]
    proj = jnp.einsum('cbtd,dcf->cbtf', xm[:N_RWKV_PROJ], w_in.reshape(D, N_RWKV_PROJ, D))
    r, k, v, z = proj[0], proj[1], proj[2], proj[3]
    xw, xa = xm[4], xm[5]
    w_log = -jax.nn.softplus(-(w0 + jnp.tanh(xw @ w1) @ w2).astype(f32)) - 0.5
    decay = jnp.exp(-jnp.exp(w_log))
    a = jax.nn.sigmoid((a0 + (xa @ a1) @ a2).astype(f32))
    heads = lambda u: u.astype(f32).reshape(B, T, H, N)
    r, k, v, decay, a = heads(r), heads(k), heads(v), heads(decay), heads(a)
    kk = k * k_k.astype(f32).reshape(H, N)
    kk = kk / jnp.maximum(jnp.sqrt(jnp.sum(kk * kk, axis=-1, keepdims=True)), 1e-12)
    k = k * (1.0 + (a - 1.0) * k_a.astype(f32).reshape(H, N))
    o = wkv7_scan(r, decay, k, v, kk, a)
    mean = jnp.mean(o, axis=-1, keepdims=True)
    var = jnp.mean(jnp.square(o - mean), axis=-1, keepdims=True)
    o = ((o - mean) * lax.rsqrt(var + LN_X_EPS) * lnx_g.astype(f32).reshape(H, N)
         + lnx_b.astype(f32).reshape(H, N))
    o = o + jnp.sum(r * k * r_k.astype(f32), axis=-1, keepdims=True) * v
    y = o.reshape(B, T, D).astype(h.dtype) * jax.nn.silu(z)
    return y @ w_out


def compress_blocks(u, pe, w1, b1, w2):
    T = u.shape[1]
    n_cmp = (T - CMP_BLOCK) // CMP_STRIDE + 1
    idx = jnp.arange(n_cmp)[:, None] * CMP_STRIDE + jnp.arange(CMP_BLOCK)[None, :]
    blocks = u[:, idx] + pe[None, None, :, None, :]
    hid = jax.nn.gelu(jnp.einsum('bnlgd,ldf->bngf', blocks,
                                 w1.reshape(CMP_BLOCK, HEAD_DIM, CMP_HIDDEN)) + b1)
    return jnp.einsum('bngf,fd->bngd', hid, w2)


def nsa_mixer(h, w_in, q_gain, k_gain, cmp_pe, cmp_w1, cmp_b1, cmp_w2, w_out):
    B, T, D = h.shape
    G, HG, N = N_KV_GROUPS, HEADS_PER_GROUP, HEAD_DIM
    f32 = jnp.float32
    proj = h @ w_in
    sizes = [D] + [KV_WIDTH] * 6 + [D, N_BRANCHES * N_HEADS]
    offs = np.cumsum(sizes)[:-1].tolist()
    q, kc_raw, vc_raw, ks, vs, kw, vw, z, gate = jnp.split(proj, offs, axis=-1)
    q = rms_norm(q.reshape(B, T, G, HG, N), q_gain)
    kv = lambda u: u.reshape(B, T, G, N)
    kc = rms_norm(compress_blocks(kv(kc_raw), cmp_pe[0], cmp_w1[0], cmp_b1[0], cmp_w2[0]), k_gain[0])
    vc = compress_blocks(kv(vc_raw), cmp_pe[1], cmp_w1[1], cmp_b1[1], cmp_w2[1])
    ks = rms_norm(kv(ks), k_gain[1])
    vs = kv(vs)
    kw = rms_norm(kv(kw), k_gain[2])
    vw = kv(vw)

    n_cmp = kc.shape[1]
    n_slc = T // SLC_BLOCK
    n_top = min(SLC_TOPK, n_slc)
    n_qb = T // Q_BLOCK
    scale = HEAD_DIM ** -0.5
    cmp_start = jnp.arange(n_cmp) * CMP_STRIDE
    cmp_end = cmp_start + CMP_BLOCK - 1
    slc_start = jnp.arange(n_slc) * SLC_BLOCK
    overlap = ((cmp_start[:, None] < slc_start[None, :] + SLC_BLOCK)
               & (cmp_end[:, None] >= slc_start[None, :])).astype(f32)
    ks_blk = jnp.moveaxis(ks.reshape(B, n_slc, SLC_BLOCK, G, N), 3, 1)
    vs_blk = jnp.moveaxis(vs.reshape(B, n_slc, SLC_BLOCK, G, N), 3, 1)
    kw_pad = jnp.pad(kw, ((0, 0), (WINDOW, 0), (0, 0), (0, 0)))
    vw_pad = jnp.pad(vw, ((0, 0), (WINDOW, 0), (0, 0), (0, 0)))
    b_ix = jnp.arange(B)[:, None, None, None]
    g_ix = jnp.arange(G)[None, :, None, None]
    blk_ids = jnp.arange(n_slc)

    def query_block(args):
        qb, q_blk, g_blk = args
        t = qb * Q_BLOCK + jnp.arange(Q_BLOCK)
        s = jnp.einsum('bqghd,bngd->bghqn', q_blk, kc).astype(f32) * scale
        p_cmp = masked_softmax(s, cmp_end[None, :] <= t[:, None])
        o_cmp = jnp.einsum('bghqn,bngd->bqghd', p_cmp.astype(vc.dtype), vc)
        imp = jnp.einsum('bghqn,nj->bgqj', p_cmp, overlap)
        dist = (t // SLC_BLOCK)[:, None] - blk_ids[None, :]
        forced = (blk_ids[None, :] == 0) | ((dist >= 0) & (dist < N_LOCAL_BLOCKS))
        score = jnp.where(dist >= 0, imp + jnp.where(forced, FORCE_BONUS, 0.0), -jnp.inf)
        _, sel = lax.top_k(score, n_top)
        k_sel = ks_blk[b_ix, g_ix, sel]
        v_sel = vs_blk[b_ix, g_ix, sel].reshape(B, G, Q_BLOCK, n_top * SLC_BLOCK, N)
        tok = sel[..., None] * SLC_BLOCK + jnp.arange(SLC_BLOCK)
        sel_mask = (tok <= t[None, None, :, None, None]).reshape(B, G, 1, Q_BLOCK, n_top * SLC_BLOCK)
        s = jnp.einsum('bqghd,bgqnsd->bghqns', q_blk, k_sel).astype(f32) * scale
        p = masked_softmax(s.reshape(B, G, HG, Q_BLOCK, n_top * SLC_BLOCK), sel_mask)
        o_slc = jnp.einsum('bghqm,bgqmd->bqghd', p.astype(v_sel.dtype), v_sel)
        start = qb * Q_BLOCK
        k_win = lax.dynamic_slice_in_dim(kw_pad, start, Q_BLOCK + WINDOW, axis=1)
        v_win = lax.dynamic_slice_in_dim(vw_pad, start, Q_BLOCK + WINDOW, axis=1)
        kpos = start - WINDOW + jnp.arange(Q_BLOCK + WINDOW)
        lag = t[:, None] - kpos[None, :]
        win_mask = (lag >= 0) & (lag < WINDOW) & (kpos[None, :] >= 0)
        s = jnp.einsum('bqghd,bkgd->bghqk', q_blk, k_win).astype(f32) * scale
        p = masked_softmax(s, win_mask)
        o_win = jnp.einsum('bghqk,bkgd->bqghd', p.astype(v_win.dtype), v_win)
        g = jax.nn.sigmoid(g_blk.astype(f32))[..., None]
        o = g[:, :, 0] * o_cmp + g[:, :, 1] * o_slc + g[:, :, 2] * o_win
        return o.astype(q_blk.dtype)

    q_blocks = jnp.moveaxis(q.reshape(B, n_qb, Q_BLOCK, G, HG, N), 1, 0)
    g_blocks = jnp.moveaxis(gate.reshape(B, n_qb, Q_BLOCK, N_BRANCHES, G, HG), 1, 0)
    o = lax.map(query_block, (jnp.arange(n_qb), q_blocks, g_blocks))
    o = jnp.moveaxis(o, 0, 1).reshape(B, T, D)
    y = o * jax.nn.silu(z)
    return y @ w_out


def setup_inputs(seed: int = 0) -> dict:
    key = jax.random.key(seed)
    keys = iter(jax.random.split(key, 32))
    nrm = lambda shape, scale: jax.random.normal(next(keys), shape, jnp.float32) * scale
    D, H, N = D_MODEL, N_HEADS, HEAD_DIM
    Lr, Ln = N_RWKV_LAYERS, N_NSA_LAYERS
    return {
        "x": nrm((BATCH, SEQ, D), 1.0),
        "norm_g": 1.0 + nrm((DEPTH, D), 0.02),
        "rwkv_mu": jax.random.uniform(next(keys), (Lr, N_SHIFT_MIX, D), jnp.float32),
        "rwkv_w_in": nrm((Lr, D, N_RWKV_PROJ * D), D ** -0.5),
        "rwkv_w0": nrm((Lr, D), 0.5),
        "rwkv_w1": nrm((Lr, D, DECAY_LORA), D ** -0.5),
        "rwkv_w2": nrm((Lr, DECAY_LORA, D), 0.5 * DECAY_LORA ** -0.5),
        "rwkv_a0": nrm((Lr, D), 0.1),
        "rwkv_a1": nrm((Lr, D, ICLR_LORA), D ** -0.5),
        "rwkv_a2": nrm((Lr, ICLR_LORA, D), 0.5 * ICLR_LORA ** -0.5),
        "rwkv_k_k": 0.85 + nrm((Lr, D), 0.02),
        "rwkv_k_a": 1.0 + nrm((Lr, D), 0.02),
        "rwkv_r_k": nrm((Lr, H, N), 0.1),
        "rwkv_lnx_g": 1.0 + nrm((Lr, D), 0.02),
        "rwkv_lnx_b": nrm((Lr, D), 0.02),
        "rwkv_w_out": nrm((Lr, D, D), D ** -0.5),
        "nsa_w_in": nrm((Ln, D, NSA_IN_WIDTH), D ** -0.5),
        "nsa_q_gain": 1.0 + nrm((Ln, N), 0.02),
        "nsa_k_gain": 1.0 + nrm((Ln, N_BRANCHES, N), 0.02),
        "nsa_cmp_pe": nrm((Ln, 2, CMP_BLOCK, N), 0.1),
        "nsa_cmp_w1": nrm((Ln, 2, CMP_BLOCK * N, CMP_HIDDEN), (CMP_BLOCK * N) ** -0.5),
        "nsa_cmp_b1": nrm((Ln, 2, CMP_HIDDEN), 0.02),
        "nsa_cmp_w2": nrm((Ln, 2, CMP_HIDDEN, N), CMP_HIDDEN ** -0.5),
        "nsa_w_out": nrm((Ln, D, D), D ** -0.5),
    }


def reference(x, norm_g, rwkv_mu, rwkv_w_in, rwkv_w0, rwkv_w1, rwkv_w2, rwkv_a0, rwkv_a1,
              rwkv_a2, rwkv_k_k, rwkv_k_a, rwkv_r_k, rwkv_lnx_g, rwkv_lnx_b, rwkv_w_out,
              nsa_w_in, nsa_q_gain, nsa_k_gain, nsa_cmp_pe, nsa_cmp_w1, nsa_cmp_b1, nsa_cmp_w2,
              nsa_w_out):
    for i in range(DEPTH):
        h = rms_norm(x, norm_g[i])
        j = i // N_MIXERS
        if i % N_MIXERS == 0:
            y = rwkv7_mixer(h, rwkv_mu[j], rwkv_w_in[j], rwkv_w0[j], rwkv_w1[j], rwkv_w2[j],
                            rwkv_a0[j], rwkv_a1[j], rwkv_a2[j], rwkv_k_k[j], rwkv_k_a[j],
                            rwkv_r_k[j], rwkv_lnx_g[j], rwkv_lnx_b[j], rwkv_w_out[j])
        else:
            y = nsa_mixer(h, nsa_w_in[j], nsa_q_gain[j], nsa_k_gain[j], nsa_cmp_pe[j],
                          nsa_cmp_w1[j], nsa_cmp_b1[j], nsa_cmp_w2[j], nsa_w_out[j])
        x = x + y
    return x
```

```python
import functools

import jax
import jax.numpy as jnp
from jax import lax
from jax.experimental import pallas as pl
from jax.experimental.pallas import tpu as pltpu

D_MODEL = 1024
HEAD_DIM = 64
N_HEADS = D_MODEL // HEAD_DIM
NORM_EPS = 1e-6
LN_X_EPS = 64e-5
DECAY_LORA = 64
ICLR_LORA = 64
N_KV_GROUPS = 4
HEADS_PER_GROUP = N_HEADS // N_KV_GROUPS
KV_WIDTH = N_KV_GROUPS * HEAD_DIM
N_BRANCHES = 3
CMP_BLOCK = 32
CMP_STRIDE = 16
CMP_HIDDEN = 256
SLC_BLOCK = 64
SLC_SHIFT = 6
SLC_TOPK = 16
N_LOCAL_BLOCKS = 2
WINDOW = 512
FORCE_BONUS = 1e4
NEG_INF = -1e30

LANES = 128
PAIR = LANES // HEAD_DIM
N_PAIRS = N_HEADS // PAIR
CHUNK = 64
VMEM_LIMIT = 48 * 1024 * 1024

BF16 = jnp.bfloat16


def _rms(x, eps=NORM_EPS):
    return x * lax.rsqrt(jnp.mean(x * x, axis=-1, keepdims=True) + eps)


def _bdot(a, b):
    return jnp.dot(a.astype(BF16), b.astype(BF16), preferred_element_type=jnp.float32)


def _bdot_nt(a, b):
    return lax.dot_general(a.astype(BF16), b.astype(BF16), (((1,), (1,)), ((), ())),
                           preferred_element_type=jnp.float32)


def _const_spec(shape):
    nd = len(shape)
    return pl.BlockSpec(shape, lambda *_: (0,) * nd)


def _rwkv_in_kernel(x_ref, xp_ref, g_ref, mu_ref, win_ref, w0_ref, w1_ref, w2_ref, a0_ref,
                    a1_ref, a2_ref, kk_ref, ka_ref,
                    r_out, lw_out, k_out, v_out, kk_out, a_out, z_out, *, tiles_per_seq):
    d = D_MODEL
    g = g_ref[...]
    h = _rms(x_ref[...]) * g
    hp = _rms(xp_ref[7:8, :]) * g
    hp = jnp.where(pl.program_id(0) % tiles_per_seq == 0, 0.0, hp)
    row = lax.broadcasted_iota(jnp.int32, h.shape, 0)
    hs = jnp.where(row == 0, hp, pltpu.roll(h, 1, axis=0))
    dh = hs - h

    def mix(c):
        return h + dh * mu_ref[c:c + 1, :]

    r_out[...] = _bdot(mix(0), win_ref[:, 0 * d:1 * d])
    k = _bdot(mix(1), win_ref[:, 1 * d:2 * d])
    v_out[...] = _bdot(mix(2), win_ref[:, 2 * d:3 * d])
    z_out[...] = _bdot(mix(3), win_ref[:, 3 * d:4 * d])

    u = w0_ref[...] + _bdot(jnp.tanh(_bdot(mix(4), w1_ref[...])), w2_ref[...])
    sp = jnp.maximum(-u, 0.0) + jnp.log(1.0 + jnp.exp(-jnp.abs(u)))
    lw_out[...] = -jnp.exp(-sp - 0.5)
    ua = a0_ref[...] + _bdot(_bdot(mix(5), a1_ref[...]), a2_ref[...])
    a = 1.0 / (1.0 + jnp.exp(-ua))
    a_out[...] = a
    kk_out[...] = k * kk_ref[...]
    k_out[...] = k * (1.0 + (a - 1.0) * ka_ref[...])


def _rwkv_in(x2d, seq, g, mu, w_in, w0, w1, w2, a0, a1, a2, k_k, k_a, tm=256):
    bt, d = x2d.shape
    tiles_per_seq = seq // tm
    row = lambda i: (i, 0)
    prev = lambda i: (jnp.maximum(i * (tm // 8) - 1, 0), 0)
    out_sds = jax.ShapeDtypeStruct((bt, d), jnp.float32)
    return pl.pallas_call(
        functools.partial(_rwkv_in_kernel, tiles_per_seq=tiles_per_seq),
        grid=(bt // tm,),
        in_specs=[pl.BlockSpec((tm, d), row), pl.BlockSpec((8, d), prev),
                  _const_spec((1, d)), _const_spec((6, d)), _const_spec((d, 4 * d)),
                  _const_spec((1, d)), _const_spec((d, DECAY_LORA)), _const_spec((DECAY_LORA, d)),
                  _const_spec((1, d)), _const_spec((d, ICLR_LORA)), _const_spec((ICLR_LORA, d)),
                  _const_spec((1, d)), _const_spec((1, d))],
        out_specs=[pl.BlockSpec((tm, d), row)] * 7,
        out_shape=[out_sds] * 7,
        compiler_params=pltpu.CompilerParams(dimension_semantics=("parallel",),
                                             vmem_limit_bytes=VMEM_LIMIT),
    )(x2d, x2d, g.reshape(1, d), mu, w_in.astype(BF16), w0.reshape(1, d), w1.astype(BF16),
      w2.astype(BF16), a0.reshape(1, d), a1.astype(BF16), a2.astype(BF16), k_k.reshape(1, d),
      k_a.reshape(1, d))


def _split3(x):
    hi = x.astype(BF16)
    r1 = x - hi.astype(x.dtype)
    mid = r1.astype(BF16)
    lo = (r1 - mid.astype(x.dtype)).astype(BF16)
    return hi, mid, lo


def _wkv_pair_chunk(r, lw, k, v, kkraw, a, s_state, rk, lng, lnb):
    c = CHUNK
    f32 = r.dtype
    lane = lax.broadcasted_iota(jnp.int32, (c, LANES), 1)
    m0 = lane < HEAD_DIM

    def head_sum(x):
        s0 = jnp.sum(jnp.where(m0, x, 0.0), axis=-1, keepdims=True)
        s1 = jnp.sum(jnp.where(m0, 0.0, x), axis=-1, keepdims=True)
        return jnp.where(m0, s0, s1)

    def stack(y):
        return jnp.concatenate([jnp.where(m0, y, 0.0), jnp.where(m0, 0.0, y)], axis=0)

    kk = kkraw / jnp.maximum(jnp.sqrt(head_sum(kkraw * kkraw)), 1e-12)
    b = kk * a

    ri = lax.broadcasted_iota(jnp.int32, (c, c), 0)
    ci = lax.broadcasted_iota(jnp.int32, (c, c), 1)
    tril = jnp.where(ci <= ri, 1.0, 0.0).astype(BF16)
    hi, mid, lo = _split3(lw)
    cum = (jnp.dot(tril, hi, preferred_element_type=jnp.float32)
           + jnp.dot(tril, mid, preferred_element_type=jnp.float32)
           + jnp.dot(tril, lo, preferred_element_type=jnp.float32)).astype(f32)
    cum_last = cum[c - 1:c, :]
    g_inv = jnp.exp(-cum)
    g_rem = jnp.exp(cum_last - cum)
    at = -kk * jnp.exp(cum - lw)
    rt = r * jnp.exp(cum)
    kt = k * g_inv
    bt = b * g_inv

    ar = jnp.concatenate([at, rt], axis=0)
    big = _bdot_nt(ar, jnp.concatenate([stack(kt), stack(bt)], axis=0))
    row2 = lax.broadcasted_iota(jnp.int32, (c, 2 * c), 0)
    col2 = lax.broadcasted_iota(jnp.int32, (c, 2 * c), 1)
    col2 = jnp.where(col2 >= c, col2 - c, col2)
    strict = col2 < row2
    incl = col2 <= row2
    a_ak = jnp.where(strict, big[:c, :2 * c], 0.0)
    a_ab = jnp.where(strict, big[:c, 2 * c:], 0.0)
    a_rk = jnp.where(incl, big[c:, :2 * c], 0.0)
    a_rb = jnp.where(incl, big[c:, 2 * c:], 0.0)

    ar_s = _bdot_nt(ar, s_state)
    am, rm = ar_s[:c], ar_s[c:]

    x = a_ab
    t = jnp.where(col2 == row2, 1.0, 0.0) + x
    for _ in range(5):
        x = _bdot(x, stack(x))
        t = t + _bdot(t, stack(x))

    vs = stack(v)
    rhs = am + _bdot(a_ak, vs)
    u = _bdot(t, stack(rhs))
    o = rm + _bdot(jnp.concatenate([a_rk, a_rb], axis=1), jnp.concatenate([vs, stack(u)], axis=0))

    vu = jnp.concatenate([v, u], axis=0)
    kb = jnp.concatenate([k * g_rem, b * g_rem], axis=0)
    upd = _bdot(vu.T, kb)
    rs = lax.broadcasted_iota(jnp.int32, (LANES, LANES), 0) < HEAD_DIM
    cs = lax.broadcasted_iota(jnp.int32, (LANES, LANES), 1) < HEAD_DIM
    new_state = s_state * jnp.exp(cum_last) + jnp.where(rs == cs, upd, 0.0)

    mean = head_sum(o) * (1.0 / HEAD_DIM)
    dev = o - mean
    var = head_sum(dev * dev) * (1.0 / HEAD_DIM)
    o = dev * lax.rsqrt(var + LN_X_EPS) * lng + lnb
    o = o + head_sum(r * k * rk) * v
    return o, new_state


def _wkv_kernel(r_ref, lw_ref, k_ref, v_ref, kk_ref, a_ref, rk_ref, lng_ref, lnb_ref, o_ref,
                state_ref):
    @pl.when(pl.program_id(1) == 0)
    def _():
        state_ref[...] = jnp.zeros_like(state_ref)

    for p in range(N_PAIRS):
        sl = slice(p * LANES, (p + 1) * LANES)
        o, new_state = _wkv_pair_chunk(
            r_ref[:, sl], lw_ref[:, sl], k_ref[:, sl], v_ref[:, sl], kk_ref[:, sl], a_ref[:, sl],
            state_ref[p], rk_ref[:, sl], lng_ref[:, sl], lnb_ref[:, sl])
        o_ref[:, sl] = o
        state_ref[p] = new_state


def _wkv(r, lw, k, v, kkraw, a, seq, r_k, lnx_g, lnx_b):
    bt, d = r.shape
    nb = bt // seq
    nc = seq // CHUNK
    row = lambda b, c: (b * nc + c, 0)
    blk = pl.BlockSpec((CHUNK, d), row)
    return pl.pallas_call(
        _wkv_kernel,
        grid=(nb, nc),
        in_specs=[blk] * 6 + [_const_spec((1, d))] * 3,
        out_specs=blk,
        out_shape=jax.ShapeDtypeStruct((bt, d), jnp.float32),
        scratch_shapes=[pltpu.VMEM((N_PAIRS, LANES, LANES), jnp.float32)],
        compiler_params=pltpu.CompilerParams(dimension_semantics=("parallel", "arbitrary"),
                                             vmem_limit_bytes=VMEM_LIMIT),
    )(r, lw, k, v, kkraw, a, r_k.reshape(1, d), lnx_g.reshape(1, d), lnx_b.reshape(1, d))


def _out_proj_kernel(x_ref, o_ref, z_ref, w_ref, y_ref):
    z = z_ref[...]
    y = o_ref[...] * (z / (1.0 + jnp.exp(-z)))
    y_ref[...] = x_ref[...] + _bdot(y, w_ref[...])


def _out_proj(x2d, o, z, w_out, tm=512):
    bt, d = x2d.shape
    row = lambda i: (i, 0)
    blk = pl.BlockSpec((tm, d), row)
    return pl.pallas_call(
        _out_proj_kernel,
        grid=(bt // tm,),
        in_specs=[blk, blk, blk, _const_spec((d, d))],
        out_specs=blk,
        out_shape=jax.ShapeDtypeStruct((bt, d), jnp.float32),
        compiler_params=pltpu.CompilerParams(dimension_semantics=("parallel",),
                                             vmem_limit_bytes=VMEM_LIMIT),
    )(x2d, o, z, w_out.astype(BF16))


def _rwkv_layer(x2d, seq, g, mu, w_in, w0, w1, w2, a0, a1, a2, k_k, k_a, r_k, lnx_g, lnx_b, w_out):
    r, lw, k, v, kkraw, a, z = _rwkv_in(x2d, seq, g, mu, w_in, w0, w1, w2, a0, a1, a2, k_k, k_a)
    o = _wkv(r, lw, k, v, kkraw, a, seq, r_k, lnx_g, lnx_b)
    return _out_proj(x2d, o, z, w_out)


def _nsa_in_kernel(x_ref, g_ref, w_ref, kg_ref, q_out, kc_out, vc_out, ks_out, vs_out, kw_out,
                   vw_out, z_out, gate_out):
    d = D_MODEL
    h = (_rms(x_ref[...]) * g_ref[...]).astype(BF16)

    def proj(off, width):
        return jnp.dot(h, w_ref[:, off:off + width], preferred_element_type=jnp.float32)

    q_out[...] = proj(0, d)
    off = d
    for ref, gain_row in ((kc_out, None), (vc_out, None), (ks_out, 1), (vs_out, None),
                          (kw_out, 2), (vw_out, None)):
        p = proj(off, KV_WIDTH)
        for g in range(N_KV_GROUPS):
            seg = p[:, g * HEAD_DIM:(g + 1) * HEAD_DIM]
            if gain_row is not None:
                seg = _rms(seg) * kg_ref[gain_row:gain_row + 1, :]
            ref[0, g] = seg.astype(ref.dtype)
        off += KV_WIDTH
    z_out[...] = proj(off, d)
    gate_out[...] = proj(off + d, LANES)


def _nsa_in(x2d, nb, seq, g, w_in, k_gain, tm=256):
    bt, d = x2d.shape
    tps = seq // tm
    width = w_in.shape[1]
    padded = d + 6 * KV_WIDTH + d + LANES
    w = jnp.pad(w_in, ((0, 0), (0, padded - width))).astype(BF16)
    row = lambda i: (i, 0)
    grp = lambda i: (i // tps, 0, i % tps, 0)
    kv_blk = pl.BlockSpec((1, N_KV_GROUPS, tm, HEAD_DIM), grp)
    kv_f32 = jax.ShapeDtypeStruct((nb, N_KV_GROUPS, seq, HEAD_DIM), jnp.float32)
    kv_b16 = jax.ShapeDtypeStruct((nb, N_KV_GROUPS, seq, HEAD_DIM), BF16)
    wide = jax.ShapeDtypeStruct((bt, d), jnp.float32)
    return pl.pallas_call(
        _nsa_in_kernel,
        grid=(bt // tm,),
        in_specs=[pl.BlockSpec((tm, d), row), _const_spec((1, d)), _const_spec((d, padded)),
                  _const_spec((N_BRANCHES, HEAD_DIM))],
        out_specs=[pl.BlockSpec((tm, d), row)] + [kv_blk] * 6
                  + [pl.BlockSpec((tm, d), row), pl.BlockSpec((tm, LANES), row)],
        out_shape=[wide, kv_f32, kv_f32, kv_b16, kv_b16, kv_b16, kv_b16, wide,
                   jax.ShapeDtypeStruct((bt, LANES), jnp.float32)],
        compiler_params=pltpu.CompilerParams(dimension_semantics=("parallel",),
                                             vmem_limit_bytes=VMEM_LIMIT),
    )(x2d, g.reshape(1, d), w, k_gain)


def _gelu_tanh(x):
    return 0.5 * x * (1.0 + jnp.tanh(0.7978845608028654 * (x + 0.044715 * (x * x * x))))


def _compress_kernel(uk_ref, uv_ref, pe_ref, w1_ref, b1_ref, w2_ref, kg_ref, kc_out, vc_out):
    half = CMP_STRIDE * HEAD_DIM
    for kv, (u_ref, out) in enumerate(((uk_ref, kc_out), (uv_ref, vc_out))):
        uh = u_ref[0, 0]
        rows = uh.shape[0]
        ha = _bdot(uh + pe_ref[kv, 0:1, :], w1_ref[kv, :half, :])
        hb = _bdot(uh + pe_ref[kv, 1:2, :], w1_ref[kv, half:, :])
        hid = _gelu_tanh(ha + pltpu.roll(hb, rows - 1, axis=0) + b1_ref[kv])
        y = _bdot(hid, w2_ref[kv])
        if kv == 0:
            y = _rms(y) * kg_ref[0:1, :]
        out[0, 0] = y.astype(out.dtype)


def _compress(kc_raw, vc_raw, pe, w1, b1, w2, k_gain):
    nb, ng, seq, n = kc_raw.shape
    rows = seq // CMP_STRIDE
    half = CMP_STRIDE * n
    uk = kc_raw.reshape(nb, ng, rows, half)
    uv = vc_raw.reshape(nb, ng, rows, half)
    u_blk = pl.BlockSpec((1, 1, rows, half), lambda b, g: (b, g, 0, 0))
    o_blk = pl.BlockSpec((1, 1, rows, n), lambda b, g: (b, g, 0, 0))
    o_sds = jax.ShapeDtypeStruct((nb, ng, rows, n), BF16)
    return pl.pallas_call(
        _compress_kernel,
        grid=(nb, ng),
        in_specs=[u_blk, u_blk, _const_spec((2, 2, half)), _const_spec((2, 2 * half, CMP_HIDDEN)),
                  _const_spec((2, 1, CMP_HIDDEN)), _const_spec((2, CMP_HIDDEN, n)),
                  _const_spec((N_BRANCHES, n))],
        out_specs=[o_blk, o_blk],
        out_shape=[o_sds, o_sds],
        compiler_params=pltpu.CompilerParams(dimension_semantics=("parallel", "parallel"),
                                             vmem_limit_bytes=VMEM_LIMIT),
    )(uk, uv, pe.reshape(2, 2, half), w1.astype(BF16), b1.reshape(2, 1, CMP_HIDDEN),
      w2.astype(BF16), k_gain)


def _softmax_step(s, mask, v_tile, carry):
    m, l, acc = carry
    m_new = jnp.maximum(m, jnp.max(jnp.where(mask, s, NEG_INF), axis=-1, keepdims=True))
    p = jnp.where(mask, jnp.exp(s - m_new), 0.0)
    alpha = jnp.exp(m - m_new)
    l = alpha * l + jnp.sum(p, axis=-1, keepdims=True)
    acc = alpha * acc + jnp.dot(p.astype(BF16), v_tile, preferred_element_type=jnp.float32)
    return m_new, l, acc


def _nsa_attn_kernel(q_ref, gate_ref, qg_ref, kc_ref, vc_ref, ks_ref, vs_ref, kw_ref, vw_ref,
                     o_ref, *, qt, kt, seq):
    hg = HEADS_PER_GROUP
    gi = pl.program_id(1)
    t0 = pl.program_id(2) * qt
    n_slc = seq // SLC_BLOCK
    n_cmp_rows = kc_ref.shape[2]
    rows = hg * qt

    q = q_ref[...]
    qs = jnp.concatenate(
        [_rms(q[:, h * HEAD_DIM:(h + 1) * HEAD_DIM]) * qg_ref[...] * (HEAD_DIM ** -0.5)
         for h in range(hg)], axis=0).astype(BF16)
    t_rows = t0 + lax.broadcasted_iota(jnp.int32, (qt, 1), 0)
    t_stack = jnp.concatenate([t_rows] * hg, axis=0)

    s = _bdot_nt(qs, kc_ref[0, 0])
    cmp_end = lax.broadcasted_iota(jnp.int32, (1, n_cmp_rows), 1) * CMP_STRIDE + (CMP_BLOCK - 1)
    cmask = cmp_end <= t_stack
    m = jnp.max(jnp.where(cmask, s, NEG_INF), axis=-1, keepdims=True)
    p = jnp.where(cmask, jnp.exp(s - m), 0.0)
    denom = jnp.sum(p, axis=-1, keepdims=True)
    p = p / jnp.maximum(denom, 1e-30)
    o_cmp = jnp.dot(p.astype(BF16), vc_ref[0, 0], preferred_element_type=jnp.float32)

    p_sum = p[0:qt]
    for h in range(1, hg):
        p_sum = p_sum + p[h * qt:(h + 1) * qt]
    jn = lax.broadcasted_iota(jnp.int32, (n_slc, n_cmp_rows), 0) * SLC_BLOCK
    nn = lax.broadcasted_iota(jnp.int32, (n_slc, n_cmp_rows), 1) * CMP_STRIDE
    ov_t = jnp.where((nn < jn + SLC_BLOCK) & (nn + (CMP_BLOCK - 1) >= jn), 1.0, 0.0).astype(BF16)
    p_hi = p_sum.astype(BF16)
    p_lo = (p_sum - p_hi.astype(p_sum.dtype)).astype(BF16)
    imp_t = (lax.dot_general(ov_t, p_hi, (((1,), (1,)), ((), ())), preferred_element_type=jnp.float32)
             + lax.dot_general(ov_t, p_lo, (((1,), (1,)), ((), ())), preferred_element_type=jnp.float32))

    jb = lax.broadcasted_iota(jnp.int32, (n_slc, qt), 0)
    tq = t0 + lax.broadcasted_iota(jnp.int32, (n_slc, qt), 1)
    dist = jnp.right_shift(tq, SLC_SHIFT) - jb
    forced = (jb == 0) | ((dist >= 0) & (dist < N_LOCAL_BLOCKS))
    score = jnp.where(dist >= 0, imp_t + jnp.where(forced, FORCE_BONUS, 0.0), -jnp.inf)
    cnt = jnp.zeros((n_slc, qt), jnp.float32)
    for j2 in range(n_slc):
        row = score[j2:j2 + 1, :]
        cnt = cnt + jnp.where(jb > j2, jnp.where(row >= score, 1.0, 0.0),
                              jnp.where(row > score, 1.0, 0.0))
    sel_t = jnp.where((cnt < min(SLC_TOPK, n_slc)) & (dist >= 0), 1.0, 0.0)
    sel = sel_t.T.astype(BF16)

    init = (jnp.full((rows, 1), NEG_INF, jnp.float32), jnp.zeros((rows, 1), jnp.float32),
            jnp.zeros((rows, HEAD_DIM), jnp.float32))

    def slc_body(i, carry):
        k0 = pl.multiple_of(i * kt, kt)
        s = _bdot_nt(qs, ks_ref[0, 0, pl.ds(k0, kt), :])
        key = k0 + lax.broadcasted_iota(jnp.int32, (1, kt), 1)
        blk = lax.broadcasted_iota(jnp.int32, (n_slc, kt), 0)
        kblk = jnp.right_shift(k0 + lax.broadcasted_iota(jnp.int32, (n_slc, kt), 1), SLC_SHIFT)
        expand = jnp.where(kblk == blk, 1.0, 0.0).astype(BF16)
        picked = jnp.dot(sel, expand, preferred_element_type=jnp.float32)
        mask = (picked > 0.5) & (key <= t_rows)
        mask = jnp.concatenate([mask] * hg, axis=0)
        return _softmax_step(s, mask, vs_ref[0, 0, pl.ds(k0, kt), :], carry)

    _, l, acc = lax.fori_loop(0, (t0 + qt + kt - 1) // kt, slc_body, init)
    o_slc = acc / l

    def win_body(i, carry):
        k0 = pl.multiple_of(i * kt, kt)
        s = _bdot_nt(qs, kw_ref[0, 0, pl.ds(k0, kt), :])
        key = k0 + lax.broadcasted_iota(jnp.int32, (1, kt), 1)
        lag = t_stack - key
        mask = (lag >= 0) & (lag < WINDOW)
        return _softmax_step(s, mask, vw_ref[0, 0, pl.ds(k0, kt), :], carry)

    first = jnp.maximum(t0 - (WINDOW - 1), 0) // kt
    _, l, acc = lax.fori_loop(first, (t0 + qt + kt - 1) // kt, win_body, init)
    o_win = acc / l

    gates = 1.0 / (1.0 + jnp.exp(-gate_ref[...]))
    gates = pltpu.roll(gates, (LANES - hg * gi) % LANES, axis=1)
    for h in range(hg):
        rs = slice(h * qt, (h + 1) * qt)
        o = (gates[:, h:h + 1] * o_cmp[rs]
             + gates[:, N_HEADS + h:N_HEADS + h + 1] * o_slc[rs]
             + gates[:, 2 * N_HEADS + h:2 * N_HEADS + h + 1] * o_win[rs])
        o_ref[:, h * HEAD_DIM:(h + 1) * HEAD_DIM] = o


def _nsa_attn(q, gate, q_gain, kc, vc, ks, vs, kw, vw, qt=128, kt=256):
    bt, d = q.shape
    nb, ng, seq, n = ks.shape
    kt = min(kt, seq)
    nq = seq // qt
    width = HEADS_PER_GROUP * n
    rows_c = kc.shape[2]
    q_blk = pl.BlockSpec((qt, width), lambda b, g, i: (b * nq + i, g))
    g_blk = pl.BlockSpec((qt, LANES), lambda b, g, i: (b * nq + i, 0))
    c_blk = pl.BlockSpec((1, 1, rows_c, n), lambda b, g, i: (b, g, 0, 0))
    k_blk = pl.BlockSpec((1, 1, seq, n), lambda b, g, i: (b, g, 0, 0))
    return pl.pallas_call(
        functools.partial(_nsa_attn_kernel, qt=qt, kt=kt, seq=seq),
        grid=(nb, ng, nq),
        in_specs=[q_blk, g_blk, _const_spec((1, n)), c_blk, c_blk, k_blk, k_blk, k_blk, k_blk],
        out_specs=q_blk,
        out_shape=jax.ShapeDtypeStruct((bt, d), jnp.float32),
        compiler_params=pltpu.CompilerParams(
            dimension_semantics=("parallel", "parallel", "arbitrary"),
            vmem_limit_bytes=VMEM_LIMIT),
    )(q, gate, q_gain.reshape(1, n), kc, vc, ks, vs, kw, vw)


def _nsa_layer(x2d, nb, seq, g, w_in, q_gain, k_gain, cmp_pe, cmp_w1, cmp_b1, cmp_w2, w_out):
    q, kc_raw, vc_raw, ks, vs, kw, vw, z, gate = _nsa_in(x2d, nb, seq, g, w_in, k_gain)
    kc, vc = _compress(kc_raw, vc_raw, cmp_pe, cmp_w1, cmp_b1, cmp_w2, k_gain)
    o = _nsa_attn(q, gate, q_gain, kc, vc, ks, vs, kw, vw)
    return _out_proj(x2d, o, z, w_out)


def kernel(x, norm_g, rwkv_mu, rwkv_w_in, rwkv_w0, rwkv_w1, rwkv_w2, rwkv_a0, rwkv_a1, rwkv_a2, rwkv_k_k, rwkv_k_a, rwkv_r_k, rwkv_lnx_g, rwkv_lnx_b, rwkv_w_out, nsa_w_in, nsa_q_gain, nsa_k_gain, nsa_cmp_pe, nsa_cmp_w1, nsa_cmp_b1, nsa_cmp_w2, nsa_w_out):
    b, t, d = x.shape
    x2d = x.reshape(b * t, d)
    x2d = _rwkv_layer(x2d, t, norm_g[0], rwkv_mu[0], rwkv_w_in[0], rwkv_w0[0], rwkv_w1[0],
                      rwkv_w2[0], rwkv_a0[0], rwkv_a1[0], rwkv_a2[0], rwkv_k_k[0], rwkv_k_a[0],
                      rwkv_r_k[0].reshape(-1), rwkv_lnx_g[0], rwkv_lnx_b[0], rwkv_w_out[0])
    x2d = _nsa_layer(x2d, b, t, norm_g[1], nsa_w_in[0], nsa_q_gain[0], nsa_k_gain[0],
                     nsa_cmp_pe[0], nsa_cmp_w1[0], nsa_cmp_b1[0], nsa_cmp_w2[0], nsa_w_out[0])
    return x2d.reshape(b, t, d)
```

```python
import functools

import jax
import jax.numpy as jnp
from jax import lax
from jax.experimental import pallas as pl
from jax.experimental.pallas import tpu as pltpu

D_MODEL = 1024
HEAD_DIM = 64
N_HEADS = D_MODEL // HEAD_DIM
NORM_EPS = 1e-6
LN_X_EPS = 64e-5
DECAY_LORA = 64
ICLR_LORA = 64
N_KV_GROUPS = 4
HEADS_PER_GROUP = N_HEADS // N_KV_GROUPS
KV_WIDTH = N_KV_GROUPS * HEAD_DIM
N_BRANCHES = 3
CMP_BLOCK = 32
CMP_STRIDE = 16
CMP_HIDDEN = 256
SLC_BLOCK = 64
SLC_SHIFT = 6
SLC_TOPK = 16
N_LOCAL_BLOCKS = 2
WINDOW = 512
FORCE_BONUS = 1e4
NEG_INF = -1e30

LANES = 128
PAIR = LANES // HEAD_DIM
N_PAIRS = N_HEADS // PAIR
CHUNK = 64
CHUNK_SHIFT = 6
VMEM_LIMIT = 48 * 1024 * 1024

BF16 = jnp.bfloat16


def _rms(x, eps=NORM_EPS):
    return x * lax.rsqrt(jnp.mean(x * x, axis=-1, keepdims=True) + eps)


def _bdot(a, b):
    return jnp.dot(a.astype(BF16), b.astype(BF16), preferred_element_type=jnp.float32)


def _bdot_nt(a, b):
    return lax.dot_general(a.astype(BF16), b.astype(BF16), (((1,), (1,)), ((), ())),
                           preferred_element_type=jnp.float32)


def _const_spec(shape):
    nd = len(shape)
    return pl.BlockSpec(shape, lambda *_: (0,) * nd)


def _rwkv_in_kernel(x_ref, xp_ref, g_ref, mu_ref, win_ref, w0_ref, w1_ref, w2_ref, a0_ref,
                    a1_ref, a2_ref, kk_ref, ka_ref,
                    r_out, lw_out, cum_out, k_out, v_out, kk_out, a_out, z_out, *, tiles_per_seq):
    d = D_MODEL
    g = g_ref[...]
    h = _rms(x_ref[...]) * g
    hp = _rms(xp_ref[7:8, :]) * g
    hp = jnp.where(pl.program_id(0) % tiles_per_seq == 0, 0.0, hp)
    row = lax.broadcasted_iota(jnp.int32, h.shape, 0)
    hs = jnp.where(row == 0, hp, pltpu.roll(h, 1, axis=0))
    dh = hs - h

    def mix(c):
        return h + dh * mu_ref[c:c + 1, :]

    r_out[...] = _bdot(mix(0), win_ref[:, 0 * d:1 * d])
    k = _bdot(mix(1), win_ref[:, 1 * d:2 * d])
    v_out[...] = _bdot(mix(2), win_ref[:, 2 * d:3 * d])
    z_out[...] = _bdot(mix(3), win_ref[:, 3 * d:4 * d])

    u = w0_ref[...] + _bdot(jnp.tanh(_bdot(mix(4), w1_ref[...])), w2_ref[...])
    sp = jnp.maximum(-u, 0.0) + jnp.log(1.0 + jnp.exp(-jnp.abs(u)))
    lw = -jnp.exp(-sp - 0.5)
    lw_out[...] = lw
    tm = lw.shape[0]
    ri = lax.broadcasted_iota(jnp.int32, (tm, tm), 0)
    ci = lax.broadcasted_iota(jnp.int32, (tm, tm), 1)
    same_chunk = jnp.right_shift(ri, CHUNK_SHIFT) == jnp.right_shift(ci, CHUNK_SHIFT)
    tril = jnp.where(ci <= ri, jnp.where(same_chunk, 1.0, 0.0), 0.0).astype(BF16)
    hi, mid, lo = _split3(lw)
    cum_out[...] = (jnp.dot(tril, hi, preferred_element_type=jnp.float32)
                    + jnp.dot(tril, mid, preferred_element_type=jnp.float32)
                    + jnp.dot(tril, lo, preferred_element_type=jnp.float32))
    ua = a0_ref[...] + _bdot(_bdot(mix(5), a1_ref[...]), a2_ref[...])
    a = 1.0 / (1.0 + jnp.exp(-ua))
    a_out[...] = a
    kk_out[...] = k * kk_ref[...]
    k_out[...] = k * (1.0 + (a - 1.0) * ka_ref[...])


def _rwkv_in(x2d, seq, g, mu, w_in, w0, w1, w2, a0, a1, a2, k_k, k_a, tm=256):
    bt, d = x2d.shape
    tiles_per_seq = seq // tm
    row = lambda i: (i, 0)
    prev = lambda i: (jnp.maximum(i * (tm // 8) - 1, 0), 0)
    out_sds = jax.ShapeDtypeStruct((bt, d), jnp.float32)
    return pl.pallas_call(
        functools.partial(_rwkv_in_kernel, tiles_per_seq=tiles_per_seq),
        grid=(bt // tm,),
        in_specs=[pl.BlockSpec((tm, d), row), pl.BlockSpec((8, d), prev),
                  _const_spec((1, d)), _const_spec((6, d)), _const_spec((d, 4 * d)),
                  _const_spec((1, d)), _const_spec((d, DECAY_LORA)), _const_spec((DECAY_LORA, d)),
                  _const_spec((1, d)), _const_spec((d, ICLR_LORA)), _const_spec((ICLR_LORA, d)),
                  _const_spec((1, d)), _const_spec((1, d))],
        out_specs=[pl.BlockSpec((tm, d), row)] * 8,
        out_shape=[out_sds] * 8,
        compiler_params=pltpu.CompilerParams(dimension_semantics=("parallel",),
                                             vmem_limit_bytes=VMEM_LIMIT),
    )(x2d, x2d, g.reshape(1, d), mu, w_in.astype(BF16), w0.reshape(1, d), w1.astype(BF16),
      w2.astype(BF16), a0.reshape(1, d), a1.astype(BF16), a2.astype(BF16), k_k.reshape(1, d),
      k_a.reshape(1, d))


def _split3(x):
    hi = x.astype(BF16)
    r1 = x - hi.astype(x.dtype)
    mid = r1.astype(BF16)
    lo = (r1 - mid.astype(x.dtype)).astype(BF16)
    return hi, mid, lo


def _wkv_kernel(r_ref, lw_ref, cum_ref, k_ref, v_ref, kk_ref, a_ref, rk_ref, lng_ref, lnb_ref,
                o_ref, state_ref):
    @pl.when(pl.program_id(1) == 0)
    def _():
        state_ref[...] = jnp.zeros_like(state_ref)

    c = CHUNK
    pairs = range(N_PAIRS)
    sls = [slice(p * LANES, (p + 1) * LANES) for p in pairs]
    m0 = lax.broadcasted_iota(jnp.int32, (c, LANES), 1) < HEAD_DIM

    def head_sum(x):
        s0 = jnp.sum(jnp.where(m0, x, 0.0), axis=-1, keepdims=True)
        s1 = jnp.sum(jnp.where(m0, 0.0, x), axis=-1, keepdims=True)
        return jnp.where(m0, s0, s1)

    def stack(y):
        return jnp.concatenate([jnp.where(m0, y, 0.0), jnp.where(m0, 0.0, y)], axis=0)

    def each(f, *lists):
        return [f(*xs) for xs in zip(*lists)]

    r = [r_ref[:, s] for s in sls]
    k = [k_ref[:, s] for s in sls]
    v = [v_ref[:, s] for s in sls]
    cum = [cum_ref[:, s] for s in sls]
    state = [state_ref[p] for p in pairs]

    def prep(p):
        kkraw = kk_ref[:, sls[p]]
        kk = kkraw / jnp.maximum(jnp.sqrt(head_sum(kkraw * kkraw)), 1e-12)
        b = kk * a_ref[:, sls[p]]
        g_inv = jnp.exp(-cum[p])
        at = -kk * jnp.exp(cum[p] - lw_ref[:, sls[p]])
        rt = r[p] * jnp.exp(cum[p])
        ar = jnp.concatenate([at, rt], axis=0)
        kb_t = jnp.concatenate([stack(k[p] * g_inv), stack(b * g_inv)], axis=0)
        return b, ar, kb_t

    b, ar, kb_t = zip(*[prep(p) for p in pairs])
    big = each(_bdot_nt, ar, kb_t)
    ar_s = each(_bdot_nt, ar, state)

    row2 = lax.broadcasted_iota(jnp.int32, (c, 2 * c), 0)
    col2 = lax.broadcasted_iota(jnp.int32, (c, 2 * c), 1)
    col2 = jnp.where(col2 >= c, col2 - c, col2)
    strict = col2 < row2
    incl = col2 <= row2
    eye2 = jnp.where(col2 == row2, 1.0, 0.0)

    x = [jnp.where(strict, g[:c, 2 * c:], 0.0) for g in big]
    t = [eye2 + xi for xi in x]
    for _ in range(5):
        x = each(lambda xi: _bdot(xi, stack(xi)), x)
        t = each(lambda ti, xi: ti + _bdot(ti, stack(xi)), t, x)

    vs = each(stack, v)
    rhs = each(lambda g, s, vsi: s[:c] + _bdot(jnp.where(strict, g[:c, :2 * c], 0.0), vsi),
               big, ar_s, vs)
    u = each(lambda ti, ri: _bdot(ti, stack(ri)), t, rhs)
    o = each(lambda g, s, vsi, ui: s[c:] + _bdot(
        jnp.where(jnp.concatenate([incl, incl], axis=1), g[c:], 0.0),
        jnp.concatenate([vsi, stack(ui)], axis=0)), big, ar_s, vs, u)

    rs = lax.broadcasted_iota(jnp.int32, (LANES, LANES), 0) < HEAD_DIM
    cs = lax.broadcasted_iota(jnp.int32, (LANES, LANES), 1) < HEAD_DIM
    diag = rs == cs
    for p in pairs:
        cum_last = cum[p][c - 1:c, :]
        g_rem = jnp.exp(cum_last - cum[p])
        vu = jnp.concatenate([v[p], u[p]], axis=0)
        kb = jnp.concatenate([k[p] * g_rem, b[p] * g_rem], axis=0)
        upd = _bdot(vu.T, kb)
        state_ref[p] = state[p] * jnp.exp(cum_last) + jnp.where(diag, upd, 0.0)

    for p in pairs:
        mean = head_sum(o[p]) * (1.0 / HEAD_DIM)
        dev = o[p] - mean
        var = head_sum(dev * dev) * (1.0 / HEAD_DIM)
        y = dev * lax.rsqrt(var + LN_X_EPS) * lng_ref[:, sls[p]] + lnb_ref[:, sls[p]]
        o_ref[:, sls[p]] = y + head_sum(r[p] * k[p] * rk_ref[:, sls[p]]) * v[p]


def _wkv(r, lw, cum, k, v, kkraw, a, seq, r_k, lnx_g, lnx_b):
    bt, d = r.shape
    nb = bt // seq
    nc = seq // CHUNK
    row = lambda b, c: (b * nc + c, 0)
    blk = pl.BlockSpec((CHUNK, d), row)
    return pl.pallas_call(
        _wkv_kernel,
        grid=(nb, nc),
        in_specs=[blk] * 7 + [_const_spec((1, d))] * 3,
        out_specs=blk,
        out_shape=jax.ShapeDtypeStruct((bt, d), jnp.float32),
        scratch_shapes=[pltpu.VMEM((N_PAIRS, LANES, LANES), jnp.float32)],
        compiler_params=pltpu.CompilerParams(dimension_semantics=("parallel", "arbitrary"),
                                             vmem_limit_bytes=VMEM_LIMIT),
    )(r, lw, cum, k, v, kkraw, a, r_k.reshape(1, d), lnx_g.reshape(1, d), lnx_b.reshape(1, d))


def _out_proj_kernel(x_ref, o_ref, z_ref, w_ref, y_ref):
    z = z_ref[...]
    y = o_ref[...] * (z / (1.0 + jnp.exp(-z)))
    y_ref[...] = x_ref[...] + _bdot(y, w_ref[...])


def _out_proj(x2d, o, z, w_out, tm=512):
    bt, d = x2d.shape
    row = lambda i: (i, 0)
    blk = pl.BlockSpec((tm, d), row)
    return pl.pallas_call(
        _out_proj_kernel,
        grid=(bt // tm,),
        in_specs=[blk, blk, blk, _const_spec((d, d))],
        out_specs=blk,
        out_shape=jax.ShapeDtypeStruct((bt, d), jnp.float32),
        compiler_params=pltpu.CompilerParams(dimension_semantics=("parallel",),
                                             vmem_limit_bytes=VMEM_LIMIT),
    )(x2d, o, z, w_out.astype(BF16))


def _rwkv_layer(x2d, seq, g, mu, w_in, w0, w1, w2, a0, a1, a2, k_k, k_a, r_k, lnx_g, lnx_b, w_out):
    r, lw, cum, k, v, kkraw, a, z = _rwkv_in(x2d, seq, g, mu, w_in, w0, w1, w2, a0, a1, a2, k_k,
                                             k_a)
    o = _wkv(r, lw, cum, k, v, kkraw, a, seq, r_k, lnx_g, lnx_b)
    return _out_proj(x2d, o, z, w_out)


def _nsa_in_kernel(x_ref, g_ref, w_ref, kg_ref, q_out, kc_out, vc_out, ks_out, vs_out, kw_out,
                   vw_out, z_out, gate_out, *, tiles_per_seq):
    d = D_MODEL
    h = (_rms(x_ref[...]) * g_ref[...]).astype(BF16)
    tm = h.shape[0]

    def proj(off, width):
        return jnp.dot(h, w_ref[:, off:off + width], preferred_element_type=jnp.float32)

    lane = lax.broadcasted_iota(jnp.int32, (tm, LANES), 1)
    low_lanes = lane < HEAD_DIM
    tok = (pl.program_id(0) % tiles_per_seq) * tm + lax.broadcasted_iota(jnp.int32, (tm, LANES), 0)
    blk_bias = jnp.where(lane - HEAD_DIM == jnp.right_shift(tok, SLC_SHIFT), NEG_INF, 0.0)

    def slot(p, g):
        two = p[:, (g // PAIR) * LANES:(g // PAIR + 1) * LANES]
        return pltpu.roll(two, HEAD_DIM, axis=1) if g % PAIR else two

    def narrow(ref, p, gain_row):
        for g in range(N_KV_GROUPS):
            seg = p[:, g * HEAD_DIM:(g + 1) * HEAD_DIM]
            if gain_row is not None:
                seg = _rms(seg) * kg_ref[gain_row:gain_row + 1, :HEAD_DIM]
            ref[0, g] = seg.astype(ref.dtype)

    def values(ref, p):
        for g in range(N_KV_GROUPS):
            ref[0, g] = jnp.where(low_lanes, slot(p, g), 1.0).astype(ref.dtype)

    q_out[...] = proj(0, d)
    narrow(kc_out, proj(d, KV_WIDTH), None)
    narrow(vc_out, proj(d + KV_WIDTH, KV_WIDTH), None)
    p = proj(d + 2 * KV_WIDTH, KV_WIDTH)
    for g in range(N_KV_GROUPS):
        x = jnp.where(low_lanes, slot(p, g), 0.0)
        ms = jnp.sum(x * x, axis=-1, keepdims=True) * (1.0 / HEAD_DIM)
        ks_out[0, g] = (x * lax.rsqrt(ms + NORM_EPS) * kg_ref[1:2, :] + blk_bias).astype(BF16)
    values(vs_out, proj(d + 3 * KV_WIDTH, KV_WIDTH))
    narrow(kw_out, proj(d + 4 * KV_WIDTH, KV_WIDTH), 2)
    values(vw_out, proj(d + 5 * KV_WIDTH, KV_WIDTH))
    z_out[...] = proj(d + 6 * KV_WIDTH, d)
    gate_out[...] = proj(2 * d + 6 * KV_WIDTH, LANES)


def _nsa_in(x2d, nb, seq, g, w_in, k_gain, tm=256):
    bt, d = x2d.shape
    tps = seq // tm
    width = w_in.shape[1]
    padded = d + 6 * KV_WIDTH + d + LANES
    w = jnp.pad(w_in, ((0, 0), (0, padded - width))).astype(BF16)
    row = lambda i: (i, 0)
    grp = lambda i: (i // tps, 0, i % tps, 0)
    kv_blk = pl.BlockSpec((1, N_KV_GROUPS, tm, HEAD_DIM), grp)
    v_blk = pl.BlockSpec((1, N_KV_GROUPS, tm, LANES), grp)
    kv_f32 = jax.ShapeDtypeStruct((nb, N_KV_GROUPS, seq, HEAD_DIM), jnp.float32)
    k_b16 = jax.ShapeDtypeStruct((nb, N_KV_GROUPS, seq, HEAD_DIM), BF16)
    v_b16 = jax.ShapeDtypeStruct((nb, N_KV_GROUPS, seq, LANES), BF16)
    wide = jax.ShapeDtypeStruct((bt, d), jnp.float32)
    assert seq // SLC_BLOCK <= HEAD_DIM
    return pl.pallas_call(
        functools.partial(_nsa_in_kernel, tiles_per_seq=tps),
        grid=(bt // tm,),
        in_specs=[pl.BlockSpec((tm, d), row), _const_spec((1, d)), _const_spec((d, padded)),
                  _const_spec((N_BRANCHES, LANES))],
        out_specs=[pl.BlockSpec((tm, d), row), kv_blk, kv_blk, v_blk, v_blk, kv_blk, v_blk,
                   pl.BlockSpec((tm, d), row), pl.BlockSpec((tm, LANES), row)],
        out_shape=[wide, kv_f32, kv_f32, v_b16, v_b16, k_b16, v_b16, wide,
                   jax.ShapeDtypeStruct((bt, LANES), jnp.float32)],
        compiler_params=pltpu.CompilerParams(dimension_semantics=("parallel",),
                                             vmem_limit_bytes=VMEM_LIMIT),
    )(x2d, g.reshape(1, d), w, jnp.pad(k_gain, ((0, 0), (0, LANES - HEAD_DIM))))


def _gelu_tanh(x):
    return 0.5 * x * (1.0 + jnp.tanh(0.7978845608028654 * (x + 0.044715 * (x * x * x))))


def _compress_kernel(uk_ref, uv_ref, pe_ref, w1_ref, b1_ref, w2_ref, kg_ref, kc_out, vc_out):
    half = CMP_STRIDE * HEAD_DIM
    for kv, (u_ref, out) in enumerate(((uk_ref, kc_out), (uv_ref, vc_out))):
        uh = u_ref[0, 0]
        rows = uh.shape[0]
        ha = _bdot(uh + pe_ref[kv, 0:1, :], w1_ref[kv, :half, :])
        hb = _bdot(uh + pe_ref[kv, 1:2, :], w1_ref[kv, half:, :])
        hid = _gelu_tanh(ha + pltpu.roll(hb, rows - 1, axis=0) + b1_ref[kv])
        y = _bdot(hid, w2_ref[kv])
        if kv == 0:
            y = y[:, :HEAD_DIM]
            out[0, 0] = (_rms(y) * kg_ref[0:1, :]).astype(out.dtype)
        else:
            low_lanes = lax.broadcasted_iota(jnp.int32, y.shape, 1) < HEAD_DIM
            out[0, 0] = jnp.where(low_lanes, y, 1.0).astype(out.dtype)


def _compress(kc_raw, vc_raw, pe, w1, b1, w2, k_gain):
    nb, ng, seq, n = kc_raw.shape
    rows = seq // CMP_STRIDE
    half = CMP_STRIDE * n
    uk = kc_raw.reshape(nb, ng, rows, half)
    uv = vc_raw.reshape(nb, ng, rows, half)
    whole = lambda b, g: (b, g, 0, 0)
    u_blk = pl.BlockSpec((1, 1, rows, half), whole)
    w2p = jnp.pad(w2, ((0, 0), (0, 0), (0, LANES - n))).astype(BF16)
    return pl.pallas_call(
        _compress_kernel,
        grid=(nb, ng),
        in_specs=[u_blk, u_blk, _const_spec((2, 2, half)), _const_spec((2, 2 * half, CMP_HIDDEN)),
                  _const_spec((2, 1, CMP_HIDDEN)), _const_spec((2, CMP_HIDDEN, LANES)),
                  _const_spec((N_BRANCHES, n))],
        out_specs=[pl.BlockSpec((1, 1, rows, n), whole), pl.BlockSpec((1, 1, rows, LANES), whole)],
        out_shape=[jax.ShapeDtypeStruct((nb, ng, rows, n), BF16),
                   jax.ShapeDtypeStruct((nb, ng, rows, LANES), BF16)],
        compiler_params=pltpu.CompilerParams(dimension_semantics=("parallel", "parallel"),
                                             vmem_limit_bytes=VMEM_LIMIT),
    )(uk, uv, pe.reshape(2, 2, half), w1.astype(BF16), b1.reshape(2, 1, CMP_HIDDEN), w2p, k_gain)


M_INIT = -1e20


LOG2_E = 1.4426950408889634


def _softmax_step(s, bias, v_tile, carry, hg, qt):
    m, acc = carry
    sb = [s[h * qt:(h + 1) * qt] for h in range(hg)]
    if bias is not None:
        sb = [x + bias for x in sb]
    m_new = jnp.maximum(m, jnp.concatenate(
        [jnp.max(x, axis=-1, keepdims=True) for x in sb], axis=0))
    p = jnp.concatenate(
        [jnp.exp2(sb[h] - m_new[h * qt:(h + 1) * qt]).astype(BF16) for h in range(hg)], axis=0)
    acc = jnp.exp2(m - m_new) * acc + jnp.dot(p, v_tile, preferred_element_type=jnp.float32)
    return m_new, acc


def _nsa_attn_kernel(q_ref, gate_ref, bcast_ref, qg_ref, kc_ref, vc_ref, ks_ref, vs_ref, kw_ref,
                     vw_ref, o_ref, *, qt, kt, seq):
    hg = HEADS_PER_GROUP
    t0 = pl.program_id(2) * qt
    n_slc = seq // SLC_BLOCK
    n_cmp_rows = kc_ref.shape[2]
    rows = hg * qt
    low_lanes = lax.broadcasted_iota(jnp.int32, (qt, LANES), 1) < HEAD_DIM

    def swap_halves(x):
        return pltpu.roll(x, HEAD_DIM, axis=1)

    q = q_ref[...]
    q_n = []
    for h in range(hg):
        two = q[:, (h // PAIR) * LANES:(h // PAIR + 1) * LANES]
        x = jnp.where(low_lanes, swap_halves(two) if h % PAIR else two, 0.0)
        ms = jnp.sum(x * x, axis=-1, keepdims=True) * (1.0 / HEAD_DIM)
        q_n.append(x * lax.rsqrt(ms + NORM_EPS) * (qg_ref[...] * (LOG2_E * HEAD_DIM ** -0.5)))
    qs = jnp.concatenate([x[:, :HEAD_DIM] for x in q_n], axis=0).astype(BF16)
    t_rows = t0 + lax.broadcasted_iota(jnp.int32, (qt, 1), 0)

    s = _bdot_nt(qs, kc_ref[0, 0])
    cmp_end = lax.broadcasted_iota(jnp.int32, (1, n_cmp_rows), 1) * CMP_STRIDE + (CMP_BLOCK - 1)
    cbias = jnp.where(cmp_end <= t_rows, 0.0, NEG_INF)
    p_heads = []
    for h in range(hg):
        sb = s[h * qt:(h + 1) * qt] + cbias
        e = jnp.exp2(sb - jnp.maximum(jnp.max(sb, axis=-1, keepdims=True), M_INIT))
        p_heads.append(e / jnp.maximum(jnp.sum(e, axis=-1, keepdims=True), 1e-30))
    o_cmp = jnp.dot(jnp.concatenate([p.astype(BF16) for p in p_heads], axis=0), vc_ref[0, 0],
                    preferred_element_type=jnp.float32)

    span = WINDOW + qt
    w0 = pl.multiple_of(jnp.maximum(t0 - WINDOW, 0), qt)
    s = _bdot_nt(qs, kw_ref[0, 0, pl.ds(w0, span), :])
    lag = t_rows - (w0 + lax.broadcasted_iota(jnp.int32, (1, span), 1))
    wbias = jnp.where(lag >= 0, jnp.where(lag < WINDOW, 0.0, NEG_INF), NEG_INF)
    p_win = []
    for h in range(hg):
        sb = s[h * qt:(h + 1) * qt] + wbias
        p_win.append(jnp.exp2(sb - jnp.max(sb, axis=-1, keepdims=True)).astype(BF16))
    acc_win = jnp.dot(jnp.concatenate(p_win, axis=0), vw_ref[0, 0, pl.ds(w0, span), :],
                      preferred_element_type=jnp.float32)

    p_sum = p_heads[0]
    for h in range(1, hg):
        p_sum = p_sum + p_heads[h]
    jn = lax.broadcasted_iota(jnp.int32, (n_slc, n_cmp_rows), 0) * SLC_BLOCK
    nn = lax.broadcasted_iota(jnp.int32, (n_slc, n_cmp_rows), 1) * CMP_STRIDE
    ov_t = jnp.where((nn < jn + SLC_BLOCK) & (nn + (CMP_BLOCK - 1) >= jn), 1.0, 0.0).astype(BF16)
    p_hi = p_sum.astype(BF16)
    p_lo = (p_sum - p_hi.astype(p_sum.dtype)).astype(BF16)
    imp_t = (lax.dot_general(ov_t, p_hi, (((1,), (1,)), ((), ())), preferred_element_type=jnp.float32)
             + lax.dot_general(ov_t, p_lo, (((1,), (1,)), ((), ())), preferred_element_type=jnp.float32))

    jb = lax.broadcasted_iota(jnp.int32, (n_slc, qt), 0)
    tq = t0 + lax.broadcasted_iota(jnp.int32, (n_slc, qt), 1)
    dist = jnp.right_shift(tq, SLC_SHIFT) - jb
    forced = (jb == 0) | ((dist >= 0) & (dist < N_LOCAL_BLOCKS))
    score = jnp.where(dist >= 0, imp_t + jnp.where(forced, FORCE_BONUS, 0.0), -jnp.inf)
    sub = 8
    groups = [score[v * sub:(v + 1) * sub] for v in range(n_slc // sub)]
    jrow = lax.broadcasted_iota(jnp.int32, (sub, qt), 0)
    cnt = [jnp.zeros((sub, qt), jnp.float32) for _ in groups]
    for j2 in range(n_slc):
        row = score[j2:j2 + 1, :]
        for v, sv in enumerate(groups):
            if v * sub > j2:
                inc = jnp.where(row >= sv, 1.0, 0.0)
            elif (v + 1) * sub - 1 <= j2:
                inc = jnp.where(row > sv, 1.0, 0.0)
            else:
                inc = jnp.where(jrow + v * sub > j2, jnp.where(row >= sv, 1.0, 0.0),
                                jnp.where(row > sv, 1.0, 0.0))
            cnt[v] = cnt[v] + inc
    cnt = jnp.concatenate(cnt, axis=0)
    nsel_t = jnp.where(dist >= 0, jnp.where(cnt < min(SLC_TOPK, n_slc), 0.0, 1.0), 1.0)
    parts = [jnp.zeros((HEAD_DIM, qt), jnp.float32), nsel_t]
    if n_slc < HEAD_DIM:
        parts.append(jnp.ones((HEAD_DIM - n_slc, qt), jnp.float32))
    nsel = jnp.concatenate(parts, axis=0).T
    q_aug = jnp.concatenate([x + nsel for x in q_n], axis=0).astype(BF16)

    def slc_body(i, carry):
        k0 = pl.multiple_of(i * kt, kt)
        s = _bdot_nt(q_aug, ks_ref[0, 0, pl.ds(k0, kt), :])
        return _softmax_step(s, None, vs_ref[0, 0, pl.ds(k0, kt), :], carry, hg, qt)

    init = (jnp.full((rows, 1), M_INIT, jnp.float32), jnp.zeros((rows, LANES), jnp.float32))
    n_full = t0 // kt
    carry = lax.fori_loop(0, n_full, slc_body, init)
    k0 = pl.multiple_of(n_full * kt, kt)
    s = _bdot_nt(q_aug, ks_ref[0, 0, pl.ds(k0, kt), :])
    causal = jnp.where(k0 + lax.broadcasted_iota(jnp.int32, (1, kt), 1) <= t_rows, 0.0, NEG_INF)
    _, acc_slc = _softmax_step(s, causal, vs_ref[0, 0, pl.ds(k0, kt), :], carry, hg, qt)

    gates = 1.0 / (1.0 + jnp.exp(-gate_ref[...]))
    g_hi = gates.astype(BF16)
    g_lo = (gates - g_hi.astype(gates.dtype)).astype(BF16)
    spread = (jnp.dot(g_hi, bcast_ref[0], preferred_element_type=jnp.float32)
              + jnp.dot(g_lo, bcast_ref[0], preferred_element_type=jnp.float32))

    def pair(acc, j, normalise):
        a0 = acc[(2 * j) * qt:(2 * j + 1) * qt]
        a1 = acc[(2 * j + 1) * qt:(2 * j + 2) * qt]
        num = jnp.where(low_lanes, a0, swap_halves(a1))
        if not normalise:
            return num
        return num / jnp.where(low_lanes, swap_halves(a0), a1)

    for j in range(hg // PAIR):
        width = PAIR * HEAD_DIM
        o = jnp.zeros((qt, width), jnp.float32)
        for br, (acc, normalise) in enumerate(((o_cmp, False), (acc_slc, True), (acc_win, True))):
            col = (br * (hg // PAIR) + j) * width
            o = o + spread[:, col:col + width] * pair(acc, j, normalise)
        o_ref[:, j * width:(j + 1) * width] = o


def _nsa_attn(q, gate, q_gain, kc, vc, ks, vs, kw, vw, qt=128, kt=512):
    bt, d = q.shape
    nb, ng, seq, n = kw.shape
    kt = min(kt, seq)
    assert seq >= WINDOW + qt and seq % kt == 0 and kt % qt == 0
    nq = seq // qt
    hg = HEADS_PER_GROUP
    width = hg * n
    rows_c = kc.shape[2]
    col = jnp.arange(N_BRANCHES * width) // n
    src = (col // hg) * N_HEADS + jnp.arange(ng)[:, None] * hg + col % hg
    bcast = (jnp.arange(LANES)[None, :, None] == src[:, None, :]).astype(BF16)
    whole = lambda b, g, i: (b, g, 0, 0)
    q_blk = pl.BlockSpec((qt, width), lambda b, g, i: (b * nq + i, g))
    g_blk = pl.BlockSpec((qt, LANES), lambda b, g, i: (b * nq + i, 0))
    b_blk = pl.BlockSpec((1, LANES, N_BRANCHES * width), lambda b, g, i: (g, 0, 0))
    kc_blk = pl.BlockSpec((1, 1, rows_c, n), whole)
    vc_blk = pl.BlockSpec((1, 1, rows_c, LANES), whole)
    k_blk = pl.BlockSpec((1, 1, seq, n), whole)
    v_blk = pl.BlockSpec((1, 1, seq, LANES), whole)
    return pl.pallas_call(
        functools.partial(_nsa_attn_kernel, qt=qt, kt=kt, seq=seq),
        grid=(nb, ng, nq),
        in_specs=[q_blk, g_blk, b_blk, _const_spec((1, LANES)), kc_blk, vc_blk, v_blk, v_blk,
                  k_blk, v_blk],
        out_specs=q_blk,
        out_shape=jax.ShapeDtypeStruct((bt, d), jnp.float32),
        compiler_params=pltpu.CompilerParams(
            dimension_semantics=("parallel", "parallel", "arbitrary"),
            vmem_limit_bytes=VMEM_LIMIT),
    )(q, gate, bcast, jnp.pad(q_gain, (0, LANES - n)).reshape(1, LANES), kc, vc, ks, vs, kw, vw)


def _nsa_layer(x2d, nb, seq, g, w_in, q_gain, k_gain, cmp_pe, cmp_w1, cmp_b1, cmp_w2, w_out):
    q, kc_raw, vc_raw, ks, vs, kw, vw, z, gate = _nsa_in(x2d, nb, seq, g, w_in, k_gain)
    kc, vc = _compress(kc_raw, vc_raw, cmp_pe, cmp_w1, cmp_b1, cmp_w2, k_gain)
    o = _nsa_attn(q, gate, q_gain, kc, vc, ks, vs, kw, vw)
    return _out_proj(x2d, o, z, w_out)


def kernel(x, norm_g, rwkv_mu, rwkv_w_in, rwkv_w0, rwkv_w1, rwkv_w2, rwkv_a0, rwkv_a1, rwkv_a2, rwkv_k_k, rwkv_k_a, rwkv_r_k, rwkv_lnx_g, rwkv_lnx_b, rwkv_w_out, nsa_w_in, nsa_q_gain, nsa_k_gain, nsa_cmp_pe, nsa_cmp_w1, nsa_cmp_b1, nsa_cmp_w2, nsa_w_out):
    b, t, d = x.shape
    x2d = x.reshape(b * t, d)
    x2d = _rwkv_layer(x2d, t, norm_g[0], rwkv_mu[0], rwkv_w_in[0], rwkv_w0[0], rwkv_w1[0],
                      rwkv_w2[0], rwkv_a0[0], rwkv_a1[0], rwkv_a2[0], rwkv_k_k[0], rwkv_k_a[0],
                      rwkv_r_k[0].reshape(-1), rwkv_lnx_g[0], rwkv_lnx_b[0], rwkv_w_out[0])
    x2d = _nsa_layer(x2d, b, t, norm_g[1], nsa_w_in[0], nsa_q_gain[0], nsa_k_gain[0],
                     nsa_cmp_pe[0], nsa_cmp_w1[0], nsa_cmp_b1[0], nsa_cmp_w2[0], nsa_w_out[0])
    return x2d.reshape(b, t, d)
```

```python
import functools

import jax
import jax.numpy as jnp
from jax import lax
from jax.experimental import pallas as pl
from jax.experimental.pallas import tpu as pltpu

D_MODEL = 1024
HEAD_DIM = 64
N_HEADS = D_MODEL // HEAD_DIM
NORM_EPS = 1e-6
LN_X_EPS = 64e-5
DECAY_LORA = 64
ICLR_LORA = 64
N_KV_GROUPS = 4
HEADS_PER_GROUP = N_HEADS // N_KV_GROUPS
KV_WIDTH = N_KV_GROUPS * HEAD_DIM
N_BRANCHES = 3
CMP_BLOCK = 32
CMP_STRIDE = 16
CMP_HIDDEN = 256
SLC_BLOCK = 64
SLC_SHIFT = 6
SLC_TOPK = 16
N_LOCAL_BLOCKS = 2
WINDOW = 512
FORCE_BONUS = 1e4
NEG_INF = -1e30

LANES = 128
PAIR = LANES // HEAD_DIM
N_PAIRS = N_HEADS // PAIR
CHUNK = 64
CHUNK_SHIFT = 6
VMEM_LIMIT = 48 * 1024 * 1024

BF16 = jnp.bfloat16


def _rms(x, eps=NORM_EPS):
    return x * lax.rsqrt(jnp.mean(x * x, axis=-1, keepdims=True) + eps)


def _bdot(a, b):
    return jnp.dot(a.astype(BF16), b.astype(BF16), preferred_element_type=jnp.float32)


def _bdot_nt(a, b):
    return lax.dot_general(a.astype(BF16), b.astype(BF16), (((1,), (1,)), ((), ())),
                           preferred_element_type=jnp.float32)


def _const_spec(shape):
    nd = len(shape)
    return pl.BlockSpec(shape, lambda *_: (0,) * nd)


def _rwkv_in_kernel(x_ref, xp_ref, g_ref, mu_ref, win_ref, w0_ref, w1_ref, w2_ref, a0_ref,
                    a1_ref, a2_ref, kk_ref, ka_ref,
                    r_out, lw_out, cum_out, k_out, v_out, kk_out, a_out, z_out, *, tiles_per_seq):
    d = D_MODEL
    g = g_ref[...]
    h = _rms(x_ref[...]) * g
    hp = _rms(xp_ref[7:8, :]) * g
    hp = jnp.where(pl.program_id(0) % tiles_per_seq == 0, 0.0, hp)
    row = lax.broadcasted_iota(jnp.int32, h.shape, 0)
    hs = jnp.where(row == 0, hp, pltpu.roll(h, 1, axis=0))
    dh = hs - h

    def mix(c):
        return h + dh * mu_ref[c:c + 1, :]

    r_out[...] = _bdot(mix(0), win_ref[:, 0 * d:1 * d])
    k = _bdot(mix(1), win_ref[:, 1 * d:2 * d])
    v_out[...] = _bdot(mix(2), win_ref[:, 2 * d:3 * d])
    z_out[...] = _bdot(mix(3), win_ref[:, 3 * d:4 * d])

    u = w0_ref[...] + _bdot(jnp.tanh(_bdot(mix(4), w1_ref[...])), w2_ref[...])
    sp = jnp.maximum(-u, 0.0) + jnp.log(1.0 + jnp.exp(-jnp.abs(u)))
    lw = -jnp.exp(-sp - 0.5)
    lw_out[...] = lw
    tm = lw.shape[0]
    ri = lax.broadcasted_iota(jnp.int32, (tm, tm), 0)
    ci = lax.broadcasted_iota(jnp.int32, (tm, tm), 1)
    same_chunk = jnp.right_shift(ri, CHUNK_SHIFT) == jnp.right_shift(ci, CHUNK_SHIFT)
    tril = jnp.where(ci <= ri, jnp.where(same_chunk, 1.0, 0.0), 0.0).astype(BF16)
    hi, mid, lo = _split3(lw)
    cum_out[...] = (jnp.dot(tril, hi, preferred_element_type=jnp.float32)
                    + jnp.dot(tril, mid, preferred_element_type=jnp.float32)
                    + jnp.dot(tril, lo, preferred_element_type=jnp.float32))
    ua = a0_ref[...] + _bdot(_bdot(mix(5), a1_ref[...]), a2_ref[...])
    a = 1.0 / (1.0 + jnp.exp(-ua))
    a_out[...] = a
    kk_out[...] = k * kk_ref[...]
    k_out[...] = k * (1.0 + (a - 1.0) * ka_ref[...])


def _rwkv_in(x2d, seq, g, mu, w_in, w0, w1, w2, a0, a1, a2, k_k, k_a, tm=256):
    bt, d = x2d.shape
    tiles_per_seq = seq // tm
    row = lambda i: (i, 0)
    prev = lambda i: (jnp.maximum(i * (tm // 8) - 1, 0), 0)
    out_sds = jax.ShapeDtypeStruct((bt, d), jnp.float32)
    return pl.pallas_call(
        functools.partial(_rwkv_in_kernel, tiles_per_seq=tiles_per_seq),
        grid=(bt // tm,),
        in_specs=[pl.BlockSpec((tm, d), row), pl.BlockSpec((8, d), prev),
                  _const_spec((1, d)), _const_spec((6, d)), _const_spec((d, 4 * d)),
                  _const_spec((1, d)), _const_spec((d, DECAY_LORA)), _const_spec((DECAY_LORA, d)),
                  _const_spec((1, d)), _const_spec((d, ICLR_LORA)), _const_spec((ICLR_LORA, d)),
                  _const_spec((1, d)), _const_spec((1, d))],
        out_specs=[pl.BlockSpec((tm, d), row)] * 8,
        out_shape=[out_sds] * 8,
        compiler_params=pltpu.CompilerParams(dimension_semantics=("parallel",),
                                             vmem_limit_bytes=VMEM_LIMIT),
    )(x2d, x2d, g.reshape(1, d), mu, w_in.astype(BF16), w0.reshape(1, d), w1.astype(BF16),
      w2.astype(BF16), a0.reshape(1, d), a1.astype(BF16), a2.astype(BF16), k_k.reshape(1, d),
      k_a.reshape(1, d))


def _split3(x):
    hi = x.astype(BF16)
    r1 = x - hi.astype(x.dtype)
    mid = r1.astype(BF16)
    lo = (r1 - mid.astype(x.dtype)).astype(BF16)
    return hi, mid, lo


def _wkv_kernel(r_ref, lw_ref, cum_ref, k_ref, v_ref, kk_ref, a_ref, rk_ref, lng_ref, lnb_ref,
                o_ref, state_ref):
    @pl.when(pl.program_id(1) == 0)
    def _():
        state_ref[...] = jnp.zeros_like(state_ref)

    c = CHUNK
    pairs = range(N_PAIRS)
    sls = [slice(p * LANES, (p + 1) * LANES) for p in pairs]
    m0 = lax.broadcasted_iota(jnp.int32, (c, LANES), 1) < HEAD_DIM

    def head_sum(x):
        s0 = jnp.sum(jnp.where(m0, x, 0.0), axis=-1, keepdims=True)
        s1 = jnp.sum(jnp.where(m0, 0.0, x), axis=-1, keepdims=True)
        return jnp.where(m0, s0, s1)

    def stack(y):
        return jnp.concatenate([jnp.where(m0, y, 0.0), jnp.where(m0, 0.0, y)], axis=0)

    def each(f, *lists):
        return [f(*xs) for xs in zip(*lists)]

    r = [r_ref[:, s] for s in sls]
    k = [k_ref[:, s] for s in sls]
    v = [v_ref[:, s] for s in sls]
    cum = [cum_ref[:, s] for s in sls]
    state = [state_ref[p] for p in pairs]

    def prep(p):
        kkraw = kk_ref[:, sls[p]]
        kk = kkraw / jnp.maximum(jnp.sqrt(head_sum(kkraw * kkraw)), 1e-12)
        b = kk * a_ref[:, sls[p]]
        g_inv = jnp.exp(-cum[p])
        at = -kk * jnp.exp(cum[p] - lw_ref[:, sls[p]])
        rt = r[p] * jnp.exp(cum[p])
        ar = jnp.concatenate([at, rt], axis=0)
        kb_t = jnp.concatenate([stack(k[p] * g_inv), stack(b * g_inv)], axis=0)
        return b, ar, kb_t

    b, ar, kb_t = zip(*[prep(p) for p in pairs])
    big = each(_bdot_nt, ar, kb_t)
    ar_s = each(_bdot_nt, ar, state)

    row2 = lax.broadcasted_iota(jnp.int32, (c, 2 * c), 0)
    col2 = lax.broadcasted_iota(jnp.int32, (c, 2 * c), 1)
    col2 = jnp.where(col2 >= c, col2 - c, col2)
    strict = col2 < row2
    incl = col2 <= row2
    eye2 = jnp.where(col2 == row2, 1.0, 0.0)

    x = [jnp.where(strict, g[:c, 2 * c:], 0.0) for g in big]
    t = [eye2 + xi for xi in x]
    for _ in range(5):
        x = each(lambda xi: _bdot(xi, stack(xi)), x)
        t = each(lambda ti, xi: ti + _bdot(ti, stack(xi)), t, x)

    vs = each(stack, v)
    rhs = each(lambda g, s, vsi: s[:c] + _bdot(jnp.where(strict, g[:c, :2 * c], 0.0), vsi),
               big, ar_s, vs)
    u = each(lambda ti, ri: _bdot(ti, stack(ri)), t, rhs)
    o = each(lambda g, s, vsi, ui: s[c:] + _bdot(
        jnp.where(jnp.concatenate([incl, incl], axis=1), g[c:], 0.0),
        jnp.concatenate([vsi, stack(ui)], axis=0)), big, ar_s, vs, u)

    rs = lax.broadcasted_iota(jnp.int32, (LANES, LANES), 0) < HEAD_DIM
    cs = lax.broadcasted_iota(jnp.int32, (LANES, LANES), 1) < HEAD_DIM
    diag = rs == cs
    for p in pairs:
        cum_last = cum[p][c - 1:c, :]
        g_rem = jnp.exp(cum_last - cum[p])
        vu = jnp.concatenate([v[p], u[p]], axis=0)
        kb = jnp.concatenate([k[p] * g_rem, b[p] * g_rem], axis=0)
        upd = _bdot(vu.T, kb)
        state_ref[p] = state[p] * jnp.exp(cum_last) + jnp.where(diag, upd, 0.0)

    for p in pairs:
        mean = head_sum(o[p]) * (1.0 / HEAD_DIM)
        dev = o[p] - mean
        var = head_sum(dev * dev) * (1.0 / HEAD_DIM)
        y = dev * lax.rsqrt(var + LN_X_EPS) * lng_ref[:, sls[p]] + lnb_ref[:, sls[p]]
        o_ref[:, sls[p]] = y + head_sum(r[p] * k[p] * rk_ref[:, sls[p]]) * v[p]


def _wkv(r, lw, cum, k, v, kkraw, a, seq, r_k, lnx_g, lnx_b):
    bt, d = r.shape
    nb = bt // seq
    nc = seq // CHUNK
    row = lambda b, c: (b * nc + c, 0)
    blk = pl.BlockSpec((CHUNK, d), row)
    return pl.pallas_call(
        _wkv_kernel,
        grid=(nb, nc),
        in_specs=[blk] * 7 + [_const_spec((1, d))] * 3,
        out_specs=blk,
        out_shape=jax.ShapeDtypeStruct((bt, d), jnp.float32),
        scratch_shapes=[pltpu.VMEM((N_PAIRS, LANES, LANES), jnp.float32)],
        compiler_params=pltpu.CompilerParams(dimension_semantics=("parallel", "arbitrary"),
                                             vmem_limit_bytes=VMEM_LIMIT),
    )(r, lw, cum, k, v, kkraw, a, r_k.reshape(1, d), lnx_g.reshape(1, d), lnx_b.reshape(1, d))


def _out_proj_kernel(x_ref, o_ref, z_ref, w_ref, y_ref):
    z = z_ref[...]
    y = o_ref[...] * (z / (1.0 + jnp.exp(-z)))
    y_ref[...] = x_ref[...] + _bdot(y, w_ref[...])


def _out_proj(x2d, o, z, w_out, tm=512):
    bt, d = x2d.shape
    row = lambda i: (i, 0)
    blk = pl.BlockSpec((tm, d), row)
    return pl.pallas_call(
        _out_proj_kernel,
        grid=(bt // tm,),
        in_specs=[blk, blk, blk, _const_spec((d, d))],
        out_specs=blk,
        out_shape=jax.ShapeDtypeStruct((bt, d), jnp.float32),
        compiler_params=pltpu.CompilerParams(dimension_semantics=("parallel",),
                                             vmem_limit_bytes=VMEM_LIMIT),
    )(x2d, o, z, w_out.astype(BF16))


def _rwkv_layer(x2d, seq, g, mu, w_in, w0, w1, w2, a0, a1, a2, k_k, k_a, r_k, lnx_g, lnx_b, w_out):
    r, lw, cum, k, v, kkraw, a, z = _rwkv_in(x2d, seq, g, mu, w_in, w0, w1, w2, a0, a1, a2, k_k,
                                             k_a)
    o = _wkv(r, lw, cum, k, v, kkraw, a, seq, r_k, lnx_g, lnx_b)
    return _out_proj(x2d, o, z, w_out)


def _nsa_in_kernel(x_ref, g_ref, w_ref, kg_ref, q_out, kc_out, vc_out, ks_out, vs_out, kw_out,
                   vw_out, z_out, gate_out, *, tiles_per_seq):
    d = D_MODEL
    h = (_rms(x_ref[...]) * g_ref[...]).astype(BF16)
    tm = h.shape[0]

    def proj(off, width):
        return jnp.dot(h, w_ref[:, off:off + width], preferred_element_type=jnp.float32)

    lane = lax.broadcasted_iota(jnp.int32, (tm, LANES), 1)
    low_lanes = lane < HEAD_DIM
    tok = (pl.program_id(0) % tiles_per_seq) * tm + lax.broadcasted_iota(jnp.int32, (tm, LANES), 0)
    blk_bias = jnp.where(lane - HEAD_DIM == jnp.right_shift(tok, SLC_SHIFT), 1.0, 0.0)

    def slot(p, g):
        two = p[:, (g // PAIR) * LANES:(g // PAIR + 1) * LANES]
        return pltpu.roll(two, HEAD_DIM, axis=1) if g % PAIR else two

    def narrow(ref, p, gain_row):
        for g in range(N_KV_GROUPS):
            seg = p[:, g * HEAD_DIM:(g + 1) * HEAD_DIM]
            if gain_row is not None:
                seg = _rms(seg) * kg_ref[gain_row:gain_row + 1, :HEAD_DIM]
            ref[0, g] = seg.astype(ref.dtype)

    def values(ref, p):
        for g in range(N_KV_GROUPS):
            ref[0, g] = jnp.where(low_lanes, slot(p, g), 1.0).astype(ref.dtype)

    q_out[...] = proj(0, d)
    narrow(kc_out, proj(d, KV_WIDTH), None)
    narrow(vc_out, proj(d + KV_WIDTH, KV_WIDTH), None)
    p = proj(d + 2 * KV_WIDTH, KV_WIDTH)
    for g in range(N_KV_GROUPS):
        x = jnp.where(low_lanes, slot(p, g), 0.0)
        ms = jnp.sum(x * x, axis=-1, keepdims=True) * (1.0 / HEAD_DIM)
        ks_out[0, g] = (x * lax.rsqrt(ms + NORM_EPS) * kg_ref[1:2, :] + blk_bias).astype(BF16)
    values(vs_out, proj(d + 3 * KV_WIDTH, KV_WIDTH))
    narrow(kw_out, proj(d + 4 * KV_WIDTH, KV_WIDTH), 2)
    values(vw_out, proj(d + 5 * KV_WIDTH, KV_WIDTH))
    z_out[...] = proj(d + 6 * KV_WIDTH, d)
    gate_out[...] = proj(2 * d + 6 * KV_WIDTH, LANES)


def _nsa_in(x2d, nb, seq, g, w_in, k_gain, tm=256):
    bt, d = x2d.shape
    tps = seq // tm
    width = w_in.shape[1]
    padded = d + 6 * KV_WIDTH + d + LANES
    w = jnp.pad(w_in, ((0, 0), (0, padded - width))).astype(BF16)
    row = lambda i: (i, 0)
    grp = lambda i: (i // tps, 0, i % tps, 0)
    kv_blk = pl.BlockSpec((1, N_KV_GROUPS, tm, HEAD_DIM), grp)
    v_blk = pl.BlockSpec((1, N_KV_GROUPS, tm, LANES), grp)
    kv_f32 = jax.ShapeDtypeStruct((nb, N_KV_GROUPS, seq, HEAD_DIM), jnp.float32)
    k_b16 = jax.ShapeDtypeStruct((nb, N_KV_GROUPS, seq, HEAD_DIM), BF16)
    v_b16 = jax.ShapeDtypeStruct((nb, N_KV_GROUPS, seq, LANES), BF16)
    wide = jax.ShapeDtypeStruct((bt, d), jnp.float32)
    assert seq // SLC_BLOCK <= HEAD_DIM
    return pl.pallas_call(
        functools.partial(_nsa_in_kernel, tiles_per_seq=tps),
        grid=(bt // tm,),
        in_specs=[pl.BlockSpec((tm, d), row), _const_spec((1, d)), _const_spec((d, padded)),
                  _const_spec((N_BRANCHES, LANES))],
        out_specs=[pl.BlockSpec((tm, d), row), kv_blk, kv_blk, v_blk, v_blk, kv_blk, v_blk,
                   pl.BlockSpec((tm, d), row), pl.BlockSpec((tm, LANES), row)],
        out_shape=[wide, kv_f32, kv_f32, v_b16, v_b16, k_b16, v_b16, wide,
                   jax.ShapeDtypeStruct((bt, LANES), jnp.float32)],
        compiler_params=pltpu.CompilerParams(dimension_semantics=("parallel",),
                                             vmem_limit_bytes=VMEM_LIMIT),
    )(x2d, g.reshape(1, d), w, jnp.pad(k_gain, ((0, 0), (0, LANES - HEAD_DIM))))


def _gelu_tanh(x):
    return 0.5 * x * (1.0 + jnp.tanh(0.7978845608028654 * (x + 0.044715 * (x * x * x))))


def _compress_kernel(uk_ref, uv_ref, pe_ref, w1_ref, b1_ref, w2_ref, kg_ref, kc_out, vc_out):
    half = CMP_STRIDE * HEAD_DIM
    for kv, (u_ref, out) in enumerate(((uk_ref, kc_out), (uv_ref, vc_out))):
        uh = u_ref[0, 0]
        rows = uh.shape[0]
        ha = _bdot(uh + pe_ref[kv, 0:1, :], w1_ref[kv, :half, :])
        hb = _bdot(uh + pe_ref[kv, 1:2, :], w1_ref[kv, half:, :])
        hid = _gelu_tanh(ha + pltpu.roll(hb, rows - 1, axis=0) + b1_ref[kv])
        y = _bdot(hid, w2_ref[kv])
        if kv == 0:
            y = y[:, :HEAD_DIM]
            out[0, 0] = (_rms(y) * kg_ref[0:1, :]).astype(out.dtype)
        else:
            low_lanes = lax.broadcasted_iota(jnp.int32, y.shape, 1) < HEAD_DIM
            out[0, 0] = jnp.where(low_lanes, y, 1.0).astype(out.dtype)


def _compress(kc_raw, vc_raw, pe, w1, b1, w2, k_gain):
    nb, ng, seq, n = kc_raw.shape
    rows = seq // CMP_STRIDE
    half = CMP_STRIDE * n
    uk = kc_raw.reshape(nb, ng, rows, half)
    uv = vc_raw.reshape(nb, ng, rows, half)
    whole = lambda b, g: (b, g, 0, 0)
    u_blk = pl.BlockSpec((1, 1, rows, half), whole)
    w2p = jnp.pad(w2, ((0, 0), (0, 0), (0, LANES - n))).astype(BF16)
    return pl.pallas_call(
        _compress_kernel,
        grid=(nb, ng),
        in_specs=[u_blk, u_blk, _const_spec((2, 2, half)), _const_spec((2, 2 * half, CMP_HIDDEN)),
                  _const_spec((2, 1, CMP_HIDDEN)), _const_spec((2, CMP_HIDDEN, LANES)),
                  _const_spec((N_BRANCHES, n))],
        out_specs=[pl.BlockSpec((1, 1, rows, n), whole), pl.BlockSpec((1, 1, rows, LANES), whole)],
        out_shape=[jax.ShapeDtypeStruct((nb, ng, rows, n), BF16),
                   jax.ShapeDtypeStruct((nb, ng, rows, LANES), BF16)],
        compiler_params=pltpu.CompilerParams(dimension_semantics=("parallel", "parallel"),
                                             vmem_limit_bytes=VMEM_LIMIT),
    )(uk, uv, pe.reshape(2, 2, half), w1.astype(BF16), b1.reshape(2, 1, CMP_HIDDEN), w2p, k_gain)


M_INIT = -1e20


LOG2_E = 1.4426950408889634


MAX_STATIC_BOUND = 50.0


def _softmax_step(s, bias, v_tile, carry, hg, qt, online):
    sb = [s[h * qt:(h + 1) * qt] for h in range(hg)]
    if bias is not None:
        sb = [x + bias for x in sb]
    if not online:
        p = jnp.concatenate([jnp.exp2(x).astype(BF16) for x in sb], axis=0)
        return carry + jnp.dot(p, v_tile, preferred_element_type=jnp.float32)
    m, acc = carry
    m_new = jnp.maximum(m, jnp.concatenate(
        [jnp.max(x, axis=-1, keepdims=True) for x in sb], axis=0))
    p = jnp.concatenate(
        [jnp.exp2(sb[h] - m_new[h * qt:(h + 1) * qt]).astype(BF16) for h in range(hg)], axis=0)
    acc = jnp.exp2(m - m_new) * acc + jnp.dot(p, v_tile, preferred_element_type=jnp.float32)
    return m_new, acc


def _nsa_attn_kernel(bound_ref, q_ref, gate_ref, bcast_ref, qg_ref, kc_ref, vc_ref, ks_ref, vs_ref,
                     kw_ref, vw_ref, o_ref, *, qt, kt, seq, online):
    hg = HEADS_PER_GROUP
    t0 = pl.program_id(2) * qt
    n_slc = seq // SLC_BLOCK
    n_cmp_rows = kc_ref.shape[2]
    rows = hg * qt
    low_lanes = lax.broadcasted_iota(jnp.int32, (qt, LANES), 1) < HEAD_DIM
    keep_c, keep_s, keep_w = (0.0, 0.0, 0.0) if online else (
        -bound_ref[0], -bound_ref[1], -bound_ref[2])

    def swap_halves(x):
        return pltpu.roll(x, HEAD_DIM, axis=1)

    q = q_ref[...]
    q_n = []
    for h in range(hg):
        two = q[:, (h // PAIR) * LANES:(h // PAIR + 1) * LANES]
        x = jnp.where(low_lanes, swap_halves(two) if h % PAIR else two, 0.0)
        ms = jnp.sum(x * x, axis=-1, keepdims=True) * (1.0 / HEAD_DIM)
        q_n.append(x * lax.rsqrt(ms + NORM_EPS) * (qg_ref[...] * (LOG2_E * HEAD_DIM ** -0.5)))
    qs = jnp.concatenate([x[:, :HEAD_DIM] for x in q_n], axis=0).astype(BF16)
    t_rows = t0 + lax.broadcasted_iota(jnp.int32, (qt, 1), 0)

    s = _bdot_nt(qs, kc_ref[0, 0])
    cmp_end = lax.broadcasted_iota(jnp.int32, (1, n_cmp_rows), 1) * CMP_STRIDE + (CMP_BLOCK - 1)
    cbias = jnp.where(cmp_end <= t_rows, keep_c, NEG_INF)
    p_heads = []
    for h in range(hg):
        sb = s[h * qt:(h + 1) * qt] + cbias
        if online:
            sb = sb - jnp.maximum(jnp.max(sb, axis=-1, keepdims=True), M_INIT)
        e = jnp.exp2(sb)
        p_heads.append(e / jnp.maximum(jnp.sum(e, axis=-1, keepdims=True), 1e-37))
    o_cmp = jnp.dot(jnp.concatenate([p.astype(BF16) for p in p_heads], axis=0), vc_ref[0, 0],
                    preferred_element_type=jnp.float32)

    span = WINDOW + qt
    w0 = pl.multiple_of(jnp.maximum(t0 - WINDOW, 0), qt)
    s = _bdot_nt(qs, kw_ref[0, 0, pl.ds(w0, span), :])
    lag = t_rows - (w0 + lax.broadcasted_iota(jnp.int32, (1, span), 1))
    wbias = jnp.where(lag >= 0, jnp.where(lag < WINDOW, keep_w, NEG_INF), NEG_INF)
    p_win = []
    for h in range(hg):
        sb = s[h * qt:(h + 1) * qt] + wbias
        if online:
            sb = sb - jnp.max(sb, axis=-1, keepdims=True)
        p_win.append(jnp.exp2(sb).astype(BF16))
    acc_win = jnp.dot(jnp.concatenate(p_win, axis=0), vw_ref[0, 0, pl.ds(w0, span), :],
                      preferred_element_type=jnp.float32)

    p_sum = p_heads[0]
    for h in range(1, hg):
        p_sum = p_sum + p_heads[h]
    jn = lax.broadcasted_iota(jnp.int32, (n_slc, n_cmp_rows), 0) * SLC_BLOCK
    nn = lax.broadcasted_iota(jnp.int32, (n_slc, n_cmp_rows), 1) * CMP_STRIDE
    ov_t = jnp.where((nn < jn + SLC_BLOCK) & (nn + (CMP_BLOCK - 1) >= jn), 1.0, 0.0).astype(BF16)
    p_hi = p_sum.astype(BF16)
    p_lo = (p_sum - p_hi.astype(p_sum.dtype)).astype(BF16)
    imp_t = (lax.dot_general(ov_t, p_hi, (((1,), (1,)), ((), ())), preferred_element_type=jnp.float32)
             + lax.dot_general(ov_t, p_lo, (((1,), (1,)), ((), ())), preferred_element_type=jnp.float32))

    jb = lax.broadcasted_iota(jnp.int32, (n_slc, qt), 0)
    tq = t0 + lax.broadcasted_iota(jnp.int32, (n_slc, qt), 1)
    dist = jnp.right_shift(tq, SLC_SHIFT) - jb
    forced = (jb == 0) | ((dist >= 0) & (dist < N_LOCAL_BLOCKS))
    score = jnp.where(dist >= 0, imp_t + jnp.where(forced, FORCE_BONUS, 0.0), -jnp.inf)
    sub = 8
    groups = [score[v * sub:(v + 1) * sub] for v in range(n_slc // sub)]
    jrow = lax.broadcasted_iota(jnp.int32, (sub, qt), 0)
    cnt = [jnp.zeros((sub, qt), jnp.float32) for _ in groups]
    for j2 in range(n_slc):
        row = score[j2:j2 + 1, :]
        for v, sv in enumerate(groups):
            if v * sub > j2:
                inc = jnp.where(row >= sv, 1.0, 0.0)
            elif (v + 1) * sub - 1 <= j2:
                inc = jnp.where(row > sv, 1.0, 0.0)
            else:
                inc = jnp.where(jrow + v * sub > j2, jnp.where(row >= sv, 1.0, 0.0),
                                jnp.where(row > sv, 1.0, 0.0))
            cnt[v] = cnt[v] + inc
    cnt = jnp.concatenate(cnt, axis=0)
    blk_bias_t = jnp.where(dist >= 0, jnp.where(cnt < min(SLC_TOPK, n_slc), keep_s, NEG_INF),
                           NEG_INF)
    parts = [jnp.zeros((HEAD_DIM, qt), jnp.float32), blk_bias_t]
    if n_slc < HEAD_DIM:
        parts.append(jnp.zeros((HEAD_DIM - n_slc, qt), jnp.float32))
    blk_bias = jnp.concatenate(parts, axis=0).T
    q_aug = jnp.concatenate([x + blk_bias for x in q_n], axis=0).astype(BF16)

    def slc_body(i, carry):
        k0 = pl.multiple_of(i * kt, kt)
        s = _bdot_nt(q_aug, ks_ref[0, 0, pl.ds(k0, kt), :])
        return _softmax_step(s, None, vs_ref[0, 0, pl.ds(k0, kt), :], carry, hg, qt, online)

    init = jnp.zeros((rows, LANES), jnp.float32)
    if online:
        init = (jnp.full((rows, 1), M_INIT, jnp.float32), init)
    n_full = t0 // kt
    carry = lax.fori_loop(0, n_full, slc_body, init)
    k0 = pl.multiple_of(n_full * kt, kt)
    s = _bdot_nt(q_aug, ks_ref[0, 0, pl.ds(k0, kt), :])
    causal = jnp.where(k0 + lax.broadcasted_iota(jnp.int32, (1, kt), 1) <= t_rows, 0.0, NEG_INF)
    acc_slc = _softmax_step(s, causal, vs_ref[0, 0, pl.ds(k0, kt), :], carry, hg, qt, online)
    if online:
        acc_slc = acc_slc[1]

    gates = 1.0 / (1.0 + jnp.exp(-gate_ref[...]))
    g_hi = gates.astype(BF16)
    g_lo = (gates - g_hi.astype(gates.dtype)).astype(BF16)
    spread = (jnp.dot(g_hi, bcast_ref[0], preferred_element_type=jnp.float32)
              + jnp.dot(g_lo, bcast_ref[0], preferred_element_type=jnp.float32))

    def pair(acc, j, normalise):
        a0 = acc[(2 * j) * qt:(2 * j + 1) * qt]
        a1 = acc[(2 * j + 1) * qt:(2 * j + 2) * qt]
        num = jnp.where(low_lanes, a0, swap_halves(a1))
        if not normalise:
            return num
        return num / jnp.where(low_lanes, swap_halves(a0), a1)

    for j in range(hg // PAIR):
        width = PAIR * HEAD_DIM
        o = jnp.zeros((qt, width), jnp.float32)
        for br, (acc, normalise) in enumerate(((o_cmp, False), (acc_slc, True), (acc_win, True))):
            col = (br * (hg // PAIR) + j) * width
            o = o + spread[:, col:col + width] * pair(acc, j, normalise)
        o_ref[:, j * width:(j + 1) * width] = o


def _nsa_attn(q, gate, q_gain, k_gain, kc, vc, ks, vs, kw, vw, qt=128, kt=512):
    bt, d = q.shape
    nb, ng, seq, n = kw.shape
    kt = min(kt, seq)
    assert seq >= WINDOW + qt and seq % kt == 0 and kt % qt == 0
    nq = seq // qt
    hg = HEADS_PER_GROUP
    width = hg * n
    rows_c = kc.shape[2]
    col = jnp.arange(N_BRANCHES * width) // n
    src = (col // hg) * N_HEADS + jnp.arange(ng)[:, None] * hg + col % hg
    bcast = (jnp.arange(LANES)[None, :, None] == src[:, None, :]).astype(BF16)
    whole = lambda b, g, i: (b, g, 0, 0)
    q_blk = pl.BlockSpec((qt, width), lambda b, g, i: (b * nq + i, g))
    g_blk = pl.BlockSpec((qt, LANES), lambda b, g, i: (b * nq + i, 0))
    b_blk = pl.BlockSpec((1, LANES, N_BRANCHES * width), lambda b, g, i: (g, 0, 0))
    kc_blk = pl.BlockSpec((1, 1, rows_c, n), whole)
    vc_blk = pl.BlockSpec((1, 1, rows_c, LANES), whole)
    k_blk = pl.BlockSpec((1, 1, seq, n), whole)
    v_blk = pl.BlockSpec((1, 1, seq, LANES), whole)
    bounds = (1.01 * LOG2_E * n ** 0.5) * jnp.max(jnp.abs(q_gain)) * jnp.max(jnp.abs(k_gain), axis=1)

    def call(online, *args):
        return pl.pallas_call(
            functools.partial(_nsa_attn_kernel, qt=qt, kt=kt, seq=seq, online=online),
            grid=(nb, ng, nq),
            in_specs=[pl.BlockSpec(memory_space=pltpu.SMEM), q_blk, g_blk, b_blk,
                      _const_spec((1, LANES)), kc_blk, vc_blk, v_blk, v_blk, k_blk, v_blk],
            out_specs=q_blk,
            out_shape=jax.ShapeDtypeStruct((bt, d), jnp.float32),
            compiler_params=pltpu.CompilerParams(
                dimension_semantics=("parallel", "parallel", "arbitrary"),
                vmem_limit_bytes=VMEM_LIMIT),
        )(*args)

    return lax.cond(jnp.max(bounds) <= MAX_STATIC_BOUND,
                    functools.partial(call, False), functools.partial(call, True),
                    bounds.astype(jnp.float32), q, gate, bcast,
                    jnp.pad(q_gain, (0, LANES - n)).reshape(1, LANES), kc, vc, ks, vs, kw, vw)


def _nsa_layer(x2d, nb, seq, g, w_in, q_gain, k_gain, cmp_pe, cmp_w1, cmp_b1, cmp_w2, w_out):
    q, kc_raw, vc_raw, ks, vs, kw, vw, z, gate = _nsa_in(x2d, nb, seq, g, w_in, k_gain)
    kc, vc = _compress(kc_raw, vc_raw, cmp_pe, cmp_w1, cmp_b1, cmp_w2, k_gain)
    o = _nsa_attn(q, gate, q_gain, k_gain, kc, vc, ks, vs, kw, vw)
    return _out_proj(x2d, o, z, w_out)


def kernel(x, norm_g, rwkv_mu, rwkv_w_in, rwkv_w0, rwkv_w1, rwkv_w2, rwkv_a0, rwkv_a1, rwkv_a2, rwkv_k_k, rwkv_k_a, rwkv_r_k, rwkv_lnx_g, rwkv_lnx_b, rwkv_w_out, nsa_w_in, nsa_q_gain, nsa_k_gain, nsa_cmp_pe, nsa_cmp_w1, nsa_cmp_b1, nsa_cmp_w2, nsa_w_out):
    b, t, d = x.shape
    x2d = x.reshape(b * t, d)
    x2d = _rwkv_layer(x2d, t, norm_g[0], rwkv_mu[0], rwkv_w_in[0], rwkv_w0[0], rwkv_w1[0],
                      rwkv_w2[0], rwkv_a0[0], rwkv_a1[0], rwkv_a2[0], rwkv_k_k[0], rwkv_k_a[0],
                      rwkv_r_k[0].reshape(-1), rwkv_lnx_g[0], rwkv_lnx_b[0], rwkv_w_out[0])
    x2d = _nsa_layer(x2d, b, t, norm_g[1], nsa_w_in[0], nsa_q_gain[0], nsa_k_gain[0],
                     nsa_cmp_pe[0], nsa_cmp_w1[0], nsa_cmp_b1[0], nsa_cmp_w2[0], nsa_w_out[0])
    return x2d.reshape(b, t, d)
```

```python
import functools

import jax
import jax.numpy as jnp
from jax import lax
from jax.experimental import pallas as pl
from jax.experimental.pallas import tpu as pltpu

D_MODEL = 1024
HEAD_DIM = 64
N_HEADS = D_MODEL // HEAD_DIM
NORM_EPS = 1e-6
LN_X_EPS = 64e-5
DECAY_LORA = 64
ICLR_LORA = 64
N_KV_GROUPS = 4
HEADS_PER_GROUP = N_HEADS // N_KV_GROUPS
KV_WIDTH = N_KV_GROUPS * HEAD_DIM
N_BRANCHES = 3
CMP_BLOCK = 32
CMP_STRIDE = 16
CMP_HIDDEN = 256
SLC_BLOCK = 64
SLC_SHIFT = 6
SLC_TOPK = 16
N_LOCAL_BLOCKS = 2
WINDOW = 512
FORCE_BONUS = 1e4
NEG_INF = -1e30

LANES = 128
PAIR = LANES // HEAD_DIM
N_PAIRS = N_HEADS // PAIR
CHUNK = 64
CHUNK_SHIFT = 6
VMEM_LIMIT = 48 * 1024 * 1024

BF16 = jnp.bfloat16


def _rms(x, eps=NORM_EPS):
    return x * lax.rsqrt(jnp.mean(x * x, axis=-1, keepdims=True) + eps)


def _bdot(a, b):
    return jnp.dot(a.astype(BF16), b.astype(BF16), preferred_element_type=jnp.float32)


def _bdot_nt(a, b):
    return lax.dot_general(a.astype(BF16), b.astype(BF16), (((1,), (1,)), ((), ())),
                           preferred_element_type=jnp.float32)


def _const_spec(shape):
    nd = len(shape)
    return pl.BlockSpec(shape, lambda *_: (0,) * nd)


def _rwkv_in_kernel(x_ref, xp_ref, g_ref, mu_ref, win_ref, w0_ref, w1_ref, w2_ref, a0_ref,
                    a1_ref, a2_ref, kk_ref, ka_ref,
                    r_out, lw_out, cum_out, k_out, v_out, kk_out, a_out, z_out, *, tiles_per_seq):
    d = D_MODEL
    g = g_ref[...]
    h = _rms(x_ref[...]) * g
    hp = _rms(xp_ref[7:8, :]) * g
    hp = jnp.where(pl.program_id(0) % tiles_per_seq == 0, 0.0, hp)
    row = lax.broadcasted_iota(jnp.int32, h.shape, 0)
    hs = jnp.where(row == 0, hp, pltpu.roll(h, 1, axis=0))
    dh = hs - h

    def mix(c):
        return h + dh * mu_ref[c:c + 1, :]

    r_out[...] = _bdot(mix(0), win_ref[:, 0 * d:1 * d]).astype(r_out.dtype)
    k = _bdot(mix(1), win_ref[:, 1 * d:2 * d])
    v_out[...] = _bdot(mix(2), win_ref[:, 2 * d:3 * d]).astype(v_out.dtype)
    z_out[...] = _bdot(mix(3), win_ref[:, 3 * d:4 * d]).astype(z_out.dtype)

    u = w0_ref[...] + _bdot(jnp.tanh(_bdot(mix(4), w1_ref[...])), w2_ref[...])
    sp = jnp.maximum(-u, 0.0) + jnp.log(1.0 + jnp.exp(-jnp.abs(u)))
    lw = -jnp.exp(-sp - 0.5)
    lw_out[...] = lw
    tm = lw.shape[0]
    ri = lax.broadcasted_iota(jnp.int32, (tm, tm), 0)
    ci = lax.broadcasted_iota(jnp.int32, (tm, tm), 1)
    same_chunk = jnp.right_shift(ri, CHUNK_SHIFT) == jnp.right_shift(ci, CHUNK_SHIFT)
    tril = jnp.where(ci <= ri, jnp.where(same_chunk, 1.0, 0.0), 0.0).astype(BF16)
    hi, mid, lo = _split3(lw)
    cum_out[...] = (jnp.dot(tril, hi, preferred_element_type=jnp.float32)
                    + jnp.dot(tril, mid, preferred_element_type=jnp.float32)
                    + jnp.dot(tril, lo, preferred_element_type=jnp.float32))
    ua = a0_ref[...] + _bdot(_bdot(mix(5), a1_ref[...]), a2_ref[...])
    a = 1.0 / (1.0 + jnp.exp(-ua))
    a_out[...] = a.astype(a_out.dtype)
    kk_out[...] = (k * kk_ref[...]).astype(kk_out.dtype)
    k_out[...] = (k * (1.0 + (a - 1.0) * ka_ref[...])).astype(k_out.dtype)


def _rwkv_in(x2d, seq, g, mu, w_in, w0, w1, w2, a0, a1, a2, k_k, k_a, tm=256):
    bt, d = x2d.shape
    tiles_per_seq = seq // tm
    row = lambda i: (i, 0)
    prev = lambda i: (jnp.maximum(i * (tm // 8) - 1, 0), 0)
    f32_sds = jax.ShapeDtypeStruct((bt, d), jnp.float32)
    b16_sds = jax.ShapeDtypeStruct((bt, d), BF16)
    return pl.pallas_call(
        functools.partial(_rwkv_in_kernel, tiles_per_seq=tiles_per_seq),
        grid=(bt // tm,),
        in_specs=[pl.BlockSpec((tm, d), row), pl.BlockSpec((8, d), prev),
                  _const_spec((1, d)), _const_spec((6, d)), _const_spec((d, 4 * d)),
                  _const_spec((1, d)), _const_spec((d, DECAY_LORA)), _const_spec((DECAY_LORA, d)),
                  _const_spec((1, d)), _const_spec((d, ICLR_LORA)), _const_spec((ICLR_LORA, d)),
                  _const_spec((1, d)), _const_spec((1, d))],
        out_specs=[pl.BlockSpec((tm, d), row)] * 8,
        out_shape=[b16_sds, f32_sds, f32_sds] + [b16_sds] * 5,
        compiler_params=pltpu.CompilerParams(dimension_semantics=("parallel",),
                                             vmem_limit_bytes=VMEM_LIMIT),
    )(x2d, x2d, g.reshape(1, d), mu, w_in.astype(BF16), w0.reshape(1, d), w1.astype(BF16),
      w2.astype(BF16), a0.reshape(1, d), a1.astype(BF16), a2.astype(BF16), k_k.reshape(1, d),
      k_a.reshape(1, d))


def _split3(x):
    hi = x.astype(BF16)
    r1 = x - hi.astype(x.dtype)
    mid = r1.astype(BF16)
    lo = (r1 - mid.astype(x.dtype)).astype(BF16)
    return hi, mid, lo


def _wkv_kernel(r_ref, lw_ref, cum_ref, k_ref, v_ref, kk_ref, a_ref, rk_ref, lng_ref, lnb_ref,
                o_ref, state_ref):
    @pl.when(pl.program_id(1) == 0)
    def _():
        state_ref[...] = jnp.zeros_like(state_ref)

    c = CHUNK
    pairs = range(N_PAIRS)
    sls = [slice(p * LANES, (p + 1) * LANES) for p in pairs]
    m0 = lax.broadcasted_iota(jnp.int32, (c, LANES), 1) < HEAD_DIM

    def head_sum(x):
        s0 = jnp.sum(jnp.where(m0, x, 0.0), axis=-1, keepdims=True)
        s1 = jnp.sum(jnp.where(m0, 0.0, x), axis=-1, keepdims=True)
        return jnp.where(m0, s0, s1)

    def stack(y):
        return jnp.concatenate([jnp.where(m0, y, 0.0), jnp.where(m0, 0.0, y)], axis=0)

    def each(f, *lists):
        return [f(*xs) for xs in zip(*lists)]

    f32 = cum_ref.dtype
    r = [r_ref[:, s].astype(f32) for s in sls]
    k = [k_ref[:, s].astype(f32) for s in sls]
    v = [v_ref[:, s].astype(f32) for s in sls]
    cum = [cum_ref[:, s] for s in sls]
    state = [state_ref[p] for p in pairs]

    def prep(p):
        kkraw = kk_ref[:, sls[p]].astype(f32)
        kk = kkraw / jnp.maximum(jnp.sqrt(head_sum(kkraw * kkraw)), 1e-12)
        b = kk * a_ref[:, sls[p]].astype(f32)
        g_inv = jnp.exp(-cum[p])
        at = -kk * jnp.exp(cum[p] - lw_ref[:, sls[p]])
        rt = r[p] * jnp.exp(cum[p])
        ar = jnp.concatenate([at, rt], axis=0)
        kb_t = jnp.concatenate([stack(k[p] * g_inv), stack(b * g_inv)], axis=0)
        return b, ar, kb_t

    b, ar, kb_t = zip(*[prep(p) for p in pairs])
    big = each(_bdot_nt, ar, kb_t)
    ar_s = each(_bdot_nt, ar, state)

    row2 = lax.broadcasted_iota(jnp.int32, (c, 2 * c), 0)
    col2 = lax.broadcasted_iota(jnp.int32, (c, 2 * c), 1)
    col2 = jnp.where(col2 >= c, col2 - c, col2)
    strict = col2 < row2
    incl = col2 <= row2
    eye2 = jnp.where(col2 == row2, 1.0, 0.0)

    x = [jnp.where(strict, g[:c, 2 * c:], 0.0) for g in big]
    t = [eye2 + xi for xi in x]
    for _ in range(5):
        x = each(lambda xi: _bdot(xi, stack(xi)), x)
        t = each(lambda ti, xi: ti + _bdot(ti, stack(xi)), t, x)

    vs = each(stack, v)
    rhs = each(lambda g, s, vsi: s[:c] + _bdot(jnp.where(strict, g[:c, :2 * c], 0.0), vsi),
               big, ar_s, vs)
    u = each(lambda ti, ri: _bdot(ti, stack(ri)), t, rhs)
    o = each(lambda g, s, vsi, ui: s[c:] + _bdot(
        jnp.where(jnp.concatenate([incl, incl], axis=1), g[c:], 0.0),
        jnp.concatenate([vsi, stack(ui)], axis=0)), big, ar_s, vs, u)

    rs = lax.broadcasted_iota(jnp.int32, (LANES, LANES), 0) < HEAD_DIM
    cs = lax.broadcasted_iota(jnp.int32, (LANES, LANES), 1) < HEAD_DIM
    diag = rs == cs
    for p in pairs:
        cum_last = cum[p][c - 1:c, :]
        g_rem = jnp.exp(cum_last - cum[p])
        vu = jnp.concatenate([v[p], u[p]], axis=0)
        kb = jnp.concatenate([k[p] * g_rem, b[p] * g_rem], axis=0)
        upd = _bdot(vu.T, kb)
        state_ref[p] = state[p] * jnp.exp(cum_last) + jnp.where(diag, upd, 0.0)

    for p in pairs:
        mean = head_sum(o[p]) * (1.0 / HEAD_DIM)
        dev = o[p] - mean
        var = head_sum(dev * dev) * (1.0 / HEAD_DIM)
        y = dev * lax.rsqrt(var + LN_X_EPS) * lng_ref[:, sls[p]] + lnb_ref[:, sls[p]]
        o_ref[:, sls[p]] = (y + head_sum(r[p] * k[p] * rk_ref[:, sls[p]]) * v[p]).astype(o_ref.dtype)


def _wkv(r, lw, cum, k, v, kkraw, a, seq, r_k, lnx_g, lnx_b):
    bt, d = r.shape
    nb = bt // seq
    nc = seq // CHUNK
    row = lambda b, c: (b * nc + c, 0)
    blk = pl.BlockSpec((CHUNK, d), row)
    return pl.pallas_call(
        _wkv_kernel,
        grid=(nb, nc),
        in_specs=[blk] * 7 + [_const_spec((1, d))] * 3,
        out_specs=blk,
        out_shape=jax.ShapeDtypeStruct((bt, d), BF16),
        scratch_shapes=[pltpu.VMEM((N_PAIRS, LANES, LANES), jnp.float32)],
        compiler_params=pltpu.CompilerParams(dimension_semantics=("parallel", "arbitrary"),
                                             vmem_limit_bytes=VMEM_LIMIT),
    )(r, lw, cum, k, v, kkraw, a, r_k.reshape(1, d), lnx_g.reshape(1, d), lnx_b.reshape(1, d))


def _out_proj_kernel(x_ref, o_ref, z_ref, w_ref, y_ref):
    x = x_ref[...]
    z = z_ref[...].astype(x.dtype)
    y = o_ref[...].astype(x.dtype) * (z / (1.0 + jnp.exp(-z)))
    y_ref[...] = x + _bdot(y, w_ref[...])


def _out_proj(x2d, o, z, w_out, tm=512):
    bt, d = x2d.shape
    row = lambda i: (i, 0)
    blk = pl.BlockSpec((tm, d), row)
    return pl.pallas_call(
        _out_proj_kernel,
        grid=(bt // tm,),
        in_specs=[blk, blk, blk, _const_spec((d, d))],
        out_specs=blk,
        out_shape=jax.ShapeDtypeStruct((bt, d), jnp.float32),
        compiler_params=pltpu.CompilerParams(dimension_semantics=("parallel",),
                                             vmem_limit_bytes=VMEM_LIMIT),
    )(x2d, o, z, w_out.astype(BF16))


def _rwkv_layer(x2d, seq, g, mu, w_in, w0, w1, w2, a0, a1, a2, k_k, k_a, r_k, lnx_g, lnx_b, w_out):
    r, lw, cum, k, v, kkraw, a, z = _rwkv_in(x2d, seq, g, mu, w_in, w0, w1, w2, a0, a1, a2, k_k,
                                             k_a)
    o = _wkv(r, lw, cum, k, v, kkraw, a, seq, r_k, lnx_g, lnx_b)
    return _out_proj(x2d, o, z, w_out)


def _nsa_in_kernel(x_ref, g_ref, w_ref, qg_ref, kg_ref, q_out, kc_out, vc_out, ks_out, vs_out,
                   kw_out, vw_out, z_out, gate_out, *, tiles_per_seq):
    d = D_MODEL
    h = (_rms(x_ref[...]) * g_ref[...]).astype(BF16)
    tm = h.shape[0]

    def proj(off, width):
        return jnp.dot(h, w_ref[:, off:off + width], preferred_element_type=jnp.float32)

    lane = lax.broadcasted_iota(jnp.int32, (tm, LANES), 1)
    low_lanes = lane < HEAD_DIM
    tok = (pl.program_id(0) % tiles_per_seq) * tm + lax.broadcasted_iota(jnp.int32, (tm, LANES), 0)
    blk_bias = jnp.where(lane - HEAD_DIM == jnp.right_shift(tok, SLC_SHIFT), 1.0, 0.0)

    def slot(p, g):
        two = p[:, (g // PAIR) * LANES:(g // PAIR + 1) * LANES]
        return pltpu.roll(two, HEAD_DIM, axis=1) if g % PAIR else two

    def narrow(ref, p, gain_row):
        for g in range(N_KV_GROUPS):
            seg = p[:, g * HEAD_DIM:(g + 1) * HEAD_DIM]
            if gain_row is not None:
                seg = _rms(seg) * kg_ref[gain_row:gain_row + 1, :HEAD_DIM]
            ref[0, g] = seg.astype(ref.dtype)

    def values(ref, p):
        for g in range(N_KV_GROUPS):
            ref[0, g] = jnp.where(low_lanes, slot(p, g), 1.0).astype(ref.dtype)

    p = proj(0, d)
    for j in range(N_PAIRS):
        x = p[:, j * LANES:(j + 1) * LANES]
        sq = x * x
        ms = jnp.where(low_lanes, jnp.sum(jnp.where(low_lanes, sq, 0.0), axis=-1, keepdims=True),
                       jnp.sum(jnp.where(low_lanes, 0.0, sq), axis=-1, keepdims=True))
        q_out[:, j * LANES:(j + 1) * LANES] = (
            x * lax.rsqrt(ms * (1.0 / HEAD_DIM) + NORM_EPS) * qg_ref[...]).astype(q_out.dtype)
    narrow(kc_out, proj(d, KV_WIDTH), None)
    narrow(vc_out, proj(d + KV_WIDTH, KV_WIDTH), None)
    p = proj(d + 2 * KV_WIDTH, KV_WIDTH)
    for g in range(N_KV_GROUPS):
        x = jnp.where(low_lanes, slot(p, g), 0.0)
        ms = jnp.sum(x * x, axis=-1, keepdims=True) * (1.0 / HEAD_DIM)
        ks_out[0, g] = (x * lax.rsqrt(ms + NORM_EPS) * kg_ref[1:2, :] + blk_bias).astype(BF16)
    values(vs_out, proj(d + 3 * KV_WIDTH, KV_WIDTH))
    narrow(kw_out, proj(d + 4 * KV_WIDTH, KV_WIDTH), 2)
    values(vw_out, proj(d + 5 * KV_WIDTH, KV_WIDTH))
    z_out[...] = proj(d + 6 * KV_WIDTH, d).astype(z_out.dtype)
    gate_out[...] = proj(2 * d + 6 * KV_WIDTH, LANES)


def _nsa_in(x2d, nb, seq, g, w_in, q_gain, k_gain, tm=256):
    bt, d = x2d.shape
    tps = seq // tm
    width = w_in.shape[1]
    padded = d + 6 * KV_WIDTH + d + LANES
    w = jnp.pad(w_in, ((0, 0), (0, padded - width))).astype(BF16)
    row = lambda i: (i, 0)
    grp = lambda i: (i // tps, 0, i % tps, 0)
    kv_blk = pl.BlockSpec((1, N_KV_GROUPS, tm, HEAD_DIM), grp)
    v_blk = pl.BlockSpec((1, N_KV_GROUPS, tm, LANES), grp)
    kv_f32 = jax.ShapeDtypeStruct((nb, N_KV_GROUPS, seq, HEAD_DIM), jnp.float32)
    k_b16 = jax.ShapeDtypeStruct((nb, N_KV_GROUPS, seq, HEAD_DIM), BF16)
    v_b16 = jax.ShapeDtypeStruct((nb, N_KV_GROUPS, seq, LANES), BF16)
    wide = jax.ShapeDtypeStruct((bt, d), BF16)
    assert seq // SLC_BLOCK <= HEAD_DIM
    qg = jnp.tile(q_gain * (LOG2_E * HEAD_DIM ** -0.5), PAIR).reshape(1, LANES)
    return pl.pallas_call(
        functools.partial(_nsa_in_kernel, tiles_per_seq=tps),
        grid=(bt // tm,),
        in_specs=[pl.BlockSpec((tm, d), row), _const_spec((1, d)), _const_spec((d, padded)),
                  _const_spec((1, LANES)), _const_spec((N_BRANCHES, LANES))],
        out_specs=[pl.BlockSpec((tm, d), row), kv_blk, kv_blk, v_blk, v_blk, kv_blk, v_blk,
                   pl.BlockSpec((tm, d), row), pl.BlockSpec((tm, LANES), row)],
        out_shape=[wide, kv_f32, kv_f32, v_b16, v_b16, k_b16, v_b16, wide,
                   jax.ShapeDtypeStruct((bt, LANES), jnp.float32)],
        compiler_params=pltpu.CompilerParams(dimension_semantics=("parallel",),
                                             vmem_limit_bytes=VMEM_LIMIT),
    )(x2d, g.reshape(1, d), w, qg, jnp.pad(k_gain, ((0, 0), (0, LANES - HEAD_DIM))))


def _gelu_tanh(x):
    return 0.5 * x * (1.0 + jnp.tanh(0.7978845608028654 * (x + 0.044715 * (x * x * x))))


def _compress_kernel(uk_ref, uv_ref, pe_ref, w1_ref, b1_ref, w2_ref, kg_ref, kc_out, vc_out):
    half = CMP_STRIDE * HEAD_DIM
    for kv, (u_ref, out) in enumerate(((uk_ref, kc_out), (uv_ref, vc_out))):
        uh = u_ref[0, 0]
        rows = uh.shape[0]
        ha = _bdot(uh + pe_ref[kv, 0:1, :], w1_ref[kv, :half, :])
        hb = _bdot(uh + pe_ref[kv, 1:2, :], w1_ref[kv, half:, :])
        hid = _gelu_tanh(ha + pltpu.roll(hb, rows - 1, axis=0) + b1_ref[kv])
        y = _bdot(hid, w2_ref[kv])
        if kv == 0:
            y = y[:, :HEAD_DIM]
            out[0, 0] = (_rms(y) * kg_ref[0:1, :]).astype(out.dtype)
        else:
            low_lanes = lax.broadcasted_iota(jnp.int32, y.shape, 1) < HEAD_DIM
            out[0, 0] = jnp.where(low_lanes, y, 1.0).astype(out.dtype)


def _compress(kc_raw, vc_raw, pe, w1, b1, w2, k_gain):
    nb, ng, seq, n = kc_raw.shape
    rows = seq // CMP_STRIDE
    half = CMP_STRIDE * n
    uk = kc_raw.reshape(nb, ng, rows, half)
    uv = vc_raw.reshape(nb, ng, rows, half)
    whole = lambda b, g: (b, g, 0, 0)
    u_blk = pl.BlockSpec((1, 1, rows, half), whole)
    w2p = jnp.pad(w2, ((0, 0), (0, 0), (0, LANES - n))).astype(BF16)
    return pl.pallas_call(
        _compress_kernel,
        grid=(nb, ng),
        in_specs=[u_blk, u_blk, _const_spec((2, 2, half)), _const_spec((2, 2 * half, CMP_HIDDEN)),
                  _const_spec((2, 1, CMP_HIDDEN)), _const_spec((2, CMP_HIDDEN, LANES)),
                  _const_spec((N_BRANCHES, n))],
        out_specs=[pl.BlockSpec((1, 1, rows, n), whole), pl.BlockSpec((1, 1, rows, LANES), whole)],
        out_shape=[jax.ShapeDtypeStruct((nb, ng, rows, n), BF16),
                   jax.ShapeDtypeStruct((nb, ng, rows, LANES), BF16)],
        compiler_params=pltpu.CompilerParams(dimension_semantics=("parallel", "parallel"),
                                             vmem_limit_bytes=VMEM_LIMIT),
    )(uk, uv, pe.reshape(2, 2, half), w1.astype(BF16), b1.reshape(2, 1, CMP_HIDDEN), w2p, k_gain)


M_INIT = -1e20


LOG2_E = 1.4426950408889634


MAX_STATIC_BOUND = 50.0


def _softmax_step(s, bias, v_tile, carry, hg, qt, online):
    sb = [s[h * qt:(h + 1) * qt] for h in range(hg)]
    if bias is not None:
        sb = [x + bias for x in sb]
    if not online:
        p = jnp.concatenate([jnp.exp2(x).astype(BF16) for x in sb], axis=0)
        return carry + jnp.dot(p, v_tile, preferred_element_type=jnp.float32)
    m, acc = carry
    m_new = jnp.maximum(m, jnp.concatenate(
        [jnp.max(x, axis=-1, keepdims=True) for x in sb], axis=0))
    p = jnp.concatenate(
        [jnp.exp2(sb[h] - m_new[h * qt:(h + 1) * qt]).astype(BF16) for h in range(hg)], axis=0)
    acc = jnp.exp2(m - m_new) * acc + jnp.dot(p, v_tile, preferred_element_type=jnp.float32)
    return m_new, acc


def _nsa_attn_kernel(bound_ref, q_ref, gate_ref, bcast_ref, kc_ref, vc_ref, ks_ref, vs_ref,
                     kw_ref, vw_ref, o_ref, *, qt, kt, seq, online):
    hg = HEADS_PER_GROUP
    t0 = pl.program_id(2) * qt
    n_slc = seq // SLC_BLOCK
    n_cmp_rows = kc_ref.shape[2]
    rows = hg * qt
    low_lanes = lax.broadcasted_iota(jnp.int32, (qt, LANES), 1) < HEAD_DIM
    keep_c, keep_s, keep_w = (0.0, 0.0, 0.0) if online else (
        -bound_ref[0], -bound_ref[1], -bound_ref[2])

    def swap_halves(x):
        return pltpu.roll(x, HEAD_DIM, axis=1)

    q = q_ref[...].astype(jnp.float32)
    q_n = []
    for h in range(hg):
        two = q[:, (h // PAIR) * LANES:(h // PAIR + 1) * LANES]
        q_n.append(jnp.where(low_lanes, swap_halves(two) if h % PAIR else two, 0.0))
    qs = jnp.concatenate([x[:, :HEAD_DIM] for x in q_n], axis=0).astype(BF16)
    t_rows = t0 + lax.broadcasted_iota(jnp.int32, (qt, 1), 0)

    s = _bdot_nt(qs, kc_ref[0, 0])
    cmp_end = lax.broadcasted_iota(jnp.int32, (1, n_cmp_rows), 1) * CMP_STRIDE + (CMP_BLOCK - 1)
    cbias = jnp.where(cmp_end <= t_rows, keep_c, NEG_INF)
    p_heads = []
    for h in range(hg):
        sb = s[h * qt:(h + 1) * qt] + cbias
        if online:
            sb = sb - jnp.maximum(jnp.max(sb, axis=-1, keepdims=True), M_INIT)
        e = jnp.exp2(sb)
        p_heads.append(e / jnp.maximum(jnp.sum(e, axis=-1, keepdims=True), 1e-37))
    o_cmp = jnp.dot(jnp.concatenate([p.astype(BF16) for p in p_heads], axis=0), vc_ref[0, 0],
                    preferred_element_type=jnp.float32)

    span = WINDOW + qt
    w0 = pl.multiple_of(jnp.maximum(t0 - WINDOW, 0), qt)
    s = _bdot_nt(qs, kw_ref[0, 0, pl.ds(w0, span), :])
    lag = t_rows - (w0 + lax.broadcasted_iota(jnp.int32, (1, span), 1))
    wbias = jnp.where(lag >= 0, jnp.where(lag < WINDOW, keep_w, NEG_INF), NEG_INF)
    p_win = []
    for h in range(hg):
        sb = s[h * qt:(h + 1) * qt] + wbias
        if online:
            sb = sb - jnp.max(sb, axis=-1, keepdims=True)
        p_win.append(jnp.exp2(sb).astype(BF16))
    acc_win = jnp.dot(jnp.concatenate(p_win, axis=0), vw_ref[0, 0, pl.ds(w0, span), :],
                      preferred_element_type=jnp.float32)

    p_sum = p_heads[0]
    for h in range(1, hg):
        p_sum = p_sum + p_heads[h]
    jn = lax.broadcasted_iota(jnp.int32, (n_slc, n_cmp_rows), 0) * SLC_BLOCK
    nn = lax.broadcasted_iota(jnp.int32, (n_slc, n_cmp_rows), 1) * CMP_STRIDE
    ov_t = jnp.where((nn < jn + SLC_BLOCK) & (nn + (CMP_BLOCK - 1) >= jn), 1.0, 0.0).astype(BF16)
    p_hi = p_sum.astype(BF16)
    p_lo = (p_sum - p_hi.astype(p_sum.dtype)).astype(BF16)
    imp_t = (lax.dot_general(ov_t, p_hi, (((1,), (1,)), ((), ())), preferred_element_type=jnp.float32)
             + lax.dot_general(ov_t, p_lo, (((1,), (1,)), ((), ())), preferred_element_type=jnp.float32))

    jb = lax.broadcasted_iota(jnp.int32, (n_slc, qt), 0)
    tq = t0 + lax.broadcasted_iota(jnp.int32, (n_slc, qt), 1)
    dist = jnp.right_shift(tq, SLC_SHIFT) - jb
    forced = (jb == 0) | ((dist >= 0) & (dist < N_LOCAL_BLOCKS))
    score = jnp.where(dist >= 0, imp_t + jnp.where(forced, FORCE_BONUS, 0.0), -jnp.inf)
    sub = 8
    groups = [score[v * sub:(v + 1) * sub] for v in range(n_slc // sub)]
    jrow = lax.broadcasted_iota(jnp.int32, (sub, qt), 0)
    cnt = [jnp.zeros((sub, qt), jnp.float32) for _ in groups]
    for j2 in range(n_slc):
        row = score[j2:j2 + 1, :]
        for v, sv in enumerate(groups):
            if v * sub > j2:
                inc = jnp.where(row >= sv, 1.0, 0.0)
            elif (v + 1) * sub - 1 <= j2:
                inc = jnp.where(row > sv, 1.0, 0.0)
            else:
                inc = jnp.where(jrow + v * sub > j2, jnp.where(row >= sv, 1.0, 0.0),
                                jnp.where(row > sv, 1.0, 0.0))
            cnt[v] = cnt[v] + inc
    cnt = jnp.concatenate(cnt, axis=0)
    blk_bias_t = jnp.where(dist >= 0, jnp.where(cnt < min(SLC_TOPK, n_slc), keep_s, NEG_INF),
                           NEG_INF)
    parts = [jnp.zeros((HEAD_DIM, qt), jnp.float32), blk_bias_t]
    if n_slc < HEAD_DIM:
        parts.append(jnp.zeros((HEAD_DIM - n_slc, qt), jnp.float32))
    blk_bias = jnp.concatenate(parts, axis=0).T
    q_aug = jnp.concatenate([x + blk_bias for x in q_n], axis=0).astype(BF16)

    def slc_body(i, carry):
        k0 = pl.multiple_of(i * kt, kt)
        s = _bdot_nt(q_aug, ks_ref[0, 0, pl.ds(k0, kt), :])
        return _softmax_step(s, None, vs_ref[0, 0, pl.ds(k0, kt), :], carry, hg, qt, online)

    init = jnp.zeros((rows, LANES), jnp.float32)
    if online:
        init = (jnp.full((rows, 1), M_INIT, jnp.float32), init)
    n_full = t0 // kt
    carry = lax.fori_loop(0, n_full, slc_body, init)
    k0 = pl.multiple_of(n_full * kt, kt)
    s = _bdot_nt(q_aug, ks_ref[0, 0, pl.ds(k0, kt), :])
    causal = jnp.where(k0 + lax.broadcasted_iota(jnp.int32, (1, kt), 1) <= t_rows, 0.0, NEG_INF)
    acc_slc = _softmax_step(s, causal, vs_ref[0, 0, pl.ds(k0, kt), :], carry, hg, qt, online)
    if online:
        acc_slc = acc_slc[1]

    gates = 1.0 / (1.0 + jnp.exp(-gate_ref[...]))
    g_hi = gates.astype(BF16)
    g_lo = (gates - g_hi.astype(gates.dtype)).astype(BF16)
    spread = (jnp.dot(g_hi, bcast_ref[0], preferred_element_type=jnp.float32)
              + jnp.dot(g_lo, bcast_ref[0], preferred_element_type=jnp.float32))

    def pair(acc, j, normalise):
        a0 = acc[(2 * j) * qt:(2 * j + 1) * qt]
        a1 = acc[(2 * j + 1) * qt:(2 * j + 2) * qt]
        num = jnp.where(low_lanes, a0, swap_halves(a1))
        if not normalise:
            return num
        return num / jnp.where(low_lanes, swap_halves(a0), a1)

    for j in range(hg // PAIR):
        width = PAIR * HEAD_DIM
        o = jnp.zeros((qt, width), jnp.float32)
        for br, (acc, normalise) in enumerate(((o_cmp, False), (acc_slc, True), (acc_win, True))):
            col = (br * (hg // PAIR) + j) * width
            o = o + spread[:, col:col + width] * pair(acc, j, normalise)
        o_ref[:, j * width:(j + 1) * width] = o.astype(o_ref.dtype)


def _nsa_attn(q, gate, q_gain, k_gain, kc, vc, ks, vs, kw, vw, qt=256, kt=512):
    bt, d = q.shape
    nb, ng, seq, n = kw.shape
    kt = min(kt, seq)
    assert seq >= WINDOW + qt and seq % kt == 0 and kt % qt == 0
    nq = seq // qt
    hg = HEADS_PER_GROUP
    width = hg * n
    rows_c = kc.shape[2]
    col = jnp.arange(N_BRANCHES * width) // n
    src = (col // hg) * N_HEADS + jnp.arange(ng)[:, None] * hg + col % hg
    bcast = (jnp.arange(LANES)[None, :, None] == src[:, None, :]).astype(BF16)
    whole = lambda b, g, i: (b, g, 0, 0)
    q_blk = pl.BlockSpec((qt, width), lambda b, g, i: (b * nq + i, g))
    g_blk = pl.BlockSpec((qt, LANES), lambda b, g, i: (b * nq + i, 0))
    b_blk = pl.BlockSpec((1, LANES, N_BRANCHES * width), lambda b, g, i: (g, 0, 0))
    kc_blk = pl.BlockSpec((1, 1, rows_c, n), whole)
    vc_blk = pl.BlockSpec((1, 1, rows_c, LANES), whole)
    k_blk = pl.BlockSpec((1, 1, seq, n), whole)
    v_blk = pl.BlockSpec((1, 1, seq, LANES), whole)
    bounds = (1.01 * LOG2_E * n ** 0.5) * jnp.max(jnp.abs(q_gain)) * jnp.max(jnp.abs(k_gain), axis=1)

    def call(online, *args):
        return pl.pallas_call(
            functools.partial(_nsa_attn_kernel, qt=qt, kt=kt, seq=seq, online=online),
            grid=(nb, ng, nq),
            in_specs=[pl.BlockSpec(memory_space=pltpu.SMEM), q_blk, g_blk, b_blk,
                      kc_blk, vc_blk, v_blk, v_blk, k_blk, v_blk],
            out_specs=q_blk,
            out_shape=jax.ShapeDtypeStruct((bt, d), BF16),
            compiler_params=pltpu.CompilerParams(
                dimension_semantics=("parallel", "parallel", "arbitrary"),
                vmem_limit_bytes=VMEM_LIMIT),
        )(*args)

    return lax.cond(jnp.max(bounds) <= MAX_STATIC_BOUND,
                    functools.partial(call, False), functools.partial(call, True),
                    bounds.astype(jnp.float32), q, gate, bcast, kc, vc, ks, vs, kw, vw)


def _nsa_layer(x2d, nb, seq, g, w_in, q_gain, k_gain, cmp_pe, cmp_w1, cmp_b1, cmp_w2, w_out):
    q, kc_raw, vc_raw, ks, vs, kw, vw, z, gate = _nsa_in(x2d, nb, seq, g, w_in, q_gain, k_gain)
    kc, vc = _compress(kc_raw, vc_raw, cmp_pe, cmp_w1, cmp_b1, cmp_w2, k_gain)
    o = _nsa_attn(q, gate, q_gain, k_gain, kc, vc, ks, vs, kw, vw)
    return _out_proj(x2d, o, z, w_out)


def kernel(x, norm_g, rwkv_mu, rwkv_w_in, rwkv_w0, rwkv_w1, rwkv_w2, rwkv_a0, rwkv_a1, rwkv_a2, rwkv_k_k, rwkv_k_a, rwkv_r_k, rwkv_lnx_g, rwkv_lnx_b, rwkv_w_out, nsa_w_in, nsa_q_gain, nsa_k_gain, nsa_cmp_pe, nsa_cmp_w1, nsa_cmp_b1, nsa_cmp_w2, nsa_w_out):
    b, t, d = x.shape
    x2d = x.reshape(b * t, d)
    x2d = _rwkv_layer(x2d, t, norm_g[0], rwkv_mu[0], rwkv_w_in[0], rwkv_w0[0], rwkv_w1[0],
                      rwkv_w2[0], rwkv_a0[0], rwkv_a1[0], rwkv_a2[0], rwkv_k_k[0], rwkv_k_a[0],
                      rwkv_r_k[0].reshape(-1), rwkv_lnx_g[0], rwkv_lnx_b[0], rwkv_w_out[0])
    x2d = _nsa_layer(x2d, b, t, norm_g[1], nsa_w_in[0], nsa_q_gain[0], nsa_k_gain[0],
                     nsa_cmp_pe[0], nsa_cmp_w1[0], nsa_cmp_b1[0], nsa_cmp_w2[0], nsa_w_out[0])
    return x2d.reshape(b, t, d)
```

```python
import functools

import jax
import jax.numpy as jnp
from jax import lax
from jax.experimental import pallas as pl
from jax.experimental.pallas import tpu as pltpu

D_MODEL = 1024
HEAD_DIM = 64
N_HEADS = D_MODEL // HEAD_DIM
NORM_EPS = 1e-6
LN_X_EPS = 64e-5
DECAY_LORA = 64
ICLR_LORA = 64
N_KV_GROUPS = 4
HEADS_PER_GROUP = N_HEADS // N_KV_GROUPS
KV_WIDTH = N_KV_GROUPS * HEAD_DIM
N_BRANCHES = 3
CMP_BLOCK = 32
CMP_STRIDE = 16
CMP_HIDDEN = 256
SLC_BLOCK = 64
SLC_SHIFT = 6
SLC_TOPK = 16
N_LOCAL_BLOCKS = 2
WINDOW = 512
FORCE_BONUS = 1e4
NEG_INF = -1e30

LANES = 128
MXU_DEPTH = 256
PAIR = LANES // HEAD_DIM
N_PAIRS = N_HEADS // PAIR
CHUNK = 64
CHUNK_SHIFT = 6
VMEM_LIMIT = 48 * 1024 * 1024

BF16 = jnp.bfloat16


def _rms(x, eps=NORM_EPS):
    return x * lax.rsqrt(jnp.mean(x * x, axis=-1, keepdims=True) + eps)


def _bdot(a, b):
    return jnp.dot(a.astype(BF16), b.astype(BF16), preferred_element_type=jnp.float32)


def _bdot_nt(a, b):
    return lax.dot_general(a.astype(BF16), b.astype(BF16), (((1,), (1,)), ((), ())),
                           preferred_element_type=jnp.float32)


def _const_spec(shape):
    nd = len(shape)
    return pl.BlockSpec(shape, lambda *_: (0,) * nd, pipeline_mode=pl.Buffered(1))


def _rwkv_in_kernel(x_ref, xp_ref, g_ref, mu_ref, win_ref, w0_ref, w1_ref, w2_ref, a0_ref,
                    a1_ref, a2_ref, kk_ref, ka_ref,
                    r_out, lw_out, cum_out, k_out, v_out, kk_out, a_out, z_out, *, tiles_per_seq):
    d = D_MODEL
    g = g_ref[...]
    h = _rms(x_ref[...]) * g
    hp = _rms(xp_ref[7:8, :]) * g
    hp = jnp.where(pl.program_id(0) % tiles_per_seq == 0, 0.0, hp)
    row = lax.broadcasted_iota(jnp.int32, h.shape, 0)
    hs = jnp.where(row == 0, hp, pltpu.roll(h, 1, axis=0))
    dh = hs - h

    def mix(c):
        return h + dh * mu_ref[c:c + 1, :]

    r_out[...] = _bdot(mix(0), win_ref[:, 0 * d:1 * d]).astype(r_out.dtype)
    k = _bdot(mix(1), win_ref[:, 1 * d:2 * d])
    v_out[...] = _bdot(mix(2), win_ref[:, 2 * d:3 * d]).astype(v_out.dtype)
    z_out[...] = _bdot(mix(3), win_ref[:, 3 * d:4 * d]).astype(z_out.dtype)

    u = w0_ref[...] + _bdot(jnp.tanh(_bdot(mix(4), w1_ref[...])), w2_ref[...])
    sp = jnp.maximum(-u, 0.0) + jnp.log(1.0 + jnp.exp(-jnp.abs(u)))
    lw = -jnp.exp(-sp - 0.5)
    lw_out[...] = lw
    rows = min(lw.shape[0], MXU_DEPTH)
    ri = lax.broadcasted_iota(jnp.int32, (rows, rows), 0)
    ci = lax.broadcasted_iota(jnp.int32, (rows, rows), 1)
    same_chunk = jnp.right_shift(ri, CHUNK_SHIFT) == jnp.right_shift(ci, CHUNK_SHIFT)
    tril = jnp.where(ci <= ri, jnp.where(same_chunk, 1.0, 0.0), 0.0).astype(BF16)
    for r0 in range(0, lw.shape[0], rows):
        hi, mid, lo = _split3(lw[r0:r0 + rows])
        cum_out[r0:r0 + rows, :] = (jnp.dot(tril, hi, preferred_element_type=jnp.float32)
                                    + jnp.dot(tril, mid, preferred_element_type=jnp.float32)
                                    + jnp.dot(tril, lo, preferred_element_type=jnp.float32))
    ua = a0_ref[...] + _bdot(_bdot(mix(5), a1_ref[...]), a2_ref[...])
    a = 1.0 / (1.0 + jnp.exp(-ua))
    a_out[...] = a.astype(a_out.dtype)
    kk_out[...] = (k * kk_ref[...]).astype(kk_out.dtype)
    k_out[...] = (k * (1.0 + (a - 1.0) * ka_ref[...])).astype(k_out.dtype)


def _rwkv_in(x2d, seq, g, mu, w_in, w0, w1, w2, a0, a1, a2, k_k, k_a, tm=512):
    bt, d = x2d.shape
    tiles_per_seq = seq // tm
    row = lambda i: (i, 0)
    prev = lambda i: (jnp.maximum(i * (tm // 8) - 1, 0), 0)
    f32_sds = jax.ShapeDtypeStruct((bt, d), jnp.float32)
    b16_sds = jax.ShapeDtypeStruct((bt, d), BF16)
    return pl.pallas_call(
        functools.partial(_rwkv_in_kernel, tiles_per_seq=tiles_per_seq),
        grid=(bt // tm,),
        in_specs=[pl.BlockSpec((tm, d), row), pl.BlockSpec((8, d), prev),
                  _const_spec((1, d)), _const_spec((6, d)), _const_spec((d, 4 * d)),
                  _const_spec((1, d)), _const_spec((d, DECAY_LORA)), _const_spec((DECAY_LORA, d)),
                  _const_spec((1, d)), _const_spec((d, ICLR_LORA)), _const_spec((ICLR_LORA, d)),
                  _const_spec((1, d)), _const_spec((1, d))],
        out_specs=[pl.BlockSpec((tm, d), row)] * 8,
        out_shape=[b16_sds, f32_sds, f32_sds] + [b16_sds] * 5,
        compiler_params=pltpu.CompilerParams(dimension_semantics=("parallel",),
                                             vmem_limit_bytes=VMEM_LIMIT),
    )(x2d, x2d, g.reshape(1, d), mu, w_in.astype(BF16), w0.reshape(1, d), w1.astype(BF16),
      w2.astype(BF16), a0.reshape(1, d), a1.astype(BF16), a2.astype(BF16), k_k.reshape(1, d),
      k_a.reshape(1, d))


def _split3(x):
    hi = x.astype(BF16)
    r1 = x - hi.astype(x.dtype)
    mid = r1.astype(BF16)
    lo = (r1 - mid.astype(x.dtype)).astype(BF16)
    return hi, mid, lo


def _wkv_kernel(r_ref, lw_ref, cum_ref, k_ref, v_ref, kk_ref, a_ref, rk_ref, lng_ref, lnb_ref,
                o_ref, state_ref):
    @pl.when(pl.program_id(1) == 0)
    def _():
        state_ref[...] = jnp.zeros_like(state_ref)

    c = CHUNK
    pairs = range(N_PAIRS)
    sls = [slice(p * LANES, (p + 1) * LANES) for p in pairs]
    m0 = lax.broadcasted_iota(jnp.int32, (c, LANES), 1) < HEAD_DIM

    def head_sum(x):
        s0 = jnp.sum(jnp.where(m0, x, 0.0), axis=-1, keepdims=True)
        s1 = jnp.sum(jnp.where(m0, 0.0, x), axis=-1, keepdims=True)
        return jnp.where(m0, s0, s1)

    def stack(y):
        return jnp.concatenate([jnp.where(m0, y, 0.0), jnp.where(m0, 0.0, y)], axis=0)

    def each(f, *lists):
        return [f(*xs) for xs in zip(*lists)]

    f32 = cum_ref.dtype
    r = [r_ref[:, s].astype(f32) for s in sls]
    k = [k_ref[:, s].astype(f32) for s in sls]
    v = [v_ref[:, s].astype(f32) for s in sls]
    cum = [cum_ref[:, s] for s in sls]
    state = [state_ref[p] for p in pairs]

    def prep(p):
        kkraw = kk_ref[:, sls[p]].astype(f32)
        kk = kkraw / jnp.maximum(jnp.sqrt(head_sum(kkraw * kkraw)), 1e-12)
        b = kk * a_ref[:, sls[p]].astype(f32)
        g_inv = jnp.exp(-cum[p])
        at = -kk * jnp.exp(cum[p] - lw_ref[:, sls[p]])
        rt = r[p] * jnp.exp(cum[p])
        ar = jnp.concatenate([at, rt], axis=0)
        kb_t = jnp.concatenate([stack(k[p] * g_inv), stack(b * g_inv)], axis=0)
        return b, ar, kb_t

    b, ar, kb_t = zip(*[prep(p) for p in pairs])
    big = each(_bdot_nt, ar, kb_t)
    ar_s = each(_bdot_nt, ar, state)

    row2 = lax.broadcasted_iota(jnp.int32, (c, 2 * c), 0)
    col2 = lax.broadcasted_iota(jnp.int32, (c, 2 * c), 1)
    col2 = jnp.where(col2 >= c, col2 - c, col2)
    strict = col2 < row2
    incl = col2 <= row2
    eye2 = jnp.where(col2 == row2, 1.0, 0.0)

    x = [jnp.where(strict, g[:c, 2 * c:], 0.0) for g in big]
    t = [eye2 + xi for xi in x]
    x = each(lambda xi: _bdot(xi, stack(xi)), x)
    for _ in range(4):
        tx = each(lambda ti, xi: _bdot(jnp.concatenate([ti, xi], axis=0), stack(xi)), t, x)
        t = each(lambda ti, r_: ti + r_[:c], t, tx)
        x = [r_[c:] for r_ in tx]
    t = each(lambda ti, xi: ti + _bdot(ti, stack(xi)), t, x)

    av = each(lambda g, vi: _bdot(
        jnp.concatenate([jnp.where(strict, g[:c, :2 * c], 0.0),
                         jnp.where(incl, g[c:, :2 * c], 0.0)], axis=0), stack(vi)), big, v)
    u = each(lambda ti, s, avi: _bdot(ti, stack(s[:c] + avi[:c])), t, ar_s, av)
    o = each(lambda g, s, avi, ui: s[c:] + avi[c:] + _bdot(
        jnp.where(incl, g[c:, 2 * c:], 0.0), stack(ui)), big, ar_s, av, u)

    rs = lax.broadcasted_iota(jnp.int32, (LANES, LANES), 0) < HEAD_DIM
    cs = lax.broadcasted_iota(jnp.int32, (LANES, LANES), 1) < HEAD_DIM
    diag = rs == cs
    for p in pairs:
        cum_last = cum[p][c - 1:c, :]
        g_rem = jnp.exp(cum_last - cum[p])
        vu = jnp.concatenate([v[p], u[p]], axis=0)
        kb = jnp.concatenate([k[p] * g_rem, b[p] * g_rem], axis=0)
        upd = _bdot(vu.T, kb)
        state_ref[p] = state[p] * jnp.exp(cum_last) + jnp.where(diag, upd, 0.0)

    for p in pairs:
        mean = head_sum(o[p]) * (1.0 / HEAD_DIM)
        dev = o[p] - mean
        var = head_sum(dev * dev) * (1.0 / HEAD_DIM)
        y = dev * lax.rsqrt(var + LN_X_EPS) * lng_ref[:, sls[p]] + lnb_ref[:, sls[p]]
        o_ref[:, sls[p]] = (y + head_sum(r[p] * k[p] * rk_ref[:, sls[p]]) * v[p]).astype(o_ref.dtype)


def _wkv(r, lw, cum, k, v, kkraw, a, seq, r_k, lnx_g, lnx_b):
    bt, d = r.shape
    nb = bt // seq
    nc = seq // CHUNK
    row = lambda b, c: (b * nc + c, 0)
    blk = pl.BlockSpec((CHUNK, d), row)
    return pl.pallas_call(
        _wkv_kernel,
        grid=(nb, nc),
        in_specs=[blk] * 7 + [_const_spec((1, d))] * 3,
        out_specs=blk,
        out_shape=jax.ShapeDtypeStruct((bt, d), BF16),
        scratch_shapes=[pltpu.VMEM((N_PAIRS, LANES, LANES), jnp.float32)],
        compiler_params=pltpu.CompilerParams(dimension_semantics=("parallel", "arbitrary"),
                                             vmem_limit_bytes=VMEM_LIMIT),
    )(r, lw, cum, k, v, kkraw, a, r_k.reshape(1, d), lnx_g.reshape(1, d), lnx_b.reshape(1, d))


def _out_proj_kernel(x_ref, o_ref, z_ref, w_ref, y_ref):
    x = x_ref[...]
    z = z_ref[...].astype(x.dtype)
    y = o_ref[...].astype(x.dtype) * (z / (1.0 + jnp.exp(-z)))
    y_ref[...] = x + _bdot(y, w_ref[...])


def _out_proj(x2d, o, z, w_out, tm=512):
    bt, d = x2d.shape
    row = lambda i: (i, 0)
    blk = pl.BlockSpec((tm, d), row)
    return pl.pallas_call(
        _out_proj_kernel,
        grid=(bt // tm,),
        in_specs=[blk, blk, blk, _const_spec((d, d))],
        out_specs=blk,
        out_shape=jax.ShapeDtypeStruct((bt, d), jnp.float32),
        compiler_params=pltpu.CompilerParams(dimension_semantics=("parallel",),
                                             vmem_limit_bytes=VMEM_LIMIT),
    )(x2d, o, z, w_out.astype(BF16))


def _rwkv_layer(x2d, seq, g, mu, w_in, w0, w1, w2, a0, a1, a2, k_k, k_a, r_k, lnx_g, lnx_b, w_out):
    r, lw, cum, k, v, kkraw, a, z = _rwkv_in(x2d, seq, g, mu, w_in, w0, w1, w2, a0, a1, a2, k_k,
                                             k_a)
    o = _wkv(r, lw, cum, k, v, kkraw, a, seq, r_k, lnx_g, lnx_b)
    return _out_proj(x2d, o, z, w_out)


def _nsa_in_kernel(x_ref, g_ref, w_ref, qg_ref, kg_ref, q_out, kc_out, vc_out, ks_out, vs_out,
                   kw_out, vw_out, z_out, gate_out, *, tiles_per_seq):
    d = D_MODEL
    h = (_rms(x_ref[...]) * g_ref[...]).astype(BF16)
    tm = h.shape[0]

    def proj(off, width):
        return jnp.dot(h, w_ref[:, off:off + width], preferred_element_type=jnp.float32)

    lane = lax.broadcasted_iota(jnp.int32, (tm, LANES), 1)
    low_lanes = lane < HEAD_DIM
    tok = (pl.program_id(0) % tiles_per_seq) * tm + lax.broadcasted_iota(jnp.int32, (tm, LANES), 0)
    blk_bias = jnp.where(lane - HEAD_DIM == jnp.right_shift(tok, SLC_SHIFT), 1.0, 0.0)

    def slot(p, g):
        two = p[:, (g // PAIR) * LANES:(g // PAIR + 1) * LANES]
        return pltpu.roll(two, HEAD_DIM, axis=1) if g % PAIR else two

    def narrow(ref, p, gain_row):
        for g in range(N_KV_GROUPS):
            seg = p[:, g * HEAD_DIM:(g + 1) * HEAD_DIM]
            if gain_row is not None:
                seg = _rms(seg) * kg_ref[gain_row:gain_row + 1, :HEAD_DIM]
            ref[0, g] = seg.astype(ref.dtype)

    def values(ref, p):
        for g in range(N_KV_GROUPS):
            ref[0, g] = jnp.where(low_lanes, slot(p, g), 1.0).astype(ref.dtype)

    p = proj(0, d)
    for j in range(N_PAIRS):
        x = p[:, j * LANES:(j + 1) * LANES]
        sq = x * x
        ms = jnp.where(low_lanes, jnp.sum(jnp.where(low_lanes, sq, 0.0), axis=-1, keepdims=True),
                       jnp.sum(jnp.where(low_lanes, 0.0, sq), axis=-1, keepdims=True))
        q_out[:, j * LANES:(j + 1) * LANES] = (
            x * lax.rsqrt(ms * (1.0 / HEAD_DIM) + NORM_EPS) * qg_ref[...]).astype(q_out.dtype)
    narrow(kc_out, proj(d, KV_WIDTH), None)
    narrow(vc_out, proj(d + KV_WIDTH, KV_WIDTH), None)
    p = proj(d + 2 * KV_WIDTH, KV_WIDTH)
    for g in range(N_KV_GROUPS):
        x = jnp.where(low_lanes, slot(p, g), 0.0)
        ms = jnp.sum(x * x, axis=-1, keepdims=True) * (1.0 / HEAD_DIM)
        ks_out[0, g] = (x * lax.rsqrt(ms + NORM_EPS) * kg_ref[1:2, :] + blk_bias).astype(BF16)
    values(vs_out, proj(d + 3 * KV_WIDTH, KV_WIDTH))
    narrow(kw_out, proj(d + 4 * KV_WIDTH, KV_WIDTH), 2)
    values(vw_out, proj(d + 5 * KV_WIDTH, KV_WIDTH))
    z_out[...] = proj(d + 6 * KV_WIDTH, d).astype(z_out.dtype)
    gate_out[...] = proj(2 * d + 6 * KV_WIDTH, LANES)


def _nsa_in(x2d, nb, seq, g, w_in, q_gain, k_gain, tm=256):
    bt, d = x2d.shape
    tps = seq // tm
    width = w_in.shape[1]
    padded = d + 6 * KV_WIDTH + d + LANES
    w = jnp.pad(w_in, ((0, 0), (0, padded - width))).astype(BF16)
    row = lambda i: (i, 0)
    grp = lambda i: (i // tps, 0, i % tps, 0)
    kv_blk = pl.BlockSpec((1, N_KV_GROUPS, tm, HEAD_DIM), grp)
    v_blk = pl.BlockSpec((1, N_KV_GROUPS, tm, LANES), grp)
    kv_f32 = jax.ShapeDtypeStruct((nb, N_KV_GROUPS, seq, HEAD_DIM), jnp.float32)
    k_b16 = jax.ShapeDtypeStruct((nb, N_KV_GROUPS, seq, HEAD_DIM), BF16)
    v_b16 = jax.ShapeDtypeStruct((nb, N_KV_GROUPS, seq, LANES), BF16)
    wide = jax.ShapeDtypeStruct((bt, d), BF16)
    assert seq // SLC_BLOCK <= HEAD_DIM
    qg = jnp.tile(q_gain * (LOG2_E * HEAD_DIM ** -0.5), PAIR).reshape(1, LANES)
    return pl.pallas_call(
        functools.partial(_nsa_in_kernel, tiles_per_seq=tps),
        grid=(bt // tm,),
        in_specs=[pl.BlockSpec((tm, d), row), _const_spec((1, d)), _const_spec((d, padded)),
                  _const_spec((1, LANES)), _const_spec((N_BRANCHES, LANES))],
        out_specs=[pl.BlockSpec((tm, d), row), kv_blk, kv_blk, v_blk, v_blk, kv_blk, v_blk,
                   pl.BlockSpec((tm, d), row), pl.BlockSpec((tm, LANES), row)],
        out_shape=[wide, kv_f32, kv_f32, v_b16, v_b16, k_b16, v_b16, wide,
                   jax.ShapeDtypeStruct((bt, LANES), jnp.float32)],
        compiler_params=pltpu.CompilerParams(dimension_semantics=("parallel",),
                                             vmem_limit_bytes=VMEM_LIMIT),
    )(x2d, g.reshape(1, d), w, qg, jnp.pad(k_gain, ((0, 0), (0, LANES - HEAD_DIM))))


def _gelu_tanh(x):
    return 0.5 * x * (1.0 + jnp.tanh(0.7978845608028654 * (x + 0.044715 * (x * x * x))))


def _compress_kernel(uk_ref, uv_ref, pe_ref, w1_ref, b1_ref, w2_ref, kg_ref, kc_out, vc_out):
    half = CMP_STRIDE * HEAD_DIM
    for kv, (u_ref, out) in enumerate(((uk_ref, kc_out), (uv_ref, vc_out))):
        uh = u_ref[0, 0]
        rows = uh.shape[0]
        ha = _bdot(uh + pe_ref[kv, 0:1, :], w1_ref[kv, :half, :])
        hb = _bdot(uh + pe_ref[kv, 1:2, :], w1_ref[kv, half:, :])
        hid = _gelu_tanh(ha + pltpu.roll(hb, rows - 1, axis=0) + b1_ref[kv])
        y = _bdot(hid, w2_ref[kv])
        if kv == 0:
            y = y[:, :HEAD_DIM]
            out[0, 0] = (_rms(y) * kg_ref[0:1, :]).astype(out.dtype)
        else:
            low_lanes = lax.broadcasted_iota(jnp.int32, y.shape, 1) < HEAD_DIM
            out[0, 0] = jnp.where(low_lanes, y, 1.0).astype(out.dtype)


def _compress(kc_raw, vc_raw, pe, w1, b1, w2, k_gain):
    nb, ng, seq, n = kc_raw.shape
    rows = seq // CMP_STRIDE
    half = CMP_STRIDE * n
    uk = kc_raw.reshape(nb, ng, rows, half)
    uv = vc_raw.reshape(nb, ng, rows, half)
    whole = lambda b, g: (b, g, 0, 0)
    u_blk = pl.BlockSpec((1, 1, rows, half), whole)
    w2p = jnp.pad(w2, ((0, 0), (0, 0), (0, LANES - n))).astype(BF16)
    return pl.pallas_call(
        _compress_kernel,
        grid=(nb, ng),
        in_specs=[u_blk, u_blk, _const_spec((2, 2, half)), _const_spec((2, 2 * half, CMP_HIDDEN)),
                  _const_spec((2, 1, CMP_HIDDEN)), _const_spec((2, CMP_HIDDEN, LANES)),
                  _const_spec((N_BRANCHES, n))],
        out_specs=[pl.BlockSpec((1, 1, rows, n), whole), pl.BlockSpec((1, 1, rows, LANES), whole)],
        out_shape=[jax.ShapeDtypeStruct((nb, ng, rows, n), BF16),
                   jax.ShapeDtypeStruct((nb, ng, rows, LANES), BF16)],
        compiler_params=pltpu.CompilerParams(dimension_semantics=("parallel", "parallel"),
                                             vmem_limit_bytes=VMEM_LIMIT),
    )(uk, uv, pe.reshape(2, 2, half), w1.astype(BF16), b1.reshape(2, 1, CMP_HIDDEN), w2p, k_gain)


M_INIT = -1e20


LOG2_E = 1.4426950408889634


MAX_STATIC_BOUND = 50.0


def _softmax_step(s, bias, v_tile, carry, hg, qt, online):
    sb = [s[h * qt:(h + 1) * qt] for h in range(hg)]
    if bias is not None:
        sb = [x + bias for x in sb]
    if not online:
        p = jnp.concatenate([jnp.exp2(x).astype(BF16) for x in sb], axis=0)
        return carry + jnp.dot(p, v_tile, preferred_element_type=jnp.float32)
    m, acc = carry
    m_new = jnp.maximum(m, jnp.concatenate(
        [jnp.max(x, axis=-1, keepdims=True) for x in sb], axis=0))
    p = jnp.concatenate(
        [jnp.exp2(sb[h] - m_new[h * qt:(h + 1) * qt]).astype(BF16) for h in range(hg)], axis=0)
    acc = jnp.exp2(m - m_new) * acc + jnp.dot(p, v_tile, preferred_element_type=jnp.float32)
    return m_new, acc


def _nsa_attn_kernel(bound_ref, q_ref, gate_ref, bcast_ref, kc_ref, vc_ref, ks_ref, vs_ref,
                     kw_ref, vw_ref, o_ref, *, qt, kt, seq, online):
    hg = HEADS_PER_GROUP
    t0 = pl.program_id(2) * qt
    n_slc = seq // SLC_BLOCK
    n_cmp_rows = kc_ref.shape[2]
    rows = hg * qt
    low_lanes = lax.broadcasted_iota(jnp.int32, (qt, LANES), 1) < HEAD_DIM
    keep_c, keep_s, keep_w = (0.0, 0.0, 0.0) if online else (
        -bound_ref[0], -bound_ref[1], -bound_ref[2])

    def swap_halves(x):
        return pltpu.roll(x, HEAD_DIM, axis=1)

    q = q_ref[...].astype(jnp.float32)
    q_n = []
    for h in range(hg):
        two = q[:, (h // PAIR) * LANES:(h // PAIR + 1) * LANES]
        q_n.append(jnp.where(low_lanes, swap_halves(two) if h % PAIR else two, 0.0))
    qs = jnp.concatenate([x[:, :HEAD_DIM] for x in q_n], axis=0).astype(BF16)
    t_rows = t0 + lax.broadcasted_iota(jnp.int32, (qt, 1), 0)

    span = WINDOW + qt
    w0 = pl.multiple_of(jnp.maximum(t0 - WINDOW, 0), qt)
    s = _bdot_nt(qs, kc_ref[0, 0])
    s_win = _bdot_nt(qs, kw_ref[0, 0, pl.ds(w0, span), :])

    cmp_end = lax.broadcasted_iota(jnp.int32, (1, n_cmp_rows), 1) * CMP_STRIDE + (CMP_BLOCK - 1)
    cbias = jnp.where(cmp_end <= t_rows, keep_c, NEG_INF)
    p_heads = []
    for h in range(hg):
        sb = s[h * qt:(h + 1) * qt] + cbias
        if online:
            sb = sb - jnp.maximum(jnp.max(sb, axis=-1, keepdims=True), M_INIT)
        e = jnp.exp2(sb)
        p_heads.append(e / jnp.maximum(jnp.sum(e, axis=-1, keepdims=True), 1e-37))
    o_cmp = jnp.dot(jnp.concatenate([p.astype(BF16) for p in p_heads], axis=0), vc_ref[0, 0],
                    preferred_element_type=jnp.float32)

    p_sum = p_heads[0]
    for h in range(1, hg):
        p_sum = p_sum + p_heads[h]
    jn = lax.broadcasted_iota(jnp.int32, (n_slc, n_cmp_rows), 0) * SLC_BLOCK
    nn = lax.broadcasted_iota(jnp.int32, (n_slc, n_cmp_rows), 1) * CMP_STRIDE
    ov_t = jnp.where((nn < jn + SLC_BLOCK) & (nn + (CMP_BLOCK - 1) >= jn), 1.0, 0.0).astype(BF16)
    p_hi = p_sum.astype(BF16)
    p_lo = (p_sum - p_hi.astype(p_sum.dtype)).astype(BF16)
    imp_t = (lax.dot_general(ov_t, p_hi, (((1,), (1,)), ((), ())), preferred_element_type=jnp.float32)
             + lax.dot_general(ov_t, p_lo, (((1,), (1,)), ((), ())), preferred_element_type=jnp.float32))

    lag = t_rows - (w0 + lax.broadcasted_iota(jnp.int32, (1, span), 1))
    wbias = jnp.where(lag >= 0, jnp.where(lag < WINDOW, keep_w, NEG_INF), NEG_INF)
    p_win = []
    for h in range(hg):
        sb = s_win[h * qt:(h + 1) * qt] + wbias
        if online:
            sb = sb - jnp.max(sb, axis=-1, keepdims=True)
        p_win.append(jnp.exp2(sb).astype(BF16))
    acc_win = jnp.dot(jnp.concatenate(p_win, axis=0), vw_ref[0, 0, pl.ds(w0, span), :],
                      preferred_element_type=jnp.float32)

    jb = lax.broadcasted_iota(jnp.int32, (n_slc, qt), 0)
    tq = t0 + lax.broadcasted_iota(jnp.int32, (n_slc, qt), 1)
    dist = jnp.right_shift(tq, SLC_SHIFT) - jb
    forced = (jb == 0) | ((dist >= 0) & (dist < N_LOCAL_BLOCKS))
    score = jnp.where(dist >= 0, imp_t + jnp.where(forced, FORCE_BONUS, 0.0), -jnp.inf)
    sub = 8
    groups = [score[v * sub:(v + 1) * sub] for v in range(n_slc // sub)]
    jrow = lax.broadcasted_iota(jnp.int32, (sub, qt), 0)
    cnt = [jnp.zeros((sub, qt), jnp.float32) for _ in groups]
    for j2 in range(n_slc):
        row = score[j2:j2 + 1, :]
        for v, sv in enumerate(groups):
            if v * sub > j2:
                inc = jnp.where(row >= sv, 1.0, 0.0)
            elif (v + 1) * sub - 1 <= j2:
                inc = jnp.where(row > sv, 1.0, 0.0)
            else:
                inc = jnp.where(jrow + v * sub > j2, jnp.where(row >= sv, 1.0, 0.0),
                                jnp.where(row > sv, 1.0, 0.0))
            cnt[v] = cnt[v] + inc
    cnt = jnp.concatenate(cnt, axis=0)
    blk_bias_t = jnp.where(dist >= 0, jnp.where(cnt < min(SLC_TOPK, n_slc), keep_s, NEG_INF),
                           NEG_INF)
    parts = [jnp.zeros((HEAD_DIM, qt), jnp.float32), blk_bias_t]
    if n_slc < HEAD_DIM:
        parts.append(jnp.zeros((HEAD_DIM - n_slc, qt), jnp.float32))
    blk_bias = jnp.concatenate(parts, axis=0).T
    q_aug = jnp.concatenate([x + blk_bias for x in q_n], axis=0).astype(BF16)

    def slc_body(i, carry):
        k0 = pl.multiple_of(i * kt, kt)
        s = _bdot_nt(q_aug, ks_ref[0, 0, pl.ds(k0, kt), :])
        return _softmax_step(s, None, vs_ref[0, 0, pl.ds(k0, kt), :], carry, hg, qt, online)

    init = jnp.zeros((rows, LANES), jnp.float32)
    if online:
        init = (jnp.full((rows, 1), M_INIT, jnp.float32), init)
    n_full = t0 // kt
    carry = lax.fori_loop(0, n_full // 2, lambda j, c: slc_body(2 * j + 1, slc_body(2 * j, c)),
                          init)
    carry = lax.cond(n_full % 2 == 1, lambda c: slc_body(n_full - 1, c), lambda c: c, carry)
    k0 = pl.multiple_of(n_full * kt, kt)
    s = _bdot_nt(q_aug, ks_ref[0, 0, pl.ds(k0, kt), :])
    causal = jnp.where(k0 + lax.broadcasted_iota(jnp.int32, (1, kt), 1) <= t_rows, 0.0, NEG_INF)
    acc_slc = _softmax_step(s, causal, vs_ref[0, 0, pl.ds(k0, kt), :], carry, hg, qt, online)
    if online:
        acc_slc = acc_slc[1]

    gates = 1.0 / (1.0 + jnp.exp(-gate_ref[...]))
    spread = jnp.dot(gates.astype(BF16), bcast_ref[0],
                     preferred_element_type=jnp.float32)

    def pair(acc, j, normalise):
        a0 = acc[(2 * j) * qt:(2 * j + 1) * qt]
        a1 = acc[(2 * j + 1) * qt:(2 * j + 2) * qt]
        num = jnp.where(low_lanes, a0, swap_halves(a1))
        if not normalise:
            return num
        return num / jnp.where(low_lanes, swap_halves(a0), a1)

    for j in range(hg // PAIR):
        width = PAIR * HEAD_DIM
        o = jnp.zeros((qt, width), jnp.float32)
        for br, (acc, normalise) in enumerate(((o_cmp, False), (acc_slc, True), (acc_win, True))):
            col = (br * (hg // PAIR) + j) * width
            o = o + spread[:, col:col + width] * pair(acc, j, normalise)
        o_ref[:, j * width:(j + 1) * width] = o.astype(o_ref.dtype)


def _nsa_attn(q, gate, q_gain, k_gain, kc, vc, ks, vs, kw, vw, qt=256, kt=512):
    bt, d = q.shape
    nb, ng, seq, n = kw.shape
    kt = min(kt, seq)
    assert seq >= WINDOW + qt and seq % kt == 0 and kt % qt == 0
    nq = seq // qt
    hg = HEADS_PER_GROUP
    width = hg * n
    rows_c = kc.shape[2]
    col = jnp.arange(N_BRANCHES * width) // n
    src = (col // hg) * N_HEADS + jnp.arange(ng)[:, None] * hg + col % hg
    bcast = (jnp.arange(LANES)[None, :, None] == src[:, None, :]).astype(BF16)
    whole = lambda b, g, i: (b, g, 0, 0)
    q_blk = pl.BlockSpec((qt, width), lambda b, g, i: (b * nq + i, g))
    g_blk = pl.BlockSpec((qt, LANES), lambda b, g, i: (b * nq + i, 0))
    b_blk = pl.BlockSpec((1, LANES, N_BRANCHES * width), lambda b, g, i: (g, 0, 0))
    kc_blk = pl.BlockSpec((1, 1, rows_c, n), whole)
    vc_blk = pl.BlockSpec((1, 1, rows_c, LANES), whole)
    k_blk = pl.BlockSpec((1, 1, seq, n), whole)
    v_blk = pl.BlockSpec((1, 1, seq, LANES), whole)
    bounds = (1.01 * LOG2_E * n ** 0.5) * jnp.max(jnp.abs(q_gain)) * jnp.max(jnp.abs(k_gain), axis=1)

    def call(online, *args):
        return pl.pallas_call(
            functools.partial(_nsa_attn_kernel, qt=qt, kt=kt, seq=seq, online=online),
            grid=(nb, ng, nq),
            in_specs=[pl.BlockSpec(memory_space=pltpu.SMEM), q_blk, g_blk, b_blk,
                      kc_blk, vc_blk, v_blk, v_blk, k_blk, v_blk],
            out_specs=q_blk,
            out_shape=jax.ShapeDtypeStruct((bt, d), BF16),
            compiler_params=pltpu.CompilerParams(
                dimension_semantics=("parallel", "parallel", "arbitrary"),
                vmem_limit_bytes=VMEM_LIMIT),
        )(*args)

    return lax.cond(jnp.max(bounds) <= MAX_STATIC_BOUND,
                    functools.partial(call, False), functools.partial(call, True),
                    bounds.astype(jnp.float32), q, gate, bcast, kc, vc, ks, vs, kw, vw)


def _nsa_layer(x2d, nb, seq, g, w_in, q_gain, k_gain, cmp_pe, cmp_w1, cmp_b1, cmp_w2, w_out):
    q, kc_raw, vc_raw, ks, vs, kw, vw, z, gate = _nsa_in(x2d, nb, seq, g, w_in, q_gain, k_gain)
    kc, vc = _compress(kc_raw, vc_raw, cmp_pe, cmp_w1, cmp_b1, cmp_w2, k_gain)
    o = _nsa_attn(q, gate, q_gain, k_gain, kc, vc, ks, vs, kw, vw)
    return _out_proj(x2d, o, z, w_out)


def kernel(x, norm_g, rwkv_mu, rwkv_w_in, rwkv_w0, rwkv_w1, rwkv_w2, rwkv_a0, rwkv_a1, rwkv_a2, rwkv_k_k, rwkv_k_a, rwkv_r_k, rwkv_lnx_g, rwkv_lnx_b, rwkv_w_out, nsa_w_in, nsa_q_gain, nsa_k_gain, nsa_cmp_pe, nsa_cmp_w1, nsa_cmp_b1, nsa_cmp_w2, nsa_w_out):
    b, t, d = x.shape
    x2d = x.reshape(b * t, d)
    x2d = _rwkv_layer(x2d, t, norm_g[0], rwkv_mu[0], rwkv_w_in[0], rwkv_w0[0], rwkv_w1[0],
                      rwkv_w2[0], rwkv_a0[0], rwkv_a1[0], rwkv_a2[0], rwkv_k_k[0], rwkv_k_a[0],
                      rwkv_r_k[0].reshape(-1), rwkv_lnx_g[0], rwkv_lnx_b[0], rwkv_w_out[0])
    x2d = _nsa_layer(x2d, b, t, norm_g[1], nsa_w_in[0], nsa_q_gain[0], nsa_k_gain[0],
                     nsa_cmp_pe[0], nsa_cmp_w1[0], nsa_cmp_b1[0], nsa_cmp_w2[0], nsa_w_out[0])
    return x2d.reshape(b, t, d)
```

```python
import functools

import jax
import jax.numpy as jnp
from jax import lax
from jax.experimental import pallas as pl
from jax.experimental.pallas import tpu as pltpu

D_MODEL = 1024
HEAD_DIM = 64
N_HEADS = D_MODEL // HEAD_DIM
NORM_EPS = 1e-6
LN_X_EPS = 64e-5
DECAY_LORA = 64
ICLR_LORA = 64
N_KV_GROUPS = 4
HEADS_PER_GROUP = N_HEADS // N_KV_GROUPS
KV_WIDTH = N_KV_GROUPS * HEAD_DIM
N_BRANCHES = 3
CMP_BLOCK = 32
CMP_STRIDE = 16
CMP_HIDDEN = 256
SLC_BLOCK = 64
SLC_SHIFT = 6
SLC_TOPK = 16
N_LOCAL_BLOCKS = 2
WINDOW = 512
FORCE_BONUS = 1e4
NEG_INF = -1e30

LANES = 128
MXU_DEPTH = 256
PAIR = LANES // HEAD_DIM
N_PAIRS = N_HEADS // PAIR
HEAD_SHIFT = 6
WKV_HEADS = LANES // HEAD_DIM
CHUNK = 64
CHUNK_SHIFT = 6
VMEM_LIMIT = 48 * 1024 * 1024

BF16 = jnp.bfloat16


def _rms(x, eps=NORM_EPS):
    return x * lax.rsqrt(jnp.mean(x * x, axis=-1, keepdims=True) + eps)


def _bdot(a, b):
    return jnp.dot(a.astype(BF16), b.astype(BF16), preferred_element_type=jnp.float32)


def _bdot_nt(a, b):
    return lax.dot_general(a.astype(BF16), b.astype(BF16), (((1,), (1,)), ((), ())),
                           preferred_element_type=jnp.float32)


def _const_spec(shape):
    nd = len(shape)
    return pl.BlockSpec(shape, lambda *_: (0,) * nd, pipeline_mode=pl.Buffered(1))


def _rwkv_in_kernel(x_ref, xp_ref, g_ref, mu_ref, win_ref, w0_ref, w1_ref, w2_ref, a0_ref,
                    a1_ref, a2_ref, kk_ref, ka_ref,
                    r_out, lw_out, cum_out, k_out, v_out, kk_out, a_out, z_out, *, tiles_per_seq):
    d = D_MODEL
    g = g_ref[...]
    h = _rms(x_ref[...]) * g
    hp = _rms(xp_ref[7:8, :]) * g
    hp = jnp.where(pl.program_id(0) % tiles_per_seq == 0, 0.0, hp)
    row = lax.broadcasted_iota(jnp.int32, h.shape, 0)
    hs = jnp.where(row == 0, hp, pltpu.roll(h, 1, axis=0))
    dh = hs - h

    def mix(c):
        return h + dh * mu_ref[c:c + 1, :]

    r_out[...] = _bdot(mix(0), win_ref[:, 0 * d:1 * d]).astype(r_out.dtype)
    k = _bdot(mix(1), win_ref[:, 1 * d:2 * d])
    v_out[...] = _bdot(mix(2), win_ref[:, 2 * d:3 * d]).astype(v_out.dtype)
    z_out[...] = _bdot(mix(3), win_ref[:, 3 * d:4 * d]).astype(z_out.dtype)

    u = w0_ref[...] + _bdot(jnp.tanh(_bdot(mix(4), w1_ref[...])), w2_ref[...])
    sp = jnp.maximum(-u, 0.0) + jnp.log(1.0 + jnp.exp(-jnp.abs(u)))
    lw = -jnp.exp(-sp - 0.5)
    lw_out[...] = lw
    rows = min(lw.shape[0], MXU_DEPTH)
    ri = lax.broadcasted_iota(jnp.int32, (rows, rows), 0)
    ci = lax.broadcasted_iota(jnp.int32, (rows, rows), 1)
    same_chunk = jnp.right_shift(ri, CHUNK_SHIFT) == jnp.right_shift(ci, CHUNK_SHIFT)
    tril = jnp.where(ci <= ri, jnp.where(same_chunk, 1.0, 0.0), 0.0).astype(BF16)
    for r0 in range(0, lw.shape[0], rows):
        hi, mid, lo = _split3(lw[r0:r0 + rows])
        cum_out[r0:r0 + rows, :] = (jnp.dot(tril, hi, preferred_element_type=jnp.float32)
                                    + jnp.dot(tril, mid, preferred_element_type=jnp.float32)
                                    + jnp.dot(tril, lo, preferred_element_type=jnp.float32))
    ua = a0_ref[...] + _bdot(_bdot(mix(5), a1_ref[...]), a2_ref[...])
    a = 1.0 / (1.0 + jnp.exp(-ua))
    a_out[...] = a.astype(a_out.dtype)
    kk_out[...] = (k * kk_ref[...]).astype(kk_out.dtype)
    k_out[...] = (k * (1.0 + (a - 1.0) * ka_ref[...])).astype(k_out.dtype)


def _rwkv_in(x2d, seq, g, mu, w_in, w0, w1, w2, a0, a1, a2, k_k, k_a, tm=512):
    bt, d = x2d.shape
    tiles_per_seq = seq // tm
    row = lambda i: (i, 0)
    prev = lambda i: (jnp.maximum(i * (tm // 8) - 1, 0), 0)
    f32_sds = jax.ShapeDtypeStruct((bt, d), jnp.float32)
    b16_sds = jax.ShapeDtypeStruct((bt, d), BF16)
    return pl.pallas_call(
        functools.partial(_rwkv_in_kernel, tiles_per_seq=tiles_per_seq),
        grid=(bt // tm,),
        in_specs=[pl.BlockSpec((tm, d), row), pl.BlockSpec((8, d), prev),
                  _const_spec((1, d)), _const_spec((6, d)), _const_spec((d, 4 * d)),
                  _const_spec((1, d)), _const_spec((d, DECAY_LORA)), _const_spec((DECAY_LORA, d)),
                  _const_spec((1, d)), _const_spec((d, ICLR_LORA)), _const_spec((ICLR_LORA, d)),
                  _const_spec((1, d)), _const_spec((1, d))],
        out_specs=[pl.BlockSpec((tm, d), row)] * 8,
        out_shape=[b16_sds, f32_sds, f32_sds] + [b16_sds] * 5,
        compiler_params=pltpu.CompilerParams(dimension_semantics=("parallel",),
                                             vmem_limit_bytes=VMEM_LIMIT),
    )(x2d, x2d, g.reshape(1, d), mu, w_in.astype(BF16), w0.reshape(1, d), w1.astype(BF16),
      w2.astype(BF16), a0.reshape(1, d), a1.astype(BF16), a2.astype(BF16), k_k.reshape(1, d),
      k_a.reshape(1, d))


def _split3(x):
    hi = x.astype(BF16)
    r1 = x - hi.astype(x.dtype)
    mid = r1.astype(BF16)
    lo = (r1 - mid.astype(x.dtype)).astype(BF16)
    return hi, mid, lo


def _wkv_kernel(r_ref, lw_ref, cum_ref, k_ref, v_ref, kk_ref, a_ref, rk_ref, lng_ref, lnb_ref,
                o_ref, state_ref):
    @pl.when(pl.program_id(1) == 0)
    def _():
        state_ref[...] = jnp.zeros_like(state_ref)

    c = CHUNK
    nh = WKV_HEADS
    uw = nh * HEAD_DIM
    pairs = range(N_HEADS // nh)
    sls = [slice(p * uw, (p + 1) * uw) for p in pairs]
    m0 = lax.broadcasted_iota(jnp.int32, (c, LANES), 1) < HEAD_DIM
    lane_head = jnp.right_shift(lax.broadcasted_iota(jnp.int32, (c, uw), 1), HEAD_SHIFT)
    head_masks = [lane_head == h for h in range(nh)]

    def head_sum(x):
        out = []
        for j in range(uw // LANES):
            xj = x[:, j * LANES:(j + 1) * LANES]
            s0 = jnp.sum(jnp.where(m0, xj, 0.0), axis=-1, keepdims=True)
            s1 = jnp.sum(jnp.where(m0, 0.0, xj), axis=-1, keepdims=True)
            out.append(jnp.where(m0, s0, s1))
        return jnp.concatenate(out, axis=1)

    def stack(y):
        return jnp.concatenate([jnp.where(m, y, 0.0) for m in head_masks], axis=0)

    def each(f, *lists):
        return [f(*xs) for xs in zip(*lists)]

    n_chunks = r_ref.shape[0] // c
    n_units = len(sls)
    items = [(slice(ci * c, (ci + 1) * c), s) for ci in range(n_chunks) for s in sls]
    f32 = cum_ref.dtype
    r = [r_ref[rows, s].astype(f32) for rows, s in items]
    k = [k_ref[rows, s].astype(f32) for rows, s in items]
    v = [v_ref[rows, s].astype(f32) for rows, s in items]
    cum = [cum_ref[rows, s] for rows, s in items]

    def prep(i):
        rows, s = items[i]
        kkraw = kk_ref[rows, s].astype(f32)
        kk = kkraw / jnp.maximum(jnp.sqrt(head_sum(kkraw * kkraw)), 1e-12)
        b = kk * a_ref[rows, s].astype(f32)
        g_inv = jnp.exp(-cum[i])
        at = -kk * jnp.exp(cum[i] - lw_ref[rows, s])
        rt = r[i] * jnp.exp(cum[i])
        ar = jnp.concatenate([at, rt], axis=0)
        kb_t = jnp.concatenate([stack(k[i] * g_inv), stack(b * g_inv)], axis=0)
        return b, ar, kb_t

    b, ar, kb_t = zip(*[prep(i) for i in range(len(items))])
    big = each(_bdot_nt, ar, kb_t)

    w = nh * c
    row2 = lax.broadcasted_iota(jnp.int32, (c, w), 0)
    col2 = jnp.bitwise_and(lax.broadcasted_iota(jnp.int32, (c, w), 1), c - 1)
    strict = col2 < row2
    incl = col2 <= row2
    eye2 = jnp.where(col2 == row2, 1.0, 0.0)

    x = [jnp.where(strict, g[:c, w:], 0.0) for g in big]
    t = [eye2 + xi for xi in x]
    x = each(lambda xi: _bdot(xi, stack(xi)), x)
    for _ in range(4):
        tx = each(lambda ti, xi: _bdot(jnp.concatenate([ti, xi], axis=0), stack(xi)), t, x)
        t = each(lambda ti, r_: ti + r_[:c], t, tx)
        x = [r_[c:] for r_ in tx]
    t = each(lambda ti, xi: ti + _bdot(ti, stack(xi)), t, x)

    av = each(lambda g, vi: _bdot(
        jnp.concatenate([jnp.where(strict, g[:c, :w], 0.0),
                         jnp.where(incl, g[c:, :w], 0.0)], axis=0), stack(vi)), big, v)
    diag = (jnp.right_shift(lax.broadcasted_iota(jnp.int32, (uw, uw), 0), HEAD_SHIFT)
            == jnp.right_shift(lax.broadcasted_iota(jnp.int32, (uw, uw), 1), HEAD_SHIFT))
    state = [state_ref[p] for p in pairs]
    for ci in range(n_chunks):
        ix = range(ci * n_units, (ci + 1) * n_units)
        ar_s = [_bdot_nt(ar[i], st) for i, st in zip(ix, state)]
        u = [_bdot(t[i], stack(s[:c] + av[i][:c])) for i, s in zip(ix, ar_s)]
        o = [s[c:] + av[i][c:] + _bdot(jnp.where(incl, big[i][c:, w:], 0.0), stack(ui))
             for i, s, ui in zip(ix, ar_s, u)]
        new_state = []
        for i, st, ui in zip(ix, state, u):
            cum_last = cum[i][c - 1:c, :]
            g_rem = jnp.exp(cum_last - cum[i])
            vu = jnp.concatenate([v[i], ui], axis=0)
            kb = jnp.concatenate([k[i] * g_rem, b[i] * g_rem], axis=0)
            upd = _bdot(vu.T, kb)
            new_state.append(st * jnp.exp(cum_last) + jnp.where(diag, upd, 0.0))
        state = new_state

        for i, oi in zip(ix, o):
            rows, s = items[i]
            mean = head_sum(oi) * (1.0 / HEAD_DIM)
            dev = oi - mean
            var = head_sum(dev * dev) * (1.0 / HEAD_DIM)
            y = dev * lax.rsqrt(var + LN_X_EPS) * lng_ref[:, s] + lnb_ref[:, s]
            o_ref[rows, s] = (y + head_sum(r[i] * k[i] * rk_ref[:, s]) * v[i]).astype(o_ref.dtype)
    for p in pairs:
        state_ref[p] = state[p]


def _wkv(r, lw, cum, k, v, kkraw, a, seq, r_k, lnx_g, lnx_b, chunks_per_step=4):
    bt, d = r.shape
    nb = bt // seq
    rows = chunks_per_step * CHUNK
    nc = seq // rows
    row = lambda b, c: (b * nc + c, 0)
    blk = pl.BlockSpec((rows, d), row)
    return pl.pallas_call(
        _wkv_kernel,
        grid=(nb, nc),
        in_specs=[blk] * 7 + [_const_spec((1, d))] * 3,
        out_specs=blk,
        out_shape=jax.ShapeDtypeStruct((bt, d), BF16),
        scratch_shapes=[pltpu.VMEM((N_HEADS // WKV_HEADS, WKV_HEADS * HEAD_DIM,
                                    WKV_HEADS * HEAD_DIM), jnp.float32)],
        compiler_params=pltpu.CompilerParams(dimension_semantics=("parallel", "arbitrary"),
                                             vmem_limit_bytes=VMEM_LIMIT),
    )(r, lw, cum, k, v, kkraw, a, r_k.reshape(1, d), lnx_g.reshape(1, d), lnx_b.reshape(1, d))


def _out_proj_kernel(x_ref, o_ref, z_ref, w_ref, y_ref):
    x = x_ref[...]
    z = z_ref[...].astype(x.dtype)
    y = o_ref[...].astype(x.dtype) * (z / (1.0 + jnp.exp(-z)))
    y_ref[...] = x + _bdot(y, w_ref[...])


def _out_proj(x2d, o, z, w_out, tm=512):
    bt, d = x2d.shape
    row = lambda i: (i, 0)
    blk = pl.BlockSpec((tm, d), row)
    return pl.pallas_call(
        _out_proj_kernel,
        grid=(bt // tm,),
        in_specs=[blk, blk, blk, _const_spec((d, d))],
        out_specs=blk,
        out_shape=jax.ShapeDtypeStruct((bt, d), jnp.float32),
        compiler_params=pltpu.CompilerParams(dimension_semantics=("parallel",),
                                             vmem_limit_bytes=VMEM_LIMIT),
    )(x2d, o, z, w_out.astype(BF16))


def _rwkv_layer(x2d, seq, g, mu, w_in, w0, w1, w2, a0, a1, a2, k_k, k_a, r_k, lnx_g, lnx_b, w_out):
    r, lw, cum, k, v, kkraw, a, z = _rwkv_in(x2d, seq, g, mu, w_in, w0, w1, w2, a0, a1, a2, k_k,
                                             k_a)
    o = _wkv(r, lw, cum, k, v, kkraw, a, seq, r_k, lnx_g, lnx_b)
    return _out_proj(x2d, o, z, w_out)


def _nsa_in_kernel(x_ref, g_ref, w_ref, qg_ref, kg_ref, q_out, kc_out, vc_out, ks_out, vs_out,
                   kw_out, vw_out, z_out, gate_out, *, tiles_per_seq):
    d = D_MODEL
    h = (_rms(x_ref[...]) * g_ref[...]).astype(BF16)
    tm = h.shape[0]

    def proj(off, width):
        return jnp.dot(h, w_ref[:, off:off + width], preferred_element_type=jnp.float32)

    lane = lax.broadcasted_iota(jnp.int32, (tm, LANES), 1)
    low_lanes = lane < HEAD_DIM
    tok = (pl.program_id(0) % tiles_per_seq) * tm + lax.broadcasted_iota(jnp.int32, (tm, LANES), 0)
    blk_bias = jnp.where(lane - HEAD_DIM == jnp.right_shift(tok, SLC_SHIFT), 1.0, 0.0)

    def slot(p, g):
        two = p[:, (g // PAIR) * LANES:(g // PAIR + 1) * LANES]
        return pltpu.roll(two, HEAD_DIM, axis=1) if g % PAIR else two

    def narrow(ref, p, gain_row):
        for g in range(N_KV_GROUPS):
            seg = p[:, g * HEAD_DIM:(g + 1) * HEAD_DIM]
            if gain_row is not None:
                seg = _rms(seg) * kg_ref[gain_row:gain_row + 1, :HEAD_DIM]
            ref[0, g] = seg.astype(ref.dtype)

    def values(ref, p):
        for g in range(N_KV_GROUPS):
            ref[0, g] = jnp.where(low_lanes, slot(p, g), 1.0).astype(ref.dtype)

    p = proj(0, d)
    for j in range(N_PAIRS):
        x = p[:, j * LANES:(j + 1) * LANES]
        sq = x * x
        ms = jnp.where(low_lanes, jnp.sum(jnp.where(low_lanes, sq, 0.0), axis=-1, keepdims=True),
                       jnp.sum(jnp.where(low_lanes, 0.0, sq), axis=-1, keepdims=True))
        q_out[:, j * LANES:(j + 1) * LANES] = (
            x * lax.rsqrt(ms * (1.0 / HEAD_DIM) + NORM_EPS) * qg_ref[...]).astype(q_out.dtype)
    narrow(kc_out, proj(d, KV_WIDTH), None)
    narrow(vc_out, proj(d + KV_WIDTH, KV_WIDTH), None)
    p = proj(d + 2 * KV_WIDTH, KV_WIDTH)
    for g in range(N_KV_GROUPS):
        x = jnp.where(low_lanes, slot(p, g), 0.0)
        ms = jnp.sum(x * x, axis=-1, keepdims=True) * (1.0 / HEAD_DIM)
        ks_out[0, g] = (x * lax.rsqrt(ms + NORM_EPS) * kg_ref[1:2, :] + blk_bias).astype(BF16)
    values(vs_out, proj(d + 3 * KV_WIDTH, KV_WIDTH))
    narrow(kw_out, proj(d + 4 * KV_WIDTH, KV_WIDTH), 2)
    values(vw_out, proj(d + 5 * KV_WIDTH, KV_WIDTH))
    z_out[...] = proj(d + 6 * KV_WIDTH, d).astype(z_out.dtype)
    gate_out[...] = proj(2 * d + 6 * KV_WIDTH, LANES)


def _nsa_in(x2d, nb, seq, g, w_in, q_gain, k_gain, tm=256):
    bt, d = x2d.shape
    tps = seq // tm
    width = w_in.shape[1]
    padded = d + 6 * KV_WIDTH + d + LANES
    w = jnp.pad(w_in, ((0, 0), (0, padded - width))).astype(BF16)
    row = lambda i: (i, 0)
    grp = lambda i: (i // tps, 0, i % tps, 0)
    kv_blk = pl.BlockSpec((1, N_KV_GROUPS, tm, HEAD_DIM), grp)
    v_blk = pl.BlockSpec((1, N_KV_GROUPS, tm, LANES), grp)
    kv_f32 = jax.ShapeDtypeStruct((nb, N_KV_GROUPS, seq, HEAD_DIM), jnp.float32)
    k_b16 = jax.ShapeDtypeStruct((nb, N_KV_GROUPS, seq, HEAD_DIM), BF16)
    v_b16 = jax.ShapeDtypeStruct((nb, N_KV_GROUPS, seq, LANES), BF16)
    wide = jax.ShapeDtypeStruct((bt, d), BF16)
    assert seq // SLC_BLOCK <= HEAD_DIM
    qg = jnp.tile(q_gain * (LOG2_E * HEAD_DIM ** -0.5), PAIR).reshape(1, LANES)
    return pl.pallas_call(
        functools.partial(_nsa_in_kernel, tiles_per_seq=tps),
        grid=(bt // tm,),
        in_specs=[pl.BlockSpec((tm, d), row), _const_spec((1, d)), _const_spec((d, padded)),
                  _const_spec((1, LANES)), _const_spec((N_BRANCHES, LANES))],
        out_specs=[pl.BlockSpec((tm, d), row), kv_blk, kv_blk, v_blk, v_blk, kv_blk, v_blk,
                   pl.BlockSpec((tm, d), row), pl.BlockSpec((tm, LANES), row)],
        out_shape=[wide, kv_f32, kv_f32, v_b16, v_b16, k_b16, v_b16, wide,
                   jax.ShapeDtypeStruct((bt, LANES), jnp.float32)],
        compiler_params=pltpu.CompilerParams(dimension_semantics=("parallel",),
                                             vmem_limit_bytes=VMEM_LIMIT),
    )(x2d, g.reshape(1, d), w, qg, jnp.pad(k_gain, ((0, 0), (0, LANES - HEAD_DIM))))


def _gelu_tanh(x):
    return 0.5 * x * (1.0 + jnp.tanh(0.7978845608028654 * (x + 0.044715 * (x * x * x))))


def _compress_kernel(uk_ref, uv_ref, pe_ref, w1_ref, b1_ref, w2_ref, kg_ref, kc_out, vc_out):
    half = CMP_STRIDE * HEAD_DIM
    for kv, (u_ref, out) in enumerate(((uk_ref, kc_out), (uv_ref, vc_out))):
        uh = u_ref[0, 0]
        rows = uh.shape[0]
        ha = _bdot(uh + pe_ref[kv, 0:1, :], w1_ref[kv, :half, :])
        hb = _bdot(uh + pe_ref[kv, 1:2, :], w1_ref[kv, half:, :])
        hid = _gelu_tanh(ha + pltpu.roll(hb, rows - 1, axis=0) + b1_ref[kv])
        y = _bdot(hid, w2_ref[kv])
        if kv == 0:
            y = y[:, :HEAD_DIM]
            out[0, 0] = (_rms(y) * kg_ref[0:1, :]).astype(out.dtype)
        else:
            low_lanes = lax.broadcasted_iota(jnp.int32, y.shape, 1) < HEAD_DIM
            out[0, 0] = jnp.where(low_lanes, y, 1.0).astype(out.dtype)


def _compress(kc_raw, vc_raw, pe, w1, b1, w2, k_gain):
    nb, ng, seq, n = kc_raw.shape
    rows = seq // CMP_STRIDE
    half = CMP_STRIDE * n
    uk = kc_raw.reshape(nb, ng, rows, half)
    uv = vc_raw.reshape(nb, ng, rows, half)
    whole = lambda b, g: (b, g, 0, 0)
    u_blk = pl.BlockSpec((1, 1, rows, half), whole)
    w2p = jnp.pad(w2, ((0, 0), (0, 0), (0, LANES - n))).astype(BF16)
    return pl.pallas_call(
        _compress_kernel,
        grid=(nb, ng),
        in_specs=[u_blk, u_blk, _const_spec((2, 2, half)), _const_spec((2, 2 * half, CMP_HIDDEN)),
                  _const_spec((2, 1, CMP_HIDDEN)), _const_spec((2, CMP_HIDDEN, LANES)),
                  _const_spec((N_BRANCHES, n))],
        out_specs=[pl.BlockSpec((1, 1, rows, n), whole), pl.BlockSpec((1, 1, rows, LANES), whole)],
        out_shape=[jax.ShapeDtypeStruct((nb, ng, rows, n), BF16),
                   jax.ShapeDtypeStruct((nb, ng, rows, LANES), BF16)],
        compiler_params=pltpu.CompilerParams(dimension_semantics=("parallel", "parallel"),
                                             vmem_limit_bytes=VMEM_LIMIT),
    )(uk, uv, pe.reshape(2, 2, half), w1.astype(BF16), b1.reshape(2, 1, CMP_HIDDEN), w2p, k_gain)


M_INIT = -1e20


LOG2_E = 1.4426950408889634


MAX_STATIC_BOUND = 50.0


def _softmax_step(s, bias, v_tile, carry, hg, qt, online):
    sb = [s[h * qt:(h + 1) * qt] for h in range(hg)]
    if bias is not None:
        sb = [x + bias for x in sb]
    if not online:
        p = jnp.concatenate([jnp.exp2(x).astype(BF16) for x in sb], axis=0)
        return carry + jnp.dot(p, v_tile, preferred_element_type=jnp.float32)
    m, acc = carry
    m_new = jnp.maximum(m, jnp.concatenate(
        [jnp.max(x, axis=-1, keepdims=True) for x in sb], axis=0))
    p = jnp.concatenate(
        [jnp.exp2(sb[h] - m_new[h * qt:(h + 1) * qt]).astype(BF16) for h in range(hg)], axis=0)
    acc = jnp.exp2(m - m_new) * acc + jnp.dot(p, v_tile, preferred_element_type=jnp.float32)
    return m_new, acc


def _nsa_attn_kernel(bound_ref, q_ref, gate_ref, bcast_ref, kc_ref, vc_ref, ks_ref, vs_ref,
                     kw_ref, vw_ref, o_ref, *, qt, kt, seq, online):
    hg = HEADS_PER_GROUP
    t0 = pl.program_id(2) * qt
    n_slc = seq // SLC_BLOCK
    n_cmp_rows = kc_ref.shape[2]
    rows = hg * qt
    low_lanes = lax.broadcasted_iota(jnp.int32, (qt, LANES), 1) < HEAD_DIM
    keep_c, keep_s, keep_w = (0.0, 0.0, 0.0) if online else (
        -bound_ref[0], -bound_ref[1], -bound_ref[2])

    def swap_halves(x):
        return pltpu.roll(x, HEAD_DIM, axis=1)

    q = q_ref[...].astype(jnp.float32)
    q_n = []
    for h in range(hg):
        two = q[:, (h // PAIR) * LANES:(h // PAIR + 1) * LANES]
        q_n.append(jnp.where(low_lanes, swap_halves(two) if h % PAIR else two, 0.0))
    qs = jnp.concatenate([x[:, :HEAD_DIM] for x in q_n], axis=0).astype(BF16)
    t_rows = t0 + lax.broadcasted_iota(jnp.int32, (qt, 1), 0)

    span = WINDOW + qt
    w0 = pl.multiple_of(jnp.maximum(t0 - WINDOW, 0), qt)
    s = _bdot_nt(qs, kc_ref[0, 0])
    s_win = _bdot_nt(qs, kw_ref[0, 0, pl.ds(w0, span), :])

    cmp_end = lax.broadcasted_iota(jnp.int32, (1, n_cmp_rows), 1) * CMP_STRIDE + (CMP_BLOCK - 1)
    cbias = jnp.where(cmp_end <= t_rows, keep_c, NEG_INF)
    p_heads = []
    for h in range(hg):
        sb = s[h * qt:(h + 1) * qt] + cbias
        if online:
            sb = sb - jnp.maximum(jnp.max(sb, axis=-1, keepdims=True), M_INIT)
        e = jnp.exp2(sb)
        p_heads.append(e / jnp.maximum(jnp.sum(e, axis=-1, keepdims=True), 1e-37))
    o_cmp = jnp.dot(jnp.concatenate([p.astype(BF16) for p in p_heads], axis=0), vc_ref[0, 0],
                    preferred_element_type=jnp.float32)

    p_sum = p_heads[0]
    for h in range(1, hg):
        p_sum = p_sum + p_heads[h]
    jn = lax.broadcasted_iota(jnp.int32, (n_slc, n_cmp_rows), 0) * SLC_BLOCK
    nn = lax.broadcasted_iota(jnp.int32, (n_slc, n_cmp_rows), 1) * CMP_STRIDE
    ov_t = jnp.where((nn < jn + SLC_BLOCK) & (nn + (CMP_BLOCK - 1) >= jn), 1.0, 0.0).astype(BF16)
    p_hi = p_sum.astype(BF16)
    p_lo = (p_sum - p_hi.astype(p_sum.dtype)).astype(BF16)
    imp_t = (lax.dot_general(ov_t, p_hi, (((1,), (1,)), ((), ())), preferred_element_type=jnp.float32)
             + lax.dot_general(ov_t, p_lo, (((1,), (1,)), ((), ())), preferred_element_type=jnp.float32))

    lag = t_rows - (w0 + lax.broadcasted_iota(jnp.int32, (1, span), 1))
    wbias = jnp.where(lag >= 0, jnp.where(lag < WINDOW, keep_w, NEG_INF), NEG_INF)
    p_win = []
    for h in range(hg):
        sb = s_win[h * qt:(h + 1) * qt] + wbias
        if online:
            sb = sb - jnp.max(sb, axis=-1, keepdims=True)
        p_win.append(jnp.exp2(sb).astype(BF16))
    acc_win = jnp.dot(jnp.concatenate(p_win, axis=0), vw_ref[0, 0, pl.ds(w0, span), :],
                      preferred_element_type=jnp.float32)

    jb = lax.broadcasted_iota(jnp.int32, (n_slc, qt), 0)
    tq = t0 + lax.broadcasted_iota(jnp.int32, (n_slc, qt), 1)
    dist = jnp.right_shift(tq, SLC_SHIFT) - jb
    forced = (jb == 0) | ((dist >= 0) & (dist < N_LOCAL_BLOCKS))
    score = jnp.where(dist >= 0, imp_t + jnp.where(forced, FORCE_BONUS, 0.0), -jnp.inf)
    sub = 8
    groups = [score[v * sub:(v + 1) * sub] for v in range(n_slc // sub)]
    jrow = lax.broadcasted_iota(jnp.int32, (sub, qt), 0)
    cnt = [jnp.zeros((sub, qt), jnp.float32) for _ in groups]
    for j2 in range(n_slc):
        row = score[j2:j2 + 1, :]
        for v, sv in enumerate(groups):
            if v * sub > j2:
                inc = jnp.where(row >= sv, 1.0, 0.0)
            elif (v + 1) * sub - 1 <= j2:
                inc = jnp.where(row > sv, 1.0, 0.0)
            else:
                inc = jnp.where(jrow + v * sub > j2, jnp.where(row >= sv, 1.0, 0.0),
                                jnp.where(row > sv, 1.0, 0.0))
            cnt[v] = cnt[v] + inc
    cnt = jnp.concatenate(cnt, axis=0)
    blk_bias_t = jnp.where(dist >= 0, jnp.where(cnt < min(SLC_TOPK, n_slc), keep_s, NEG_INF),
                           NEG_INF)
    parts = [jnp.zeros((HEAD_DIM, qt), jnp.float32), blk_bias_t]
    if n_slc < HEAD_DIM:
        parts.append(jnp.zeros((HEAD_DIM - n_slc, qt), jnp.float32))
    blk_bias = jnp.concatenate(parts, axis=0).T
    q_aug = jnp.concatenate([x + blk_bias for x in q_n], axis=0).astype(BF16)

    def slc_body(i, carry):
        k0 = pl.multiple_of(i * kt, kt)
        s = _bdot_nt(q_aug, ks_ref[0, 0, pl.ds(k0, kt), :])
        return _softmax_step(s, None, vs_ref[0, 0, pl.ds(k0, kt), :], carry, hg, qt, online)

    init = jnp.zeros((rows, LANES), jnp.float32)
    if online:
        init = (jnp.full((rows, 1), M_INIT, jnp.float32), init)
    n_full = t0 // kt
    carry = lax.fori_loop(0, n_full // 2, lambda j, c: slc_body(2 * j + 1, slc_body(2 * j, c)),
                          init)
    carry = lax.cond(n_full % 2 == 1, lambda c: slc_body(n_full - 1, c), lambda c: c, carry)
    k0 = pl.multiple_of(n_full * kt, kt)
    s = _bdot_nt(q_aug, ks_ref[0, 0, pl.ds(k0, kt), :])
    causal = jnp.where(k0 + lax.broadcasted_iota(jnp.int32, (1, kt), 1) <= t_rows, 0.0, NEG_INF)
    acc_slc = _softmax_step(s, causal, vs_ref[0, 0, pl.ds(k0, kt), :], carry, hg, qt, online)
    if online:
        acc_slc = acc_slc[1]

    gates = 1.0 / (1.0 + jnp.exp(-gate_ref[...]))
    spread = jnp.dot(gates.astype(BF16), bcast_ref[0],
                     preferred_element_type=jnp.float32)

    def pair(acc, j, normalise):
        a0 = acc[(2 * j) * qt:(2 * j + 1) * qt]
        a1 = acc[(2 * j + 1) * qt:(2 * j + 2) * qt]
        num = jnp.where(low_lanes, a0, swap_halves(a1))
        if not normalise:
            return num
        return num / jnp.where(low_lanes, swap_halves(a0), a1)

    for j in range(hg // PAIR):
        width = PAIR * HEAD_DIM
        o = jnp.zeros((qt, width), jnp.float32)
        for br, (acc, normalise) in enumerate(((o_cmp, False), (acc_slc, True), (acc_win, True))):
            col = (br * (hg // PAIR) + j) * width
            o = o + spread[:, col:col + width] * pair(acc, j, normalise)
        o_ref[:, j * width:(j + 1) * width] = o.astype(o_ref.dtype)


def _nsa_attn(q, gate, q_gain, k_gain, kc, vc, ks, vs, kw, vw, qt=256, kt=512):
    bt, d = q.shape
    nb, ng, seq, n = kw.shape
    kt = min(kt, seq)
    assert seq >= WINDOW + qt and seq % kt == 0 and kt % qt == 0
    nq = seq // qt
    hg = HEADS_PER_GROUP
    width = hg * n
    rows_c = kc.shape[2]
    col = jnp.arange(N_BRANCHES * width) // n
    src = (col // hg) * N_HEADS + jnp.arange(ng)[:, None] * hg + col % hg
    bcast = (jnp.arange(LANES)[None, :, None] == src[:, None, :]).astype(BF16)
    whole = lambda b, g, i: (b, g, 0, 0)
    q_blk = pl.BlockSpec((qt, width), lambda b, g, i: (b * nq + i, g))
    g_blk = pl.BlockSpec((qt, LANES), lambda b, g, i: (b * nq + i, 0))
    b_blk = pl.BlockSpec((1, LANES, N_BRANCHES * width), lambda b, g, i: (g, 0, 0))
    kc_blk = pl.BlockSpec((1, 1, rows_c, n), whole)
    vc_blk = pl.BlockSpec((1, 1, rows_c, LANES), whole)
    k_blk = pl.BlockSpec((1, 1, seq, n), whole)
    v_blk = pl.BlockSpec((1, 1, seq, LANES), whole)
    bounds = (1.01 * LOG2_E * n ** 0.5) * jnp.max(jnp.abs(q_gain)) * jnp.max(jnp.abs(k_gain), axis=1)

    def call(online, *args):
        return pl.pallas_call(
            functools.partial(_nsa_attn_kernel, qt=qt, kt=kt, seq=seq, online=online),
            grid=(nb, ng, nq),
            in_specs=[pl.BlockSpec(memory_space=pltpu.SMEM), q_blk, g_blk, b_blk,
                      kc_blk, vc_blk, v_blk, v_blk, k_blk, v_blk],
            out_specs=q_blk,
            out_shape=jax.ShapeDtypeStruct((bt, d), BF16),
            compiler_params=pltpu.CompilerParams(
                dimension_semantics=("parallel", "parallel", "arbitrary"),
                vmem_limit_bytes=VMEM_LIMIT),
        )(*args)

    return lax.cond(jnp.max(bounds) <= MAX_STATIC_BOUND,
                    functools.partial(call, False), functools.partial(call, True),
                    bounds.astype(jnp.float32), q, gate, bcast, kc, vc, ks, vs, kw, vw)


def _nsa_layer(x2d, nb, seq, g, w_in, q_gain, k_gain, cmp_pe, cmp_w1, cmp_b1, cmp_w2, w_out):
    q, kc_raw, vc_raw, ks, vs, kw, vw, z, gate = _nsa_in(x2d, nb, seq, g, w_in, q_gain, k_gain)
    kc, vc = _compress(kc_raw, vc_raw, cmp_pe, cmp_w1, cmp_b1, cmp_w2, k_gain)
    o = _nsa_attn(q, gate, q_gain, k_gain, kc, vc, ks, vs, kw, vw)
    return _out_proj(x2d, o, z, w_out)


def kernel(x, norm_g, rwkv_mu, rwkv_w_in, rwkv_w0, rwkv_w1, rwkv_w2, rwkv_a0, rwkv_a1, rwkv_a2, rwkv_k_k, rwkv_k_a, rwkv_r_k, rwkv_lnx_g, rwkv_lnx_b, rwkv_w_out, nsa_w_in, nsa_q_gain, nsa_k_gain, nsa_cmp_pe, nsa_cmp_w1, nsa_cmp_b1, nsa_cmp_w2, nsa_w_out):
    b, t, d = x.shape
    x2d = x.reshape(b * t, d)
    x2d = _rwkv_layer(x2d, t, norm_g[0], rwkv_mu[0], rwkv_w_in[0], rwkv_w0[0], rwkv_w1[0],
                      rwkv_w2[0], rwkv_a0[0], rwkv_a1[0], rwkv_a2[0], rwkv_k_k[0], rwkv_k_a[0],
                      rwkv_r_k[0].reshape(-1), rwkv_lnx_g[0], rwkv_lnx_b[0], rwkv_w_out[0])
    x2d = _nsa_layer(x2d, b, t, norm_g[1], nsa_w_in[0], nsa_q_gain[0], nsa_k_gain[0],
                     nsa_cmp_pe[0], nsa_cmp_w1[0], nsa_cmp_b1[0], nsa_cmp_w2[0], nsa_w_out[0])
    return x2d.reshape(b, t, d)
```

```python
import functools

import jax
import jax.numpy as jnp
from jax import lax
from jax.experimental import pallas as pl
from jax.experimental.pallas import tpu as pltpu

D_MODEL = 1024
HEAD_DIM = 64
N_HEADS = D_MODEL // HEAD_DIM
NORM_EPS = 1e-6
LN_X_EPS = 64e-5
DECAY_LORA = 64
ICLR_LORA = 64
N_KV_GROUPS = 4
HEADS_PER_GROUP = N_HEADS // N_KV_GROUPS
KV_WIDTH = N_KV_GROUPS * HEAD_DIM
N_BRANCHES = 3
CMP_BLOCK = 32
CMP_STRIDE = 16
CMP_HIDDEN = 256
SLC_BLOCK = 64
SLC_SHIFT = 6
SLC_TOPK = 16
N_LOCAL_BLOCKS = 2
WINDOW = 512
FORCE_BONUS = 1e4
NEG_INF = -1e30

LANES = 128
MXU_DEPTH = 256
PAIR = LANES // HEAD_DIM
N_PAIRS = N_HEADS // PAIR
HEAD_SHIFT = 6
WKV_HEADS = LANES // HEAD_DIM
CHUNK = 64
CHUNK_SHIFT = 6
VMEM_LIMIT = 48 * 1024 * 1024

BF16 = jnp.bfloat16


def _rms(x, eps=NORM_EPS):
    return x * lax.rsqrt(jnp.mean(x * x, axis=-1, keepdims=True) + eps)


def _bdot(a, b):
    return jnp.dot(a.astype(BF16), b.astype(BF16), preferred_element_type=jnp.float32)


def _bdot_nt(a, b):
    return lax.dot_general(a.astype(BF16), b.astype(BF16), (((1,), (1,)), ((), ())),
                           preferred_element_type=jnp.float32)


def _const_spec(shape):
    nd = len(shape)
    return pl.BlockSpec(shape, lambda *_: (0,) * nd, pipeline_mode=pl.Buffered(1))


def _rwkv_in_kernel(x_ref, xp_ref, g_ref, mu_ref, win_ref, w0_ref, w1_ref, w2_ref, a0_ref,
                    a1_ref, a2_ref, kk_ref, ka_ref,
                    r_out, lw_out, cum_out, k_out, v_out, kk_out, a_out, z_out, *, tiles_per_seq):
    d = D_MODEL
    g = g_ref[...]
    h = _rms(x_ref[...]) * g
    hp = _rms(xp_ref[7:8, :]) * g
    hp = jnp.where(pl.program_id(0) % tiles_per_seq == 0, 0.0, hp)
    row = lax.broadcasted_iota(jnp.int32, h.shape, 0)
    hs = jnp.where(row == 0, hp, pltpu.roll(h, 1, axis=0))
    dh = hs - h

    def mix(c):
        return h + dh * mu_ref[c:c + 1, :]

    r_out[...] = _bdot(mix(0), win_ref[:, 0 * d:1 * d]).astype(r_out.dtype)
    k = _bdot(mix(1), win_ref[:, 1 * d:2 * d])
    v_out[...] = _bdot(mix(2), win_ref[:, 2 * d:3 * d]).astype(v_out.dtype)
    z_out[...] = _bdot(mix(3), win_ref[:, 3 * d:4 * d]).astype(z_out.dtype)

    u = w0_ref[...] + _bdot(jnp.tanh(_bdot(mix(4), w1_ref[...])), w2_ref[...])
    sp = jnp.maximum(-u, 0.0) + jnp.log(1.0 + jnp.exp(-jnp.abs(u)))
    lw = -jnp.exp(-sp - 0.5)
    lw_out[...] = lw
    rows = min(lw.shape[0], MXU_DEPTH)
    ri = lax.broadcasted_iota(jnp.int32, (rows, rows), 0)
    ci = lax.broadcasted_iota(jnp.int32, (rows, rows), 1)
    same_chunk = jnp.right_shift(ri, CHUNK_SHIFT) == jnp.right_shift(ci, CHUNK_SHIFT)
    tril = jnp.where(ci <= ri, jnp.where(same_chunk, 1.0, 0.0), 0.0).astype(BF16)
    for r0 in range(0, lw.shape[0], rows):
        hi, mid, lo = _split3(lw[r0:r0 + rows])
        cum_out[r0:r0 + rows, :] = (jnp.dot(tril, hi, preferred_element_type=jnp.float32)
                                    + jnp.dot(tril, mid, preferred_element_type=jnp.float32)
                                    + jnp.dot(tril, lo, preferred_element_type=jnp.float32))
    ua = a0_ref[...] + _bdot(_bdot(mix(5), a1_ref[...]), a2_ref[...])
    a = 1.0 / (1.0 + jnp.exp(-ua))
    a_out[...] = a.astype(a_out.dtype)
    kk_out[...] = (k * kk_ref[...]).astype(kk_out.dtype)
    k_out[...] = (k * (1.0 + (a - 1.0) * ka_ref[...])).astype(k_out.dtype)


def _rwkv_in(x2d, seq, g, mu, w_in, w0, w1, w2, a0, a1, a2, k_k, k_a, tm=512):
    bt, d = x2d.shape
    tiles_per_seq = seq // tm
    row = lambda i: (i, 0)
    prev = lambda i: (jnp.maximum(i * (tm // 8) - 1, 0), 0)
    f32_sds = jax.ShapeDtypeStruct((bt, d), jnp.float32)
    b16_sds = jax.ShapeDtypeStruct((bt, d), BF16)
    return pl.pallas_call(
        functools.partial(_rwkv_in_kernel, tiles_per_seq=tiles_per_seq),
        grid=(bt // tm,),
        in_specs=[pl.BlockSpec((tm, d), row), pl.BlockSpec((8, d), prev),
                  _const_spec((1, d)), _const_spec((6, d)), _const_spec((d, 4 * d)),
                  _const_spec((1, d)), _const_spec((d, DECAY_LORA)), _const_spec((DECAY_LORA, d)),
                  _const_spec((1, d)), _const_spec((d, ICLR_LORA)), _const_spec((ICLR_LORA, d)),
                  _const_spec((1, d)), _const_spec((1, d))],
        out_specs=[pl.BlockSpec((tm, d), row)] * 8,
        out_shape=[b16_sds, f32_sds, f32_sds] + [b16_sds] * 5,
        compiler_params=pltpu.CompilerParams(dimension_semantics=("parallel",),
                                             vmem_limit_bytes=VMEM_LIMIT),
    )(x2d, x2d, g.reshape(1, d), mu, w_in.astype(BF16), w0.reshape(1, d), w1.astype(BF16),
      w2.astype(BF16), a0.reshape(1, d), a1.astype(BF16), a2.astype(BF16), k_k.reshape(1, d),
      k_a.reshape(1, d))


def _split3(x):
    hi = x.astype(BF16)
    r1 = x - hi.astype(x.dtype)
    mid = r1.astype(BF16)
    lo = (r1 - mid.astype(x.dtype)).astype(BF16)
    return hi, mid, lo


def _wkv_kernel(r_ref, lw_ref, cum_ref, k_ref, v_ref, kk_ref, a_ref, rk_ref, lng_ref, lnb_ref,
                o_ref, state_ref):
    @pl.when(pl.program_id(1) == 0)
    def _():
        state_ref[...] = jnp.zeros_like(state_ref)

    c = CHUNK
    nh = WKV_HEADS
    uw = nh * HEAD_DIM
    pairs = range(N_HEADS // nh)
    sls = [slice(p * uw, (p + 1) * uw) for p in pairs]
    m0 = lax.broadcasted_iota(jnp.int32, (c, LANES), 1) < HEAD_DIM
    lane_head = jnp.right_shift(lax.broadcasted_iota(jnp.int32, (c, uw), 1), HEAD_SHIFT)
    head_masks = [lane_head == h for h in range(nh)]

    def head_sum(x):
        out = []
        for j in range(uw // LANES):
            xj = x[:, j * LANES:(j + 1) * LANES]
            s0 = jnp.sum(jnp.where(m0, xj, 0.0), axis=-1, keepdims=True)
            s1 = jnp.sum(jnp.where(m0, 0.0, xj), axis=-1, keepdims=True)
            out.append(jnp.where(m0, s0, s1))
        return jnp.concatenate(out, axis=1)

    def stack(y):
        return jnp.concatenate([jnp.where(m, y, 0.0) for m in head_masks], axis=0)

    def each(f, *lists):
        return [f(*xs) for xs in zip(*lists)]

    n_chunks = r_ref.shape[0] // c
    n_units = len(sls)
    items = [(slice(ci * c, (ci + 1) * c), s) for ci in range(n_chunks) for s in sls]
    f32 = cum_ref.dtype
    r = [r_ref[rows, s].astype(f32) for rows, s in items]
    k = [k_ref[rows, s].astype(f32) for rows, s in items]
    v = [v_ref[rows, s].astype(f32) for rows, s in items]
    cum = [cum_ref[rows, s] for rows, s in items]

    def prep(i):
        rows, s = items[i]
        kkraw = kk_ref[rows, s].astype(f32)
        kk = kkraw / jnp.maximum(jnp.sqrt(head_sum(kkraw * kkraw)), 1e-12)
        b = kk * a_ref[rows, s].astype(f32)
        g_inv = jnp.exp(-cum[i])
        at = -kk * jnp.exp(cum[i] - lw_ref[rows, s])
        rt = r[i] * jnp.exp(cum[i])
        ar = jnp.concatenate([at, rt], axis=0)
        kb_t = jnp.concatenate([stack(k[i] * g_inv), stack(b * g_inv)], axis=0)
        return b, ar, kb_t

    b, ar, kb_t = zip(*[prep(i) for i in range(len(items))])
    big = each(_bdot_nt, ar, kb_t)

    w = nh * c
    row2 = lax.broadcasted_iota(jnp.int32, (c, w), 0)
    col2 = jnp.bitwise_and(lax.broadcasted_iota(jnp.int32, (c, w), 1), c - 1)
    strict = col2 < row2
    incl = col2 <= row2
    eye2 = jnp.where(col2 == row2, 1.0, 0.0)

    x = [jnp.where(strict, g[:c, w:], 0.0) for g in big]
    t = [eye2 + xi for xi in x]
    x = each(lambda xi: _bdot(xi, stack(xi)), x)
    for _ in range(4):
        tx = each(lambda ti, xi: _bdot(jnp.concatenate([ti, xi], axis=0), stack(xi)), t, x)
        t = each(lambda ti, r_: ti + r_[:c], t, tx)
        x = [r_[c:] for r_ in tx]
    t = each(lambda ti, xi: ti + _bdot(ti, stack(xi)), t, x)

    av = each(lambda g, vi: _bdot(
        jnp.concatenate([jnp.where(strict, g[:c, :w], 0.0),
                         jnp.where(incl, g[c:, :w], 0.0)], axis=0), stack(vi)), big, v)
    diag = (jnp.right_shift(lax.broadcasted_iota(jnp.int32, (uw, uw), 0), HEAD_SHIFT)
            == jnp.right_shift(lax.broadcasted_iota(jnp.int32, (uw, uw), 1), HEAD_SHIFT))
    state = [state_ref[p] for p in pairs]
    for ci in range(n_chunks):
        ix = range(ci * n_units, (ci + 1) * n_units)
        ar_s = [_bdot_nt(ar[i], st) for i, st in zip(ix, state)]
        u = [_bdot(t[i], stack(s[:c] + av[i][:c])) for i, s in zip(ix, ar_s)]
        o = [s[c:] + av[i][c:] + _bdot(jnp.where(incl, big[i][c:, w:], 0.0), stack(ui))
             for i, s, ui in zip(ix, ar_s, u)]
        new_state = []
        for i, st, ui in zip(ix, state, u):
            cum_last = cum[i][c - 1:c, :]
            g_rem = jnp.exp(cum_last - cum[i])
            vu = jnp.concatenate([v[i], ui], axis=0)
            kb = jnp.concatenate([k[i] * g_rem, b[i] * g_rem], axis=0)
            upd = _bdot(vu.T, kb)
            new_state.append(st * jnp.exp(cum_last) + jnp.where(diag, upd, 0.0))
        state = new_state

        for i, oi in zip(ix, o):
            rows, s = items[i]
            mean = head_sum(oi) * (1.0 / HEAD_DIM)
            dev = oi - mean
            var = head_sum(dev * dev) * (1.0 / HEAD_DIM)
            y = dev * lax.rsqrt(var + LN_X_EPS) * lng_ref[:, s] + lnb_ref[:, s]
            o_ref[rows, s] = (y + head_sum(r[i] * k[i] * rk_ref[:, s]) * v[i]).astype(o_ref.dtype)
    for p in pairs:
        state_ref[p] = state[p]


def _wkv(r, lw, cum, k, v, kkraw, a, seq, r_k, lnx_g, lnx_b, chunks_per_step=4):
    bt, d = r.shape
    nb = bt // seq
    rows = chunks_per_step * CHUNK
    nc = seq // rows
    row = lambda b, c: (b * nc + c, 0)
    blk = pl.BlockSpec((rows, d), row)
    return pl.pallas_call(
        _wkv_kernel,
        grid=(nb, nc),
        in_specs=[blk] * 7 + [_const_spec((1, d))] * 3,
        out_specs=blk,
        out_shape=jax.ShapeDtypeStruct((bt, d), BF16),
        scratch_shapes=[pltpu.VMEM((N_HEADS // WKV_HEADS, WKV_HEADS * HEAD_DIM,
                                    WKV_HEADS * HEAD_DIM), jnp.float32)],
        compiler_params=pltpu.CompilerParams(dimension_semantics=("parallel", "arbitrary"),
                                             vmem_limit_bytes=VMEM_LIMIT),
    )(r, lw, cum, k, v, kkraw, a, r_k.reshape(1, d), lnx_g.reshape(1, d), lnx_b.reshape(1, d))


def _out_proj_kernel(x_ref, o_ref, z_ref, w_ref, y_ref):
    x = x_ref[...]
    z = z_ref[...].astype(x.dtype)
    y = o_ref[...].astype(x.dtype) * (z / (1.0 + jnp.exp(-z)))
    y_ref[...] = x + _bdot(y, w_ref[...])


def _out_proj(x2d, o, z, w_out, tm=512):
    bt, d = x2d.shape
    row = lambda i: (i, 0)
    blk = pl.BlockSpec((tm, d), row)
    return pl.pallas_call(
        _out_proj_kernel,
        grid=(bt // tm,),
        in_specs=[blk, blk, blk, _const_spec((d, d))],
        out_specs=blk,
        out_shape=jax.ShapeDtypeStruct((bt, d), jnp.float32),
        compiler_params=pltpu.CompilerParams(dimension_semantics=("parallel",),
                                             vmem_limit_bytes=VMEM_LIMIT),
    )(x2d, o, z, w_out.astype(BF16))


def _rwkv_layer(x2d, seq, g, mu, w_in, w0, w1, w2, a0, a1, a2, k_k, k_a, r_k, lnx_g, lnx_b, w_out):
    r, lw, cum, k, v, kkraw, a, z = _rwkv_in(x2d, seq, g, mu, w_in, w0, w1, w2, a0, a1, a2, k_k,
                                             k_a)
    o = _wkv(r, lw, cum, k, v, kkraw, a, seq, r_k, lnx_g, lnx_b)
    return _out_proj(x2d, o, z, w_out)


def _nsa_in_kernel(x_ref, g_ref, w_ref, qg_ref, kg_ref, q_out, kc_out, vc_out, ks_out, vs_out,
                   kw_out, vw_out, z_out, gate_out, *, tiles_per_seq):
    d = D_MODEL
    h = (_rms(x_ref[...]) * g_ref[...]).astype(BF16)
    tm = h.shape[0]

    def proj(off, width):
        return jnp.dot(h, w_ref[:, off:off + width], preferred_element_type=jnp.float32)

    lane = lax.broadcasted_iota(jnp.int32, (tm, LANES), 1)
    low_lanes = lane < HEAD_DIM
    tok = (pl.program_id(0) % tiles_per_seq) * tm + lax.broadcasted_iota(jnp.int32, (tm, LANES), 0)
    blk_bias = jnp.where(lane - HEAD_DIM == jnp.right_shift(tok, SLC_SHIFT), 1.0, 0.0)

    def slot(p, g):
        two = p[:, (g // PAIR) * LANES:(g // PAIR + 1) * LANES]
        return pltpu.roll(two, HEAD_DIM, axis=1) if g % PAIR else two

    def narrow(ref, p, gain_row):
        for g in range(N_KV_GROUPS):
            seg = p[:, g * HEAD_DIM:(g + 1) * HEAD_DIM]
            if gain_row is not None:
                seg = _rms(seg) * kg_ref[gain_row:gain_row + 1, :HEAD_DIM]
            ref[0, g] = seg.astype(ref.dtype)

    def values(ref, p):
        for g in range(N_KV_GROUPS):
            ref[0, g] = jnp.where(low_lanes, slot(p, g), 1.0).astype(ref.dtype)

    p = proj(0, d)
    for j in range(N_PAIRS):
        x = p[:, j * LANES:(j + 1) * LANES]
        sq = x * x
        ms = jnp.where(low_lanes, jnp.sum(jnp.where(low_lanes, sq, 0.0), axis=-1, keepdims=True),
                       jnp.sum(jnp.where(low_lanes, 0.0, sq), axis=-1, keepdims=True))
        q_out[:, j * LANES:(j + 1) * LANES] = (
            x * lax.rsqrt(ms * (1.0 / HEAD_DIM) + NORM_EPS) * qg_ref[...]).astype(q_out.dtype)
    narrow(kc_out, proj(d, KV_WIDTH), None)
    narrow(vc_out, proj(d + KV_WIDTH, KV_WIDTH), None)
    p = proj(d + 2 * KV_WIDTH, KV_WIDTH)
    for g in range(N_KV_GROUPS):
        x = jnp.where(low_lanes, slot(p, g), 0.0)
        ms = jnp.sum(x * x, axis=-1, keepdims=True) * (1.0 / HEAD_DIM)
        ks_out[0, g] = (x * lax.rsqrt(ms + NORM_EPS) * kg_ref[1:2, :] + blk_bias).astype(BF16)
    values(vs_out, proj(d + 3 * KV_WIDTH, KV_WIDTH))
    narrow(kw_out, proj(d + 4 * KV_WIDTH, KV_WIDTH), 2)
    values(vw_out, proj(d + 5 * KV_WIDTH, KV_WIDTH))
    z_out[...] = proj(d + 6 * KV_WIDTH, d).astype(z_out.dtype)
    gate_out[...] = proj(2 * d + 6 * KV_WIDTH, LANES)


def _nsa_in(x2d, nb, seq, g, w_in, q_gain, k_gain, tm=256):
    bt, d = x2d.shape
    tps = seq // tm
    width = w_in.shape[1]
    padded = d + 6 * KV_WIDTH + d + LANES
    w = jnp.pad(w_in, ((0, 0), (0, padded - width))).astype(BF16)
    row = lambda i: (i, 0)
    grp = lambda i: (i // tps, 0, i % tps, 0)
    kv_blk = pl.BlockSpec((1, N_KV_GROUPS, tm, HEAD_DIM), grp)
    v_blk = pl.BlockSpec((1, N_KV_GROUPS, tm, LANES), grp)
    kv_f32 = jax.ShapeDtypeStruct((nb, N_KV_GROUPS, seq, HEAD_DIM), jnp.float32)
    k_b16 = jax.ShapeDtypeStruct((nb, N_KV_GROUPS, seq, HEAD_DIM), BF16)
    v_b16 = jax.ShapeDtypeStruct((nb, N_KV_GROUPS, seq, LANES), BF16)
    wide = jax.ShapeDtypeStruct((bt, d), BF16)
    assert seq // SLC_BLOCK <= HEAD_DIM
    qg = jnp.tile(q_gain * (LOG2_E * HEAD_DIM ** -0.5), PAIR).reshape(1, LANES)
    return pl.pallas_call(
        functools.partial(_nsa_in_kernel, tiles_per_seq=tps),
        grid=(bt // tm,),
        in_specs=[pl.BlockSpec((tm, d), row), _const_spec((1, d)), _const_spec((d, padded)),
                  _const_spec((1, LANES)), _const_spec((N_BRANCHES, LANES))],
        out_specs=[pl.BlockSpec((tm, d), row), kv_blk, kv_blk, v_blk, v_blk, kv_blk, v_blk,
                   pl.BlockSpec((tm, d), row), pl.BlockSpec((tm, LANES), row)],
        out_shape=[wide, kv_f32, kv_f32, v_b16, v_b16, k_b16, v_b16, wide,
                   jax.ShapeDtypeStruct((bt, LANES), jnp.float32)],
        compiler_params=pltpu.CompilerParams(dimension_semantics=("parallel",),
                                             vmem_limit_bytes=VMEM_LIMIT),
    )(x2d, g.reshape(1, d), w, qg, jnp.pad(k_gain, ((0, 0), (0, LANES - HEAD_DIM))))


def _gelu_tanh(x):
    return 0.5 * x * (1.0 + jnp.tanh(0.7978845608028654 * (x + 0.044715 * (x * x * x))))


def _compress_kernel(uk_ref, uv_ref, pe_ref, w1_ref, b1_ref, w2_ref, kg_ref, kc_out, vc_out):
    half = CMP_STRIDE * HEAD_DIM
    for kv, (u_ref, out) in enumerate(((uk_ref, kc_out), (uv_ref, vc_out))):
        uh = u_ref[0, 0]
        rows = uh.shape[0]
        ha = _bdot(uh + pe_ref[kv, 0:1, :], w1_ref[kv, :half, :])
        hb = _bdot(uh + pe_ref[kv, 1:2, :], w1_ref[kv, half:, :])
        hid = _gelu_tanh(ha + pltpu.roll(hb, rows - 1, axis=0) + b1_ref[kv])
        y = _bdot(hid, w2_ref[kv])
        if kv == 0:
            y = y[:, :HEAD_DIM]
            out[0, 0] = (_rms(y) * kg_ref[0:1, :]).astype(out.dtype)
        else:
            low_lanes = lax.broadcasted_iota(jnp.int32, y.shape, 1) < HEAD_DIM
            out[0, 0] = jnp.where(low_lanes, y, 1.0).astype(out.dtype)


def _compress(kc_raw, vc_raw, pe, w1, b1, w2, k_gain):
    nb, ng, seq, n = kc_raw.shape
    rows = seq // CMP_STRIDE
    half = CMP_STRIDE * n
    uk = kc_raw.reshape(nb, ng, rows, half)
    uv = vc_raw.reshape(nb, ng, rows, half)
    whole = lambda b, g: (b, g, 0, 0)
    u_blk = pl.BlockSpec((1, 1, rows, half), whole)
    w2p = jnp.pad(w2, ((0, 0), (0, 0), (0, LANES - n))).astype(BF16)
    return pl.pallas_call(
        _compress_kernel,
        grid=(nb, ng),
        in_specs=[u_blk, u_blk, _const_spec((2, 2, half)), _const_spec((2, 2 * half, CMP_HIDDEN)),
                  _const_spec((2, 1, CMP_HIDDEN)), _const_spec((2, CMP_HIDDEN, LANES)),
                  _const_spec((N_BRANCHES, n))],
        out_specs=[pl.BlockSpec((1, 1, rows, n), whole), pl.BlockSpec((1, 1, rows, LANES), whole)],
        out_shape=[jax.ShapeDtypeStruct((nb, ng, rows, n), BF16),
                   jax.ShapeDtypeStruct((nb, ng, rows, LANES), BF16)],
        compiler_params=pltpu.CompilerParams(dimension_semantics=("parallel", "parallel"),
                                             vmem_limit_bytes=VMEM_LIMIT),
    )(uk, uv, pe.reshape(2, 2, half), w1.astype(BF16), b1.reshape(2, 1, CMP_HIDDEN), w2p, k_gain)


M_INIT = -1e20


LOG2_E = 1.4426950408889634


MAX_STATIC_BOUND = 50.0


def _softmax_step(s, bias, v_tile, carry, hg, qt, online):
    sb = [s[h * qt:(h + 1) * qt] for h in range(hg)]
    if bias is not None:
        sb = [x + bias for x in sb]
    if not online:
        return jnp.concatenate(
            [carry[h * qt:(h + 1) * qt] + jnp.dot(jnp.exp2(sb[h]).astype(BF16), v_tile,
                                                   preferred_element_type=jnp.float32)
             for h in range(hg)], axis=0)
    m, acc = carry
    m_new = jnp.maximum(m, jnp.concatenate(
        [jnp.max(x, axis=-1, keepdims=True) for x in sb], axis=0))
    p = jnp.concatenate(
        [jnp.exp2(sb[h] - m_new[h * qt:(h + 1) * qt]).astype(BF16) for h in range(hg)], axis=0)
    acc = jnp.exp2(m - m_new) * acc + jnp.dot(p, v_tile, preferred_element_type=jnp.float32)
    return m_new, acc


N_ATTN_INPUTS = 10


def _nsa_attn_kernel(*refs, qt, kt, seq, online, tile):
    t0 = pl.program_id(2) * qt if tile is None else tile * qt
    _nsa_attn_tile(*refs[:N_ATTN_INPUTS], refs[-1], t0=t0, qt=qt, kt=kt, seq=seq, online=online)


def _nsa_attn_tile(bound_ref, q_ref, gate_ref, bcast_ref, kc_ref, vc_ref, ks_ref, vs_ref,
                   kw_ref, vw_ref, o_ref, *, t0, qt, kt, seq, online):
    hg = HEADS_PER_GROUP
    static = isinstance(t0, int)
    n_slc = seq // SLC_BLOCK
    n_cmp_rows = kc_ref.shape[2]
    if static:
        n_cmp_rows = min(n_cmp_rows, -(-((t0 + qt) // CMP_STRIDE) // LANES) * LANES)
    rows = hg * qt

    def aligned(x, m):
        return x if isinstance(x, int) else pl.multiple_of(x, m)
    low_lanes = lax.broadcasted_iota(jnp.int32, (qt, LANES), 1) < HEAD_DIM
    keep_c, keep_s, keep_w = (0.0, 0.0, 0.0) if online else (
        -bound_ref[0], -bound_ref[1], -bound_ref[2])

    def swap_halves(x):
        return pltpu.roll(x, HEAD_DIM, axis=1)

    q = q_ref[...].astype(jnp.float32)
    q_n = []
    for h in range(hg):
        two = q[:, (h // PAIR) * LANES:(h // PAIR + 1) * LANES]
        q_n.append(jnp.where(low_lanes, swap_halves(two) if h % PAIR else two, 0.0))
    qs = jnp.concatenate([x[:, :HEAD_DIM] for x in q_n], axis=0).astype(BF16)
    t_rows = t0 + lax.broadcasted_iota(jnp.int32, (qt, 1), 0)

    w0 = max(t0 - WINDOW, 0) if static else pl.multiple_of(jnp.maximum(t0 - WINDOW, 0), qt)
    span = t0 + qt - w0 if static else WINDOW + qt
    s = _bdot_nt(qs, kc_ref[0, 0, :n_cmp_rows, :])
    s_win = _bdot_nt(qs, kw_ref[0, 0, pl.ds(w0, span), :])

    cmp_end = lax.broadcasted_iota(jnp.int32, (1, n_cmp_rows), 1) * CMP_STRIDE + (CMP_BLOCK - 1)
    cbias = jnp.where(cmp_end <= t_rows, keep_c, NEG_INF)
    p_heads = []
    for h in range(hg):
        sb = s[h * qt:(h + 1) * qt] + cbias
        if online:
            sb = sb - jnp.maximum(jnp.max(sb, axis=-1, keepdims=True), M_INIT)
        e = jnp.exp2(sb)
        p_heads.append(e / jnp.maximum(jnp.sum(e, axis=-1, keepdims=True), 1e-37))
    o_cmp = jnp.dot(jnp.concatenate([p.astype(BF16) for p in p_heads], axis=0),
                    vc_ref[0, 0, :n_cmp_rows, :], preferred_element_type=jnp.float32)

    p_sum = p_heads[0]
    for h in range(1, hg):
        p_sum = p_sum + p_heads[h]
    jn = lax.broadcasted_iota(jnp.int32, (n_slc, n_cmp_rows), 0) * SLC_BLOCK
    nn = lax.broadcasted_iota(jnp.int32, (n_slc, n_cmp_rows), 1) * CMP_STRIDE
    ov_t = jnp.where((nn < jn + SLC_BLOCK) & (nn + (CMP_BLOCK - 1) >= jn), 1.0, 0.0).astype(BF16)
    p_hi = p_sum.astype(BF16)
    p_lo = (p_sum - p_hi.astype(p_sum.dtype)).astype(BF16)
    imp_t = (lax.dot_general(ov_t, p_hi, (((1,), (1,)), ((), ())), preferred_element_type=jnp.float32)
             + lax.dot_general(ov_t, p_lo, (((1,), (1,)), ((), ())), preferred_element_type=jnp.float32))

    lag = t_rows - (w0 + lax.broadcasted_iota(jnp.int32, (1, span), 1))
    wbias = jnp.where(lag >= 0, jnp.where(lag < WINDOW, keep_w, NEG_INF), NEG_INF)
    v_win = vw_ref[0, 0, pl.ds(w0, span), :]
    acc_win = []
    for h in range(hg):
        sb = s_win[h * qt:(h + 1) * qt] + wbias
        if online:
            sb = sb - jnp.max(sb, axis=-1, keepdims=True)
        acc_win.append(jnp.dot(jnp.exp2(sb).astype(BF16), v_win,
                               preferred_element_type=jnp.float32))
    acc_win = jnp.concatenate(acc_win, axis=0)

    jb = lax.broadcasted_iota(jnp.int32, (n_slc, qt), 0)
    tq = t0 + lax.broadcasted_iota(jnp.int32, (n_slc, qt), 1)
    dist = jnp.right_shift(tq, SLC_SHIFT) - jb
    forced = (jb == 0) | ((dist >= 0) & (dist < N_LOCAL_BLOCKS))
    score = jnp.where(dist >= 0, imp_t + jnp.where(forced, FORCE_BONUS, 0.0), -jnp.inf)
    sub = 8
    n_cand = (t0 + qt - 1) // SLC_BLOCK + 1 if static else n_slc
    if n_cand <= SLC_TOPK:
        n_cand = 0
    groups = [score[v * sub:(v + 1) * sub] for v in range(n_slc // sub)]
    jrow = lax.broadcasted_iota(jnp.int32, (sub, qt), 0)
    cnt = [jnp.zeros((sub, qt), jnp.float32) for _ in groups]
    for j2 in range(n_cand):
        row = score[j2:j2 + 1, :]
        for v, sv in enumerate(groups[:-(-n_cand // sub)]):
            if v * sub > j2:
                inc = jnp.where(row >= sv, 1.0, 0.0)
            elif (v + 1) * sub - 1 <= j2:
                inc = jnp.where(row > sv, 1.0, 0.0)
            else:
                inc = jnp.where(jrow + v * sub > j2, jnp.where(row >= sv, 1.0, 0.0),
                                jnp.where(row > sv, 1.0, 0.0))
            cnt[v] = cnt[v] + inc
    cnt = jnp.concatenate(cnt, axis=0)
    blk_bias_t = jnp.where(dist >= 0, jnp.where(cnt < min(SLC_TOPK, n_slc), keep_s, NEG_INF),
                           NEG_INF)
    parts = [jnp.zeros((HEAD_DIM, qt), jnp.float32), blk_bias_t]
    if n_slc < HEAD_DIM:
        parts.append(jnp.zeros((HEAD_DIM - n_slc, qt), jnp.float32))
    blk_bias = jnp.concatenate(parts, axis=0).T
    q_aug = jnp.concatenate([x + blk_bias for x in q_n], axis=0).astype(BF16)

    def slc_tile(k0, width, bias, carry):
        k_tile = ks_ref[0, 0, pl.ds(k0, width), :]
        s = jnp.concatenate([_bdot_nt(q_aug[h * qt:(h + 1) * qt], k_tile) for h in range(hg)],
                            axis=0)
        return _softmax_step(s, bias, vs_ref[0, 0, pl.ds(k0, width), :], carry, hg, qt, online)

    def slc_body(i, carry):
        return slc_tile(aligned(i * kt, kt), kt, None, carry)

    carry = jnp.zeros((rows, LANES), jnp.float32)
    if online:
        carry = (jnp.full((rows, 1), M_INIT, jnp.float32), carry)
    n_full = t0 // kt
    if static:
        for i in range(n_full):
            carry = slc_body(i, carry)
    else:
        carry = lax.fori_loop(0, n_full // 2,
                              lambda j, c: slc_body(2 * j + 1, slc_body(2 * j, c)), carry)
        carry = lax.cond(n_full % 2 == 1, lambda c: slc_body(n_full - 1, c), lambda c: c, carry)
    k0 = aligned(n_full * kt, kt)
    last = t0 + qt - k0 if static else kt
    causal = jnp.where(k0 + lax.broadcasted_iota(jnp.int32, (1, last), 1) <= t_rows, 0.0, NEG_INF)
    acc_slc = slc_tile(k0, last, causal, carry)
    if online:
        acc_slc = acc_slc[1]

    gates = 1.0 / (1.0 + jnp.exp(-gate_ref[...]))
    spread = jnp.dot(gates.astype(BF16), bcast_ref[0],
                     preferred_element_type=jnp.float32)

    def pair(acc, j, normalise):
        a0 = acc[(2 * j) * qt:(2 * j + 1) * qt]
        a1 = acc[(2 * j + 1) * qt:(2 * j + 2) * qt]
        num = jnp.where(low_lanes, a0, swap_halves(a1))
        if not normalise:
            return num
        return num / jnp.where(low_lanes, swap_halves(a0), a1)

    for j in range(hg // PAIR):
        width = PAIR * HEAD_DIM
        o = jnp.zeros((qt, width), jnp.float32)
        for br, (acc, normalise) in enumerate(((o_cmp, False), (acc_slc, True), (acc_win, True))):
            col = (br * (hg // PAIR) + j) * width
            o = o + spread[:, col:col + width] * pair(acc, j, normalise)
        o_ref[:, j * width:(j + 1) * width] = o.astype(o_ref.dtype)


def _nsa_attn(q, gate, q_gain, k_gain, kc, vc, ks, vs, kw, vw, qt=256, kt=512):
    bt, d = q.shape
    nb, ng, seq, n = kw.shape
    kt = min(kt, seq)
    assert seq >= WINDOW + qt and seq % kt == 0 and kt % qt == 0
    nq = seq // qt
    hg = HEADS_PER_GROUP
    width = hg * n
    rows_c = kc.shape[2]
    col = jnp.arange(N_BRANCHES * width) // n
    src = (col // hg) * N_HEADS + jnp.arange(ng)[:, None] * hg + col % hg
    bcast = (jnp.arange(LANES)[None, :, None] == src[:, None, :]).astype(BF16)
    bounds = (1.01 * LOG2_E * n ** 0.5) * jnp.max(jnp.abs(q_gain)) * jnp.max(jnp.abs(k_gain), axis=1)
    out_sds = jax.ShapeDtypeStruct((bt, d), BF16)

    def specs(tile):
        pick = (lambda b, g, i: i) if tile is None else (lambda b, g: tile)
        at = lambda f: (lambda *ids: f(ids[0], ids[1], pick(*ids)))
        keys = seq if tile is None else (tile + 1) * qt
        cmp_rows = rows_c if tile is None else min(rows_c, -(-(keys // CMP_STRIDE) // LANES) * LANES)
        whole = at(lambda b, g, i: (b, g, 0, 0))
        q_blk = pl.BlockSpec((qt, width), at(lambda b, g, i: (b * nq + i, g)))
        in_specs = [pl.BlockSpec(memory_space=pltpu.SMEM), q_blk,
                    pl.BlockSpec((qt, LANES), at(lambda b, g, i: (b * nq + i, 0))),
                    pl.BlockSpec((1, LANES, N_BRANCHES * width), at(lambda b, g, i: (g, 0, 0))),
                    pl.BlockSpec((1, 1, cmp_rows, n), whole),
                    pl.BlockSpec((1, 1, cmp_rows, LANES), whole),
                    pl.BlockSpec((1, 1, keys, LANES), whole), pl.BlockSpec((1, 1, keys, LANES), whole),
                    pl.BlockSpec((1, 1, keys, n), whole), pl.BlockSpec((1, 1, keys, LANES), whole)]
        return in_specs, q_blk

    def generic(*args):
        in_specs, q_blk = specs(None)
        return pl.pallas_call(
            functools.partial(_nsa_attn_kernel, qt=qt, kt=kt, seq=seq, online=True, tile=None),
            grid=(nb, ng, nq), in_specs=in_specs, out_specs=q_blk, out_shape=out_sds,
            compiler_params=pltpu.CompilerParams(
                dimension_semantics=("parallel", "parallel", "arbitrary"),
                vmem_limit_bytes=VMEM_LIMIT),
        )(*args)

    def per_tile(*args):
        o = None
        for tile in range(nq):
            in_specs, q_blk = specs(tile)
            operands = list(args)
            if o is not None:
                in_specs.append(pl.BlockSpec(memory_space=pl.ANY))
                operands.append(o)
            o = pl.pallas_call(
                functools.partial(_nsa_attn_kernel, qt=qt, kt=kt, seq=seq, online=False, tile=tile),
                grid=(nb, ng), in_specs=in_specs, out_specs=q_blk, out_shape=out_sds,
                input_output_aliases={} if tile == 0 else {N_ATTN_INPUTS: 0},
                compiler_params=pltpu.CompilerParams(
                    dimension_semantics=("parallel", "parallel"), vmem_limit_bytes=VMEM_LIMIT),
            )(*operands)
        return o

    return lax.cond(jnp.max(bounds) <= MAX_STATIC_BOUND, per_tile, generic,
                    bounds.astype(jnp.float32), q, gate, bcast, kc, vc, ks, vs, kw, vw)


def _nsa_layer(x2d, nb, seq, g, w_in, q_gain, k_gain, cmp_pe, cmp_w1, cmp_b1, cmp_w2, w_out):
    q, kc_raw, vc_raw, ks, vs, kw, vw, z, gate = _nsa_in(x2d, nb, seq, g, w_in, q_gain, k_gain)
    kc, vc = _compress(kc_raw, vc_raw, cmp_pe, cmp_w1, cmp_b1, cmp_w2, k_gain)
    o = _nsa_attn(q, gate, q_gain, k_gain, kc, vc, ks, vs, kw, vw)
    return _out_proj(x2d, o, z, w_out)


def kernel(x, norm_g, rwkv_mu, rwkv_w_in, rwkv_w0, rwkv_w1, rwkv_w2, rwkv_a0, rwkv_a1, rwkv_a2, rwkv_k_k, rwkv_k_a, rwkv_r_k, rwkv_lnx_g, rwkv_lnx_b, rwkv_w_out, nsa_w_in, nsa_q_gain, nsa_k_gain, nsa_cmp_pe, nsa_cmp_w1, nsa_cmp_b1, nsa_cmp_w2, nsa_w_out):
    b, t, d = x.shape
    x2d = x.reshape(b * t, d)
    x2d = _rwkv_layer(x2d, t, norm_g[0], rwkv_mu[0], rwkv_w_in[0], rwkv_w0[0], rwkv_w1[0],
                      rwkv_w2[0], rwkv_a0[0], rwkv_a1[0], rwkv_a2[0], rwkv_k_k[0], rwkv_k_a[0],
                      rwkv_r_k[0].reshape(-1), rwkv_lnx_g[0], rwkv_lnx_b[0], rwkv_w_out[0])
    x2d = _nsa_layer(x2d, b, t, norm_g[1], nsa_w_in[0], nsa_q_gain[0], nsa_k_gain[0],
                     nsa_cmp_pe[0], nsa_cmp_w1[0], nsa_cmp_b1[0], nsa_cmp_w2[0], nsa_w_out[0])
    return x2d.reshape(b, t, d)
```

```python
import functools

import jax
import jax.numpy as jnp
from jax import lax
from jax.experimental import pallas as pl
from jax.experimental.pallas import tpu as pltpu

D_MODEL = 1024
HEAD_DIM = 64
N_HEADS = D_MODEL // HEAD_DIM
NORM_EPS = 1e-6
LN_X_EPS = 64e-5
DECAY_LORA = 64
ICLR_LORA = 64
N_KV_GROUPS = 4
HEADS_PER_GROUP = N_HEADS // N_KV_GROUPS
KV_WIDTH = N_KV_GROUPS * HEAD_DIM
N_BRANCHES = 3
CMP_BLOCK = 32
CMP_STRIDE = 16
CMP_HIDDEN = 256
SLC_BLOCK = 64
SLC_SHIFT = 6
SLC_TOPK = 16
N_LOCAL_BLOCKS = 2
WINDOW = 512
FORCE_BONUS = 1e4
NEG_INF = -1e30

LANES = 128
MXU_DEPTH = 256
PAIR = LANES // HEAD_DIM
N_PAIRS = N_HEADS // PAIR
HEAD_SHIFT = 6
WKV_HEADS = LANES // HEAD_DIM
CHUNK = 64
CHUNK_SHIFT = 6
VMEM_LIMIT = 48 * 1024 * 1024

BF16 = jnp.bfloat16


def _rms(x, eps=NORM_EPS):
    return x * lax.rsqrt(jnp.mean(x * x, axis=-1, keepdims=True) + eps)


def _bdot(a, b):
    return jnp.dot(a.astype(BF16), b.astype(BF16), preferred_element_type=jnp.float32)


def _bdot_nt(a, b):
    return lax.dot_general(a.astype(BF16), b.astype(BF16), (((1,), (1,)), ((), ())),
                           preferred_element_type=jnp.float32)


def _const_spec(shape):
    nd = len(shape)
    return pl.BlockSpec(shape, lambda *_: (0,) * nd, pipeline_mode=pl.Buffered(1))


def _rwkv_in_kernel(x_ref, xp_ref, g_ref, mu_ref, win_ref, w0_ref, w1_ref, w2_ref, a0_ref,
                    a1_ref, a2_ref, kk_ref, ka_ref,
                    r_out, lw_out, cum_out, k_out, v_out, kk_out, a_out, z_out, *, tiles_per_seq):
    d = D_MODEL
    g = g_ref[...]
    h = _rms(x_ref[...]) * g
    hp = _rms(xp_ref[7:8, :]) * g
    hp = jnp.where(pl.program_id(0) % tiles_per_seq == 0, 0.0, hp)
    row = lax.broadcasted_iota(jnp.int32, h.shape, 0)
    hs = jnp.where(row == 0, hp, pltpu.roll(h, 1, axis=0))
    dh = hs - h

    def mix(c):
        return h + dh * mu_ref[c:c + 1, :]

    r_out[...] = _bdot(mix(0), win_ref[:, 0 * d:1 * d]).astype(r_out.dtype)
    k = _bdot(mix(1), win_ref[:, 1 * d:2 * d])
    v_out[...] = _bdot(mix(2), win_ref[:, 2 * d:3 * d]).astype(v_out.dtype)
    z_out[...] = _bdot(mix(3), win_ref[:, 3 * d:4 * d]).astype(z_out.dtype)

    u = w0_ref[...] + _bdot(jnp.tanh(_bdot(mix(4), w1_ref[...])), w2_ref[...])
    sp = jnp.maximum(-u, 0.0) + jnp.log(1.0 + jnp.exp(-jnp.abs(u)))
    lw = -jnp.exp(-sp - 0.5)
    lw_out[...] = lw
    rows = min(lw.shape[0], MXU_DEPTH)
    ri = lax.broadcasted_iota(jnp.int32, (rows, rows), 0)
    ci = lax.broadcasted_iota(jnp.int32, (rows, rows), 1)
    same_chunk = jnp.right_shift(ri, CHUNK_SHIFT) == jnp.right_shift(ci, CHUNK_SHIFT)
    tril = jnp.where(ci <= ri, jnp.where(same_chunk, 1.0, 0.0), 0.0).astype(BF16)
    for r0 in range(0, lw.shape[0], rows):
        hi, mid, lo = _split3(lw[r0:r0 + rows])
        cum_out[r0:r0 + rows, :] = (jnp.dot(tril, hi, preferred_element_type=jnp.float32)
                                    + jnp.dot(tril, mid, preferred_element_type=jnp.float32)
                                    + jnp.dot(tril, lo, preferred_element_type=jnp.float32))
    ua = a0_ref[...] + _bdot(_bdot(mix(5), a1_ref[...]), a2_ref[...])
    a = 1.0 / (1.0 + jnp.exp(-ua))
    a_out[...] = a.astype(a_out.dtype)
    kk_out[...] = (k * kk_ref[...]).astype(kk_out.dtype)
    k_out[...] = (k * (1.0 + (a - 1.0) * ka_ref[...])).astype(k_out.dtype)


def _rwkv_in(x2d, seq, g, mu, w_in, w0, w1, w2, a0, a1, a2, k_k, k_a, tm=512):
    bt, d = x2d.shape
    tiles_per_seq = seq // tm
    row = lambda i: (i, 0)
    prev = lambda i: (jnp.maximum(i * (tm // 8) - 1, 0), 0)
    f32_sds = jax.ShapeDtypeStruct((bt, d), jnp.float32)
    b16_sds = jax.ShapeDtypeStruct((bt, d), BF16)
    return pl.pallas_call(
        functools.partial(_rwkv_in_kernel, tiles_per_seq=tiles_per_seq),
        grid=(bt // tm,),
        in_specs=[pl.BlockSpec((tm, d), row), pl.BlockSpec((8, d), prev),
                  _const_spec((1, d)), _const_spec((6, d)), _const_spec((d, 4 * d)),
                  _const_spec((1, d)), _const_spec((d, DECAY_LORA)), _const_spec((DECAY_LORA, d)),
                  _const_spec((1, d)), _const_spec((d, ICLR_LORA)), _const_spec((ICLR_LORA, d)),
                  _const_spec((1, d)), _const_spec((1, d))],
        out_specs=[pl.BlockSpec((tm, d), row)] * 8,
        out_shape=[b16_sds, f32_sds, f32_sds] + [b16_sds] * 5,
        compiler_params=pltpu.CompilerParams(dimension_semantics=("parallel",),
                                             vmem_limit_bytes=VMEM_LIMIT),
    )(x2d, x2d, g.reshape(1, d), mu, w_in.astype(BF16), w0.reshape(1, d), w1.astype(BF16),
      w2.astype(BF16), a0.reshape(1, d), a1.astype(BF16), a2.astype(BF16), k_k.reshape(1, d),
      k_a.reshape(1, d))


def _split3(x):
    hi = x.astype(BF16)
    r1 = x - hi.astype(x.dtype)
    mid = r1.astype(BF16)
    lo = (r1 - mid.astype(x.dtype)).astype(BF16)
    return hi, mid, lo


def _wkv_kernel(r_ref, lw_ref, cum_ref, k_ref, v_ref, kk_ref, a_ref, rk_ref, lng_ref, lnb_ref,
                o_ref, state_ref):
    @pl.when(pl.program_id(1) == 0)
    def _():
        state_ref[...] = jnp.zeros_like(state_ref)

    c = CHUNK
    nh = WKV_HEADS
    uw = nh * HEAD_DIM
    pairs = range(N_HEADS // nh)
    sls = [slice(p * uw, (p + 1) * uw) for p in pairs]
    m0 = lax.broadcasted_iota(jnp.int32, (c, LANES), 1) < HEAD_DIM
    lane_head = jnp.right_shift(lax.broadcasted_iota(jnp.int32, (c, uw), 1), HEAD_SHIFT)
    head_masks = [lane_head == h for h in range(nh)]

    def head_sum(x):
        out = []
        for j in range(uw // LANES):
            xj = x[:, j * LANES:(j + 1) * LANES]
            s0 = jnp.sum(jnp.where(m0, xj, 0.0), axis=-1, keepdims=True)
            s1 = jnp.sum(jnp.where(m0, 0.0, xj), axis=-1, keepdims=True)
            out.append(jnp.where(m0, s0, s1))
        return jnp.concatenate(out, axis=1)

    def stack(y):
        return jnp.concatenate([jnp.where(m, y, 0.0) for m in head_masks], axis=0)

    def each(f, *lists):
        return [f(*xs) for xs in zip(*lists)]

    n_rows, n_chunks = r_ref.shape[0], r_ref.shape[1] // c
    f32 = cum_ref.dtype
    w = nh * c
    row2 = lax.broadcasted_iota(jnp.int32, (c, w), 0)
    col2 = jnp.bitwise_and(lax.broadcasted_iota(jnp.int32, (c, w), 1), c - 1)
    strict = col2 < row2
    incl = col2 <= row2
    eye2 = jnp.where(col2 == row2, 1.0, 0.0)
    diag = (jnp.right_shift(lax.broadcasted_iota(jnp.int32, (uw, uw), 0), HEAD_SHIFT)
            == jnp.right_shift(lax.broadcasted_iota(jnp.int32, (uw, uw), 1), HEAD_SHIFT))

    def prep_stages(ci, d):
        items = [(bi, slice(ci * c, (ci + 1) * c), s) for bi in range(n_rows) for s in sls]
        d.update(items=items, r=[], k=[], v=[], cum=[], b=[], ar=[], kb_t=[])
        for i, it in enumerate(items):
            r, k, cum = r_ref[it].astype(f32), k_ref[it].astype(f32), cum_ref[it]
            kkraw = kk_ref[it].astype(f32)
            kk = kkraw / jnp.maximum(jnp.sqrt(head_sum(kkraw * kkraw)), 1e-12)
            b = kk * a_ref[it].astype(f32)
            g_inv = jnp.exp(-cum)
            at = -kk * jnp.exp(cum - lw_ref[it])
            d['ar'].append(jnp.concatenate([at, r * jnp.exp(cum)], axis=0))
            d['kb_t'].append(jnp.concatenate([stack(k * g_inv), stack(b * g_inv)], axis=0))
            for name, val in (('r', r), ('k', k), ('v', v_ref[it].astype(f32)), ('cum', cum),
                              ('b', b)):
                d[name].append(val)
            if i % 2:
                yield

    def head_stages(d):
        v = d['v']
        big = each(_bdot_nt, d['ar'], d['kb_t'])
        yield
        x = [jnp.where(strict, g[:c, w:], 0.0) for g in big]
        t = [eye2 + xi for xi in x]
        x = each(lambda xi: _bdot(xi, stack(xi)), x)
        yield
        for _ in range(4):
            tx = each(lambda ti, xi: _bdot(jnp.concatenate([ti, xi], axis=0), stack(xi)), t, x)
            t = each(lambda ti, r_: ti + r_[:c], t, tx)
            x = [r_[c:] for r_ in tx]
            yield
        t = each(lambda ti, xi: ti + _bdot(ti, stack(xi)), t, x)
        yield
        av = each(lambda g, vi: _bdot(
            jnp.concatenate([jnp.where(strict, g[:c, :w], 0.0),
                             jnp.where(incl, g[c:, :w], 0.0)], axis=0), stack(vi)), big, v)
        d.update(big=big, t=t, av=av)
        yield

    def tail_stages(d, box):
        state = box[0]
        ar_s = each(_bdot_nt, d['ar'], state)
        yield
        u = each(lambda ti, s, avi: _bdot(ti, stack(s[:c] + avi[:c])), d['t'], ar_s, d['av'])
        yield
        o = each(lambda g, s, avi, ui: s[c:] + avi[c:] + _bdot(
            jnp.where(incl, g[c:, w:], 0.0), stack(ui)), d['big'], ar_s, d['av'], u)
        new_state = []
        for st, ui, vi, ki, bi_, cumi in zip(state, u, d['v'], d['k'], d['b'], d['cum']):
            cum_last = cumi[c - 1:c, :]
            g_rem = jnp.exp(cum_last - cumi)
            vu = jnp.concatenate([vi, ui], axis=0)
            kb = jnp.concatenate([ki * g_rem, bi_ * g_rem], axis=0)
            upd = _bdot(vu.T, kb)
            new_state.append(st * jnp.exp(cum_last) + jnp.where(diag, upd, 0.0))
        box[0] = new_state
        yield
        for it, oi, ri, ki, vi in zip(d['items'], o, d['r'], d['k'], d['v']):
            s = it[2]
            mean = head_sum(oi) * (1.0 / HEAD_DIM)
            dev = oi - mean
            var = head_sum(dev * dev) * (1.0 / HEAD_DIM)
            y = dev * lax.rsqrt(var + LN_X_EPS) * lng_ref[:, s] + lnb_ref[:, s]
            o_ref[it] = (y + head_sum(ri * ki * rk_ref[:, s]) * vi).astype(o_ref.dtype)
        yield

    n_state = n_rows * len(sls)
    box = [[state_ref[j] for j in range(n_state)]]
    chunks = [dict() for _ in range(n_chunks)]
    for step in range(n_chunks + 2):
        live = {}
        if 0 < step <= n_chunks:
            live['head'] = head_stages(chunks[step - 1])
        if step < n_chunks:
            live['prep'] = prep_stages(step, chunks[step])
        if step > 1:
            live['tail'] = tail_stages(chunks[step - 2], box)
        n = 0
        while live:
            n += 1
            for name in ('head', 'prep', 'tail'):
                if name not in live or (name == 'tail' and n % 2 and len(live) > 1):
                    continue
                if next(live[name], StopIteration) is StopIteration:
                    del live[name]
    for j in range(n_state):
        state_ref[j] = box[0][j]


def _wkv(r, lw, cum, k, v, kkraw, a, seq, r_k, lnx_g, lnx_b, chunks_per_step=4, rows_per_step=2):
    bt, d = r.shape
    nb = bt // seq
    rows = chunks_per_step * CHUNK
    blk = pl.BlockSpec((rows_per_step, rows, d), lambda b, c: (b, c, 0))
    uw = WKV_HEADS * HEAD_DIM
    streams = [x.reshape(nb, seq, d) for x in (r, lw, cum, k, v, kkraw, a)]
    return pl.pallas_call(
        _wkv_kernel,
        grid=(nb // rows_per_step, seq // rows),
        in_specs=[blk] * 7 + [_const_spec((1, d))] * 3,
        out_specs=blk,
        out_shape=jax.ShapeDtypeStruct((nb, seq, d), BF16),
        scratch_shapes=[pltpu.VMEM((rows_per_step * (N_HEADS // WKV_HEADS), uw, uw), jnp.float32)],
        compiler_params=pltpu.CompilerParams(dimension_semantics=("parallel", "arbitrary"),
                                             vmem_limit_bytes=VMEM_LIMIT),
    )(*streams, r_k.reshape(1, d), lnx_g.reshape(1, d), lnx_b.reshape(1, d)).reshape(bt, d)


def _out_proj_kernel(x_ref, o_ref, z_ref, w_ref, y_ref):
    x = x_ref[...]
    z = z_ref[...].astype(x.dtype)
    y = o_ref[...].astype(x.dtype) * (z / (1.0 + jnp.exp(-z)))
    y_ref[...] = x + _bdot(y, w_ref[...])


def _out_proj(x2d, o, z, w_out, tm=1024):
    bt, d = x2d.shape
    row = lambda i: (i, 0)
    blk = pl.BlockSpec((tm, d), row)
    return pl.pallas_call(
        _out_proj_kernel,
        grid=(bt // tm,),
        in_specs=[blk, blk, blk, _const_spec((d, d))],
        out_specs=blk,
        out_shape=jax.ShapeDtypeStruct((bt, d), jnp.float32),
        compiler_params=pltpu.CompilerParams(dimension_semantics=("parallel",),
                                             vmem_limit_bytes=VMEM_LIMIT),
    )(x2d, o, z, w_out.astype(BF16))


def _rwkv_layer(x2d, seq, g, mu, w_in, w0, w1, w2, a0, a1, a2, k_k, k_a, r_k, lnx_g, lnx_b, w_out):
    r, lw, cum, k, v, kkraw, a, z = _rwkv_in(x2d, seq, g, mu, w_in, w0, w1, w2, a0, a1, a2, k_k,
                                             k_a)
    o = _wkv(r, lw, cum, k, v, kkraw, a, seq, r_k, lnx_g, lnx_b)
    return _out_proj(x2d, o, z, w_out)


def _nsa_in_kernel(x_ref, g_ref, w_ref, qg_ref, kg_ref, q_out, kc_out, vc_out, ks_out, vs_out,
                   kw_out, vw_out, z_out, gate_out, *, tiles_per_seq):
    d = D_MODEL
    h = (_rms(x_ref[...]) * g_ref[...]).astype(BF16)
    tm = h.shape[0]

    def proj(off, width):
        return jnp.dot(h, w_ref[:, off:off + width], preferred_element_type=jnp.float32)

    lane = lax.broadcasted_iota(jnp.int32, (tm, LANES), 1)
    low_lanes = lane < HEAD_DIM
    tok = (pl.program_id(0) % tiles_per_seq) * tm + lax.broadcasted_iota(jnp.int32, (tm, LANES), 0)
    blk_bias = jnp.where(lane - HEAD_DIM == jnp.right_shift(tok, SLC_SHIFT), 1.0, 0.0)

    def slot(p, g):
        two = p[:, (g // PAIR) * LANES:(g // PAIR + 1) * LANES]
        return pltpu.roll(two, HEAD_DIM, axis=1) if g % PAIR else two

    def narrow(ref, p, gain_row):
        for g in range(N_KV_GROUPS):
            seg = p[:, g * HEAD_DIM:(g + 1) * HEAD_DIM]
            if gain_row is not None:
                seg = _rms(seg) * kg_ref[gain_row:gain_row + 1, :HEAD_DIM]
            ref[0, g] = seg.astype(ref.dtype)

    def values(ref, p):
        for g in range(N_KV_GROUPS):
            ref[0, g] = jnp.where(low_lanes, slot(p, g), 1.0).astype(ref.dtype)

    p = proj(0, d)
    for j in range(N_PAIRS):
        x = p[:, j * LANES:(j + 1) * LANES]
        sq = x * x
        ms = jnp.where(low_lanes, jnp.sum(jnp.where(low_lanes, sq, 0.0), axis=-1, keepdims=True),
                       jnp.sum(jnp.where(low_lanes, 0.0, sq), axis=-1, keepdims=True))
        q_out[:, j * LANES:(j + 1) * LANES] = (
            x * lax.rsqrt(ms * (1.0 / HEAD_DIM) + NORM_EPS) * qg_ref[...]).astype(q_out.dtype)
    narrow(kc_out, proj(d, KV_WIDTH), None)
    narrow(vc_out, proj(d + KV_WIDTH, KV_WIDTH), None)
    p = proj(d + 2 * KV_WIDTH, KV_WIDTH)
    for g in range(N_KV_GROUPS):
        x = jnp.where(low_lanes, slot(p, g), 0.0)
        ms = jnp.sum(x * x, axis=-1, keepdims=True) * (1.0 / HEAD_DIM)
        ks_out[0, g] = (x * lax.rsqrt(ms + NORM_EPS) * kg_ref[1:2, :] + blk_bias).astype(BF16)
    values(vs_out, proj(d + 3 * KV_WIDTH, KV_WIDTH))
    narrow(kw_out, proj(d + 4 * KV_WIDTH, KV_WIDTH), 2)
    values(vw_out, proj(d + 5 * KV_WIDTH, KV_WIDTH))
    z_out[...] = proj(d + 6 * KV_WIDTH, d).astype(z_out.dtype)
    gate_out[...] = proj(2 * d + 6 * KV_WIDTH, LANES)


def _nsa_in(x2d, nb, seq, g, w_in, q_gain, k_gain, tm=256):
    bt, d = x2d.shape
    tps = seq // tm
    width = w_in.shape[1]
    padded = d + 6 * KV_WIDTH + d + LANES
    w = jnp.pad(w_in, ((0, 0), (0, padded - width))).astype(BF16)
    row = lambda i: (i, 0)
    grp = lambda i: (i // tps, 0, i % tps, 0)
    kv_blk = pl.BlockSpec((1, N_KV_GROUPS, tm, HEAD_DIM), grp)
    v_blk = pl.BlockSpec((1, N_KV_GROUPS, tm, LANES), grp)
    k_b16 = jax.ShapeDtypeStruct((nb, N_KV_GROUPS, seq, HEAD_DIM), BF16)
    v_b16 = jax.ShapeDtypeStruct((nb, N_KV_GROUPS, seq, LANES), BF16)
    wide = jax.ShapeDtypeStruct((bt, d), BF16)
    assert seq // SLC_BLOCK <= HEAD_DIM
    qg = jnp.tile(q_gain * (LOG2_E * HEAD_DIM ** -0.5), PAIR).reshape(1, LANES)
    return pl.pallas_call(
        functools.partial(_nsa_in_kernel, tiles_per_seq=tps),
        grid=(bt // tm,),
        in_specs=[pl.BlockSpec((tm, d), row), _const_spec((1, d)), _const_spec((d, padded)),
                  _const_spec((1, LANES)), _const_spec((N_BRANCHES, LANES))],
        out_specs=[pl.BlockSpec((tm, d), row), kv_blk, kv_blk, v_blk, v_blk, kv_blk, v_blk,
                   pl.BlockSpec((tm, d), row), pl.BlockSpec((tm, LANES), row)],
        out_shape=[wide, k_b16, k_b16, v_b16, v_b16, k_b16, v_b16, wide,
                   jax.ShapeDtypeStruct((bt, LANES), jnp.float32)],
        compiler_params=pltpu.CompilerParams(dimension_semantics=("parallel",),
                                             vmem_limit_bytes=VMEM_LIMIT),
    )(x2d, g.reshape(1, d), w, qg, jnp.pad(k_gain, ((0, 0), (0, LANES - HEAD_DIM))))


def _gelu_tanh(x):
    return 0.5 * x * (1.0 + jnp.tanh(0.7978845608028654 * (x + 0.044715 * (x * x * x))))


def _compress_kernel(uk_ref, uv_ref, pe_ref, w1_ref, b1_ref, w2_ref, kg_ref, kc_out, vc_out):
    half = CMP_STRIDE * HEAD_DIM
    for kv, (u_ref, out) in enumerate(((uk_ref, kc_out), (uv_ref, vc_out))):
        uh = u_ref[0, 0]
        rows = uh.shape[0]
        ha = _bdot(uh + pe_ref[kv, 0:1, :], w1_ref[kv, :half, :])
        hb = _bdot(uh + pe_ref[kv, 1:2, :], w1_ref[kv, half:, :])
        hid = _gelu_tanh(ha + pltpu.roll(hb, rows - 1, axis=0) + b1_ref[kv])
        y = _bdot(hid, w2_ref[kv])
        if kv == 0:
            y = y[:, :HEAD_DIM]
            out[0, 0] = (_rms(y) * kg_ref[0:1, :]).astype(out.dtype)
        else:
            low_lanes = lax.broadcasted_iota(jnp.int32, y.shape, 1) < HEAD_DIM
            out[0, 0] = jnp.where(low_lanes, y, 1.0).astype(out.dtype)


def _compress(kc_raw, vc_raw, pe, w1, b1, w2, k_gain):
    nb, ng, seq, n = kc_raw.shape
    rows = seq // CMP_STRIDE
    half = CMP_STRIDE * n
    uk = kc_raw.reshape(nb, ng, rows, half)
    uv = vc_raw.reshape(nb, ng, rows, half)
    whole = lambda b, g: (b, g, 0, 0)
    u_blk = pl.BlockSpec((1, 1, rows, half), whole)
    w2p = jnp.pad(w2, ((0, 0), (0, 0), (0, LANES - n))).astype(BF16)
    return pl.pallas_call(
        _compress_kernel,
        grid=(nb, ng),
        in_specs=[u_blk, u_blk, _const_spec((2, 2, half)), _const_spec((2, 2 * half, CMP_HIDDEN)),
                  _const_spec((2, 1, CMP_HIDDEN)), _const_spec((2, CMP_HIDDEN, LANES)),
                  _const_spec((N_BRANCHES, n))],
        out_specs=[pl.BlockSpec((1, 1, rows, n), whole), pl.BlockSpec((1, 1, rows, LANES), whole)],
        out_shape=[jax.ShapeDtypeStruct((nb, ng, rows, n), BF16),
                   jax.ShapeDtypeStruct((nb, ng, rows, LANES), BF16)],
        compiler_params=pltpu.CompilerParams(dimension_semantics=("parallel", "parallel"),
                                             vmem_limit_bytes=VMEM_LIMIT),
    )(uk, uv, pe.reshape(2, 2, half), w1.astype(BF16), b1.reshape(2, 1, CMP_HIDDEN), w2p, k_gain)


M_INIT = -1e20


LOG2_E = 1.4426950408889634


MAX_STATIC_BOUND = 50.0


def _softmax_step(s, bias, v_tile, carry, hg, qt, online):
    sb = [s[h * qt:(h + 1) * qt] for h in range(hg)]
    if bias is not None:
        sb = [x + bias for x in sb]
    if not online:
        return jnp.concatenate(
            [carry[h * qt:(h + 1) * qt] + jnp.dot(jnp.exp2(sb[h]).astype(BF16), v_tile,
                                                   preferred_element_type=jnp.float32)
             for h in range(hg)], axis=0)
    m, acc = carry
    m_new = jnp.maximum(m, jnp.concatenate(
        [jnp.max(x, axis=-1, keepdims=True) for x in sb], axis=0))
    p = jnp.concatenate(
        [jnp.exp2(sb[h] - m_new[h * qt:(h + 1) * qt]).astype(BF16) for h in range(hg)], axis=0)
    acc = jnp.exp2(m - m_new) * acc + jnp.dot(p, v_tile, preferred_element_type=jnp.float32)
    return m_new, acc


N_ATTN_INPUTS = 10


def _nsa_attn_kernel(*refs, qt, kt, seq, online, tile):
    t0 = pl.program_id(2) * qt if tile is None else tile * qt
    _nsa_attn_tile(*refs[:N_ATTN_INPUTS], refs[-1], t0=t0, qt=qt, kt=kt, seq=seq, online=online)


def _nsa_attn_tile(bound_ref, q_ref, gate_ref, bcast_ref, kc_ref, vc_ref, ks_ref, vs_ref,
                   kw_ref, vw_ref, o_ref, *, t0, qt, kt, seq, online):
    hg = HEADS_PER_GROUP
    static = isinstance(t0, int)
    n_slc = seq // SLC_BLOCK
    n_cmp_rows = kc_ref.shape[2]
    if static:
        n_cmp_rows = min(n_cmp_rows, -(-((t0 + qt) // CMP_STRIDE) // LANES) * LANES)
    rows = hg * qt

    def aligned(x, m):
        return x if isinstance(x, int) else pl.multiple_of(x, m)
    low_lanes = lax.broadcasted_iota(jnp.int32, (qt, LANES), 1) < HEAD_DIM
    keep_c, keep_s, keep_w = (0.0, 0.0, 0.0) if online else (
        -bound_ref[0], -bound_ref[1], -bound_ref[2])

    def swap_halves(x):
        return pltpu.roll(x, HEAD_DIM, axis=1)

    q = q_ref[...].astype(jnp.float32)
    q_n = []
    for h in range(hg):
        two = q[:, (h // PAIR) * LANES:(h // PAIR + 1) * LANES]
        q_n.append(jnp.where(low_lanes, swap_halves(two) if h % PAIR else two, 0.0))
    qs = jnp.concatenate([x[:, :HEAD_DIM] for x in q_n], axis=0).astype(BF16)
    t_rows = t0 + lax.broadcasted_iota(jnp.int32, (qt, 1), 0)

    w0 = max(t0 - WINDOW, 0) if static else pl.multiple_of(jnp.maximum(t0 - WINDOW, 0), qt)
    span = t0 + qt - w0 if static else WINDOW + qt
    s = _bdot_nt(qs, kc_ref[0, 0, :n_cmp_rows, :])
    s_win = _bdot_nt(qs, kw_ref[0, 0, pl.ds(w0, span), :])

    cmp_end = lax.broadcasted_iota(jnp.int32, (1, n_cmp_rows), 1) * CMP_STRIDE + (CMP_BLOCK - 1)
    cbias = jnp.where(cmp_end <= t_rows, keep_c, NEG_INF)
    p_heads = []
    for h in range(hg):
        sb = s[h * qt:(h + 1) * qt] + cbias
        if online:
            sb = sb - jnp.maximum(jnp.max(sb, axis=-1, keepdims=True), M_INIT)
        e = jnp.exp2(sb)
        p_heads.append(e / jnp.maximum(jnp.sum(e, axis=-1, keepdims=True), 1e-37))
    o_cmp = jnp.dot(jnp.concatenate([p.astype(BF16) for p in p_heads], axis=0),
                    vc_ref[0, 0, :n_cmp_rows, :], preferred_element_type=jnp.float32)

    p_sum = p_heads[0]
    for h in range(1, hg):
        p_sum = p_sum + p_heads[h]
    jn = lax.broadcasted_iota(jnp.int32, (n_slc, n_cmp_rows), 0) * SLC_BLOCK
    nn = lax.broadcasted_iota(jnp.int32, (n_slc, n_cmp_rows), 1) * CMP_STRIDE
    ov_t = jnp.where((nn < jn + SLC_BLOCK) & (nn + (CMP_BLOCK - 1) >= jn), 1.0, 0.0).astype(BF16)
    p_hi = p_sum.astype(BF16)
    p_lo = (p_sum - p_hi.astype(p_sum.dtype)).astype(BF16)
    imp_t = (lax.dot_general(ov_t, p_hi, (((1,), (1,)), ((), ())), preferred_element_type=jnp.float32)
             + lax.dot_general(ov_t, p_lo, (((1,), (1,)), ((), ())), preferred_element_type=jnp.float32))

    lag = t_rows - (w0 + lax.broadcasted_iota(jnp.int32, (1, span), 1))
    wbias = jnp.where(lag >= 0, jnp.where(lag < WINDOW, keep_w, NEG_INF), NEG_INF)
    v_win = vw_ref[0, 0, pl.ds(w0, span), :]
    acc_win = []
    for h in range(hg):
        sb = s_win[h * qt:(h + 1) * qt] + wbias
        if online:
            sb = sb - jnp.max(sb, axis=-1, keepdims=True)
        acc_win.append(jnp.dot(jnp.exp2(sb).astype(BF16), v_win,
                               preferred_element_type=jnp.float32))
    acc_win = jnp.concatenate(acc_win, axis=0)

    jb = lax.broadcasted_iota(jnp.int32, (n_slc, qt), 0)
    tq = t0 + lax.broadcasted_iota(jnp.int32, (n_slc, qt), 1)
    dist = jnp.right_shift(tq, SLC_SHIFT) - jb
    forced = (jb == 0) | ((dist >= 0) & (dist < N_LOCAL_BLOCKS))
    score = jnp.where(dist >= 0, imp_t + jnp.where(forced, FORCE_BONUS, 0.0), -jnp.inf)
    sub = 8
    n_cand = (t0 + qt - 1) // SLC_BLOCK + 1 if static else n_slc
    if n_cand <= SLC_TOPK:
        n_cand = 0
    groups = [score[v * sub:(v + 1) * sub] for v in range(n_slc // sub)]
    jrow = lax.broadcasted_iota(jnp.int32, (sub, qt), 0)
    cnt = [jnp.zeros((sub, qt), jnp.float32) for _ in groups]
    for j2 in range(n_cand):
        row = score[j2:j2 + 1, :]
        for v, sv in enumerate(groups[:-(-n_cand // sub)]):
            if v * sub > j2:
                inc = jnp.where(row >= sv, 1.0, 0.0)
            elif (v + 1) * sub - 1 <= j2:
                inc = jnp.where(row > sv, 1.0, 0.0)
            else:
                inc = jnp.where(jrow + v * sub > j2, jnp.where(row >= sv, 1.0, 0.0),
                                jnp.where(row > sv, 1.0, 0.0))
            cnt[v] = cnt[v] + inc
    cnt = jnp.concatenate(cnt, axis=0)
    blk_bias_t = jnp.where(dist >= 0, jnp.where(cnt < min(SLC_TOPK, n_slc), keep_s, NEG_INF),
                           NEG_INF)
    parts = [jnp.zeros((HEAD_DIM, qt), jnp.float32), blk_bias_t]
    if n_slc < HEAD_DIM:
        parts.append(jnp.zeros((HEAD_DIM - n_slc, qt), jnp.float32))
    blk_bias = jnp.concatenate(parts, axis=0).T
    q_aug = jnp.concatenate([x + blk_bias for x in q_n], axis=0).astype(BF16)

    def slc_tile(k0, width, bias, carry):
        k_tile = ks_ref[0, 0, pl.ds(k0, width), :]
        s = jnp.concatenate([_bdot_nt(q_aug[h * qt:(h + 1) * qt], k_tile) for h in range(hg)],
                            axis=0)
        return _softmax_step(s, bias, vs_ref[0, 0, pl.ds(k0, width), :], carry, hg, qt, online)

    def slc_body(i, carry):
        return slc_tile(aligned(i * kt, kt), kt, None, carry)

    carry = jnp.zeros((rows, LANES), jnp.float32)
    if online:
        carry = (jnp.full((rows, 1), M_INIT, jnp.float32), carry)
    n_full = t0 // kt
    if static:
        for i in range(n_full):
            carry = slc_body(i, carry)
    else:
        carry = lax.fori_loop(0, n_full // 2,
                              lambda j, c: slc_body(2 * j + 1, slc_body(2 * j, c)), carry)
        carry = lax.cond(n_full % 2 == 1, lambda c: slc_body(n_full - 1, c), lambda c: c, carry)
    k0 = aligned(n_full * kt, kt)
    last = t0 + qt - k0 if static else kt
    causal = jnp.where(k0 + lax.broadcasted_iota(jnp.int32, (1, last), 1) <= t_rows, 0.0, NEG_INF)
    acc_slc = slc_tile(k0, last, causal, carry)
    if online:
        acc_slc = acc_slc[1]

    gates = 1.0 / (1.0 + jnp.exp(-gate_ref[...]))
    spread = jnp.dot(gates.astype(BF16), bcast_ref[0],
                     preferred_element_type=jnp.float32)

    def pair(acc, j, normalise):
        a0 = acc[(2 * j) * qt:(2 * j + 1) * qt]
        a1 = acc[(2 * j + 1) * qt:(2 * j + 2) * qt]
        num = jnp.where(low_lanes, a0, swap_halves(a1))
        if not normalise:
            return num
        return num / jnp.where(low_lanes, swap_halves(a0), a1)

    for j in range(hg // PAIR):
        width = PAIR * HEAD_DIM
        o = jnp.zeros((qt, width), jnp.float32)
        for br, (acc, normalise) in enumerate(((o_cmp, False), (acc_slc, True), (acc_win, True))):
            col = (br * (hg // PAIR) + j) * width
            o = o + spread[:, col:col + width] * pair(acc, j, normalise)
        o_ref[:, j * width:(j + 1) * width] = o.astype(o_ref.dtype)


def _nsa_attn(q, gate, q_gain, k_gain, kc, vc, ks, vs, kw, vw, qt=256, kt=512):
    bt, d = q.shape
    nb, ng, seq, n = kw.shape
    kt = min(kt, seq)
    assert seq >= WINDOW + qt and seq % kt == 0 and kt % qt == 0
    nq = seq // qt
    hg = HEADS_PER_GROUP
    width = hg * n
    rows_c = kc.shape[2]
    col = jnp.arange(N_BRANCHES * width) // n
    src = (col // hg) * N_HEADS + jnp.arange(ng)[:, None] * hg + col % hg
    bcast = (jnp.arange(LANES)[None, :, None] == src[:, None, :]).astype(BF16)
    bounds = (1.01 * LOG2_E * n ** 0.5) * jnp.max(jnp.abs(q_gain)) * jnp.max(jnp.abs(k_gain), axis=1)
    out_sds = jax.ShapeDtypeStruct((bt, d), BF16)

    def specs(tile):
        pick = (lambda b, g, i: i) if tile is None else (lambda b, g: tile)
        at = lambda f: (lambda *ids: f(ids[0], ids[1], pick(*ids)))
        keys = seq if tile is None else (tile + 1) * qt
        cmp_rows = rows_c if tile is None else min(rows_c, -(-(keys // CMP_STRIDE) // LANES) * LANES)
        whole = at(lambda b, g, i: (b, g, 0, 0))
        q_blk = pl.BlockSpec((qt, width), at(lambda b, g, i: (b * nq + i, g)))
        in_specs = [pl.BlockSpec(memory_space=pltpu.SMEM), q_blk,
                    pl.BlockSpec((qt, LANES), at(lambda b, g, i: (b * nq + i, 0))),
                    pl.BlockSpec((1, LANES, N_BRANCHES * width), at(lambda b, g, i: (g, 0, 0))),
                    pl.BlockSpec((1, 1, cmp_rows, n), whole),
                    pl.BlockSpec((1, 1, cmp_rows, LANES), whole),
                    pl.BlockSpec((1, 1, keys, LANES), whole), pl.BlockSpec((1, 1, keys, LANES), whole),
                    pl.BlockSpec((1, 1, keys, n), whole), pl.BlockSpec((1, 1, keys, LANES), whole)]
        return in_specs, q_blk

    def generic(*args):
        in_specs, q_blk = specs(None)
        return pl.pallas_call(
            functools.partial(_nsa_attn_kernel, qt=qt, kt=kt, seq=seq, online=True, tile=None),
            grid=(nb, ng, nq), in_specs=in_specs, out_specs=q_blk, out_shape=out_sds,
            compiler_params=pltpu.CompilerParams(
                dimension_semantics=("parallel", "parallel", "arbitrary"),
                vmem_limit_bytes=VMEM_LIMIT),
        )(*args)

    def per_tile(*args):
        o = None
        for tile in range(nq):
            in_specs, q_blk = specs(tile)
            operands = list(args)
            if o is not None:
                in_specs.append(pl.BlockSpec(memory_space=pl.ANY))
                operands.append(o)
            o = pl.pallas_call(
                functools.partial(_nsa_attn_kernel, qt=qt, kt=kt, seq=seq, online=False, tile=tile),
                grid=(nb, ng), in_specs=in_specs, out_specs=q_blk, out_shape=out_sds,
                input_output_aliases={} if tile == 0 else {N_ATTN_INPUTS: 0},
                compiler_params=pltpu.CompilerParams(
                    dimension_semantics=("parallel", "parallel"), vmem_limit_bytes=VMEM_LIMIT),
            )(*operands)
        return o

    return lax.cond(jnp.max(bounds) <= MAX_STATIC_BOUND, per_tile, generic,
                    bounds.astype(jnp.float32), q, gate, bcast, kc, vc, ks, vs, kw, vw)


def _nsa_layer(x2d, nb, seq, g, w_in, q_gain, k_gain, cmp_pe, cmp_w1, cmp_b1, cmp_w2, w_out):
    q, kc_raw, vc_raw, ks, vs, kw, vw, z, gate = _nsa_in(x2d, nb, seq, g, w_in, q_gain, k_gain)
    kc, vc = _compress(kc_raw, vc_raw, cmp_pe, cmp_w1, cmp_b1, cmp_w2, k_gain)
    o = _nsa_attn(q, gate, q_gain, k_gain, kc, vc, ks, vs, kw, vw)
    return _out_proj(x2d, o, z, w_out)


def kernel(x, norm_g, rwkv_mu, rwkv_w_in, rwkv_w0, rwkv_w1, rwkv_w2, rwkv_a0, rwkv_a1, rwkv_a2, rwkv_k_k, rwkv_k_a, rwkv_r_k, rwkv_lnx_g, rwkv_lnx_b, rwkv_w_out, nsa_w_in, nsa_q_gain, nsa_k_gain, nsa_cmp_pe, nsa_cmp_w1, nsa_cmp_b1, nsa_cmp_w2, nsa_w_out):
    b, t, d = x.shape
    x2d = x.reshape(b * t, d)
    x2d = _rwkv_layer(x2d, t, norm_g[0], rwkv_mu[0], rwkv_w_in[0], rwkv_w0[0], rwkv_w1[0],
                      rwkv_w2[0], rwkv_a0[0], rwkv_a1[0], rwkv_a2[0], rwkv_k_k[0], rwkv_k_a[0],
                      rwkv_r_k[0].reshape(-1), rwkv_lnx_g[0], rwkv_lnx_b[0], rwkv_w_out[0])
    x2d = _nsa_layer(x2d, b, t, norm_g[1], nsa_w_in[0], nsa_q_gain[0], nsa_k_gain[0],
                     nsa_cmp_pe[0], nsa_cmp_w1[0], nsa_cmp_b1[0], nsa_cmp_w2[0], nsa_w_out[0])
    return x2d.reshape(b, t, d)
```

```python
import functools

import jax
import jax.numpy as jnp
from jax import lax
from jax.experimental import pallas as pl
from jax.experimental.pallas import tpu as pltpu

D_MODEL = 1024
HEAD_DIM = 64
N_HEADS = D_MODEL // HEAD_DIM
NORM_EPS = 1e-6
LN_X_EPS = 64e-5
DECAY_LORA = 64
ICLR_LORA = 64
N_KV_GROUPS = 4
HEADS_PER_GROUP = N_HEADS // N_KV_GROUPS
KV_WIDTH = N_KV_GROUPS * HEAD_DIM
N_BRANCHES = 3
CMP_BLOCK = 32
CMP_STRIDE = 16
CMP_HIDDEN = 256
SLC_BLOCK = 64
SLC_SHIFT = 6
SLC_TOPK = 16
N_LOCAL_BLOCKS = 2
WINDOW = 512
FORCE_BONUS = 1e4
NEG_INF = -1e30

LANES = 128
MXU_DEPTH = 256
PAIR = LANES // HEAD_DIM
N_PAIRS = N_HEADS // PAIR
HEAD_SHIFT = 6
WKV_HEADS = LANES // HEAD_DIM
CHUNK = 64
CHUNK_SHIFT = 6
VMEM_LIMIT = 48 * 1024 * 1024

BF16 = jnp.bfloat16


def _rms(x, eps=NORM_EPS):
    return x * lax.rsqrt(jnp.mean(x * x, axis=-1, keepdims=True) + eps)


def _bdot(a, b):
    return jnp.dot(a.astype(BF16), b.astype(BF16), preferred_element_type=jnp.float32)


def _bdot_nt(a, b):
    return lax.dot_general(a.astype(BF16), b.astype(BF16), (((1,), (1,)), ((), ())),
                           preferred_element_type=jnp.float32)


def _const_spec(shape):
    nd = len(shape)
    return pl.BlockSpec(shape, lambda *_: (0,) * nd, pipeline_mode=pl.Buffered(1))


def _rwkv_in_kernel(x_ref, xp_ref, g_ref, mu_ref, win_ref, w0_ref, w1_ref, w2_ref, a0_ref,
                    a1_ref, a2_ref, kk_ref, ka_ref,
                    r_out, lw_out, cum_out, k_out, v_out, kk_out, a_out, z_out, *, tiles_per_seq):
    d = D_MODEL
    g = g_ref[...]
    h = _rms(x_ref[...]) * g
    hp = _rms(xp_ref[7:8, :]) * g
    hp = jnp.where(pl.program_id(0) % tiles_per_seq == 0, 0.0, hp)
    row = lax.broadcasted_iota(jnp.int32, h.shape, 0)
    hs = jnp.where(row == 0, hp, pltpu.roll(h, 1, axis=0))
    dh = hs - h

    def mix(c):
        return h + dh * mu_ref[c:c + 1, :]

    r_out[...] = _bdot(mix(0), win_ref[:, 0 * d:1 * d]).astype(r_out.dtype)
    k = _bdot(mix(1), win_ref[:, 1 * d:2 * d])
    v_out[...] = _bdot(mix(2), win_ref[:, 2 * d:3 * d]).astype(v_out.dtype)
    z_out[...] = _bdot(mix(3), win_ref[:, 3 * d:4 * d]).astype(z_out.dtype)

    u = w0_ref[...] + _bdot(jnp.tanh(_bdot(mix(4), w1_ref[...])), w2_ref[...])
    sp = jnp.maximum(-u, 0.0) + jnp.log(1.0 + jnp.exp(-jnp.abs(u)))
    lw = -jnp.exp(-sp - 0.5)
    lw_out[...] = lw
    rows = min(lw.shape[0], MXU_DEPTH)
    ri = lax.broadcasted_iota(jnp.int32, (rows, rows), 0)
    ci = lax.broadcasted_iota(jnp.int32, (rows, rows), 1)
    same_chunk = jnp.right_shift(ri, CHUNK_SHIFT) == jnp.right_shift(ci, CHUNK_SHIFT)
    tril = jnp.where(ci <= ri, jnp.where(same_chunk, 1.0, 0.0), 0.0).astype(BF16)
    for r0 in range(0, lw.shape[0], rows):
        hi, mid, lo = _split3(lw[r0:r0 + rows])
        cum_out[r0:r0 + rows, :] = (jnp.dot(tril, hi, preferred_element_type=jnp.float32)
                                    + jnp.dot(tril, mid, preferred_element_type=jnp.float32)
                                    + jnp.dot(tril, lo, preferred_element_type=jnp.float32))
    ua = a0_ref[...] + _bdot(_bdot(mix(5), a1_ref[...]), a2_ref[...])
    a = 1.0 / (1.0 + jnp.exp(-ua))
    a_out[...] = a.astype(a_out.dtype)
    kk_out[...] = (k * kk_ref[...]).astype(kk_out.dtype)
    k_out[...] = (k * (1.0 + (a - 1.0) * ka_ref[...])).astype(k_out.dtype)


def _rwkv_in(x2d, seq, g, mu, w_in, w0, w1, w2, a0, a1, a2, k_k, k_a, tm=512):
    bt, d = x2d.shape
    tiles_per_seq = seq // tm
    row = lambda i: (i, 0)
    prev = lambda i: (jnp.maximum(i * (tm // 8) - 1, 0), 0)
    f32_sds = jax.ShapeDtypeStruct((bt, d), jnp.float32)
    b16_sds = jax.ShapeDtypeStruct((bt, d), BF16)
    return pl.pallas_call(
        functools.partial(_rwkv_in_kernel, tiles_per_seq=tiles_per_seq),
        grid=(bt // tm,),
        in_specs=[pl.BlockSpec((tm, d), row), pl.BlockSpec((8, d), prev),
                  _const_spec((1, d)), _const_spec((6, d)), _const_spec((d, 4 * d)),
                  _const_spec((1, d)), _const_spec((d, DECAY_LORA)), _const_spec((DECAY_LORA, d)),
                  _const_spec((1, d)), _const_spec((d, ICLR_LORA)), _const_spec((ICLR_LORA, d)),
                  _const_spec((1, d)), _const_spec((1, d))],
        out_specs=[pl.BlockSpec((tm, d), row)] * 8,
        out_shape=[b16_sds, f32_sds, f32_sds] + [b16_sds] * 5,
        compiler_params=pltpu.CompilerParams(dimension_semantics=("parallel",),
                                             vmem_limit_bytes=VMEM_LIMIT),
    )(x2d, x2d, g.reshape(1, d), mu, w_in.astype(BF16), w0.reshape(1, d), w1.astype(BF16),
      w2.astype(BF16), a0.reshape(1, d), a1.astype(BF16), a2.astype(BF16), k_k.reshape(1, d),
      k_a.reshape(1, d))


def _split3(x):
    hi = x.astype(BF16)
    r1 = x - hi.astype(x.dtype)
    mid = r1.astype(BF16)
    lo = (r1 - mid.astype(x.dtype)).astype(BF16)
    return hi, mid, lo


def _wkv_kernel(r_ref, lw_ref, cum_ref, k_ref, v_ref, kk_ref, a_ref, rk_ref, lng_ref, lnb_ref,
                o_ref, state_ref):
    @pl.when(pl.program_id(1) == 0)
    def _():
        state_ref[...] = jnp.zeros_like(state_ref)

    c = CHUNK
    nh = WKV_HEADS
    uw = nh * HEAD_DIM
    pairs = range(N_HEADS // nh)
    sls = [slice(p * uw, (p + 1) * uw) for p in pairs]
    m0 = lax.broadcasted_iota(jnp.int32, (c, LANES), 1) < HEAD_DIM
    lane_head = jnp.right_shift(lax.broadcasted_iota(jnp.int32, (c, uw), 1), HEAD_SHIFT)
    head_masks = [lane_head == h for h in range(nh)]

    def head_sum(x):
        out = []
        for j in range(uw // LANES):
            xj = x[:, j * LANES:(j + 1) * LANES]
            s0 = jnp.sum(jnp.where(m0, xj, 0.0), axis=-1, keepdims=True)
            s1 = jnp.sum(jnp.where(m0, 0.0, xj), axis=-1, keepdims=True)
            out.append(jnp.where(m0, s0, s1))
        return jnp.concatenate(out, axis=1)

    def stack(y):
        return jnp.concatenate([jnp.where(m, y, 0.0) for m in head_masks], axis=0)

    def each(f, *lists):
        return [f(*xs) for xs in zip(*lists)]

    n_rows, n_chunks = r_ref.shape[0], r_ref.shape[1] // c
    f32 = cum_ref.dtype
    w = nh * c
    row2 = lax.broadcasted_iota(jnp.int32, (c, w), 0)
    col2 = jnp.bitwise_and(lax.broadcasted_iota(jnp.int32, (c, w), 1), c - 1)
    strict = col2 < row2
    incl = col2 <= row2
    eye2 = jnp.where(col2 == row2, 1.0, 0.0)
    diag = (jnp.right_shift(lax.broadcasted_iota(jnp.int32, (uw, uw), 0), HEAD_SHIFT)
            == jnp.right_shift(lax.broadcasted_iota(jnp.int32, (uw, uw), 1), HEAD_SHIFT))

    def prep_stages(ci, d):
        items = [(bi, slice(ci * c, (ci + 1) * c), s) for bi in range(n_rows) for s in sls]
        d.update(items=items, r=[], k=[], v=[], cum=[], b=[], ar=[], kb_t=[])
        for i, it in enumerate(items):
            r, k, cum = r_ref[it].astype(f32), k_ref[it].astype(f32), cum_ref[it]
            kkraw = kk_ref[it].astype(f32)
            kk = kkraw / jnp.maximum(jnp.sqrt(head_sum(kkraw * kkraw)), 1e-12)
            b = kk * a_ref[it].astype(f32)
            g_inv = jnp.exp(-cum)
            at = -kk * jnp.exp(cum - lw_ref[it])
            d['ar'].append(jnp.concatenate([at, r * jnp.exp(cum)], axis=0))
            d['kb_t'].append(jnp.concatenate([stack(k * g_inv), stack(b * g_inv)], axis=0))
            for name, val in (('r', r), ('k', k), ('v', v_ref[it].astype(f32)), ('cum', cum),
                              ('b', b)):
                d[name].append(val)
            if i % 2:
                yield

    def head_stages(d):
        v = d['v']
        big = each(_bdot_nt, d['ar'], d['kb_t'])
        yield
        x = [jnp.where(strict, g[:c, w:], 0.0) for g in big]
        t = [eye2 + xi for xi in x]
        x = each(lambda xi: _bdot(xi, stack(xi)), x)
        yield
        for _ in range(4):
            tx = each(lambda ti, xi: _bdot(jnp.concatenate([ti, xi], axis=0), stack(xi)), t, x)
            t = each(lambda ti, r_: ti + r_[:c], t, tx)
            x = [r_[c:] for r_ in tx]
            yield
        t = each(lambda ti, xi: ti + _bdot(ti, stack(xi)), t, x)
        yield
        av = each(lambda g, vi: _bdot(
            jnp.concatenate([jnp.where(strict, g[:c, :w], 0.0),
                             jnp.where(incl, g[c:, :w], 0.0)], axis=0), stack(vi)), big, v)
        d.update(big=big, t=t, av=av)
        yield

    def tail_stages(d, box):
        state = box[0]
        ar_s = each(_bdot_nt, d['ar'], state)
        yield
        u = each(lambda ti, s, avi: _bdot(ti, stack(s[:c] + avi[:c])), d['t'], ar_s, d['av'])
        yield
        o = each(lambda g, s, avi, ui: s[c:] + avi[c:] + _bdot(
            jnp.where(incl, g[c:, w:], 0.0), stack(ui)), d['big'], ar_s, d['av'], u)
        new_state = []
        for st, ui, vi, ki, bi_, cumi in zip(state, u, d['v'], d['k'], d['b'], d['cum']):
            cum_last = cumi[c - 1:c, :]
            g_rem = jnp.exp(cum_last - cumi)
            vu = jnp.concatenate([vi, ui], axis=0)
            kb = jnp.concatenate([ki * g_rem, bi_ * g_rem], axis=0)
            upd = _bdot(vu.T, kb)
            new_state.append(st * jnp.exp(cum_last) + jnp.where(diag, upd, 0.0))
        box[0] = new_state
        yield
        for it, oi, ri, ki, vi in zip(d['items'], o, d['r'], d['k'], d['v']):
            s = it[2]
            mean = head_sum(oi) * (1.0 / HEAD_DIM)
            dev = oi - mean
            var = head_sum(dev * dev) * (1.0 / HEAD_DIM)
            y = dev * lax.rsqrt(var + LN_X_EPS) * lng_ref[:, s] + lnb_ref[:, s]
            o_ref[it] = (y + head_sum(ri * ki * rk_ref[:, s]) * vi).astype(o_ref.dtype)
        yield

    n_state = n_rows * len(sls)
    box = [[state_ref[j] for j in range(n_state)]]
    chunks = [dict() for _ in range(n_chunks)]
    for step in range(n_chunks + 2):
        live = {}
        if 0 < step <= n_chunks:
            live['head'] = head_stages(chunks[step - 1])
        if step < n_chunks:
            live['prep'] = prep_stages(step, chunks[step])
        if step > 1:
            live['tail'] = tail_stages(chunks[step - 2], box)
        n = 0
        while live:
            n += 1
            for name in ('head', 'prep', 'tail'):
                if name not in live or (name == 'tail' and n % 2 and len(live) > 1):
                    continue
                if next(live[name], StopIteration) is StopIteration:
                    del live[name]
    for j in range(n_state):
        state_ref[j] = box[0][j]


def _wkv(r, lw, cum, k, v, kkraw, a, seq, r_k, lnx_g, lnx_b, chunks_per_step=4, rows_per_step=2):
    bt, d = r.shape
    nb = bt // seq
    rows = chunks_per_step * CHUNK
    blk = pl.BlockSpec((rows_per_step, rows, d), lambda b, c: (b, c, 0))
    uw = WKV_HEADS * HEAD_DIM
    streams = [x.reshape(nb, seq, d) for x in (r, lw, cum, k, v, kkraw, a)]
    return pl.pallas_call(
        _wkv_kernel,
        grid=(nb // rows_per_step, seq // rows),
        in_specs=[blk] * 7 + [_const_spec((1, d))] * 3,
        out_specs=blk,
        out_shape=jax.ShapeDtypeStruct((nb, seq, d), BF16),
        scratch_shapes=[pltpu.VMEM((rows_per_step * (N_HEADS // WKV_HEADS), uw, uw), jnp.float32)],
        compiler_params=pltpu.CompilerParams(dimension_semantics=("parallel", "arbitrary"),
                                             vmem_limit_bytes=VMEM_LIMIT),
    )(*streams, r_k.reshape(1, d), lnx_g.reshape(1, d), lnx_b.reshape(1, d)).reshape(bt, d)


def _out_proj_kernel(x_ref, o_ref, z_ref, w_ref, y_ref):
    x = x_ref[...]
    z = z_ref[...].astype(x.dtype)
    y = o_ref[...].astype(x.dtype) * (z / (1.0 + jnp.exp(-z)))
    y_ref[...] = x + _bdot(y, w_ref[...])


def _out_proj(x2d, o, z, w_out, tm=1024):
    bt, d = x2d.shape
    row = lambda i: (i, 0)
    blk = pl.BlockSpec((tm, d), row)
    return pl.pallas_call(
        _out_proj_kernel,
        grid=(bt // tm,),
        in_specs=[blk, blk, blk, _const_spec((d, d))],
        out_specs=blk,
        out_shape=jax.ShapeDtypeStruct((bt, d), jnp.float32),
        compiler_params=pltpu.CompilerParams(dimension_semantics=("parallel",),
                                             vmem_limit_bytes=VMEM_LIMIT),
    )(x2d, o, z, w_out.astype(BF16))


def _rwkv_layer(x2d, seq, g, mu, w_in, w0, w1, w2, a0, a1, a2, k_k, k_a, r_k, lnx_g, lnx_b, w_out):
    r, lw, cum, k, v, kkraw, a, z = _rwkv_in(x2d, seq, g, mu, w_in, w0, w1, w2, a0, a1, a2, k_k,
                                             k_a)
    o = _wkv(r, lw, cum, k, v, kkraw, a, seq, r_k, lnx_g, lnx_b)
    return _out_proj(x2d, o, z, w_out)


def _nsa_in_kernel(x_ref, g_ref, w_ref, qg_ref, kg_ref, q_out, kc_out, vc_out, ks_out, vs_out,
                   kw_out, vw_out, z_out, gate_out, scr_ref, *, tiles_per_seq):
    d = D_MODEL
    h = (_rms(x_ref[...]) * g_ref[...]).astype(BF16)
    tm = h.shape[0]

    def proj(off, width):
        return jnp.dot(h, w_ref[:, off:off + width], preferred_element_type=jnp.float32)

    lane = lax.broadcasted_iota(jnp.int32, (tm, LANES), 1)
    low_lanes = lane < HEAD_DIM
    tok = (pl.program_id(0) % tiles_per_seq) * tm + lax.broadcasted_iota(jnp.int32, (tm, LANES), 0)
    blk_bias = jnp.where(lane - HEAD_DIM == jnp.right_shift(tok, SLC_SHIFT), 1.0, 0.0)

    def slot(p, g):
        two = p[:, (g // PAIR) * LANES:(g // PAIR + 1) * LANES]
        return pltpu.roll(two, HEAD_DIM, axis=1) if g % PAIR else two

    def narrow(ref, p, gain_row):
        for g in range(N_KV_GROUPS):
            seg = p[:, g * HEAD_DIM:(g + 1) * HEAD_DIM]
            if gain_row is not None:
                seg = _rms(seg) * kg_ref[gain_row:gain_row + 1, :HEAD_DIM]
            ref[0, g] = seg.astype(ref.dtype)

    def values(ref, p):
        for g in range(N_KV_GROUPS):
            ref[0, g] = jnp.where(low_lanes, slot(p, g), 1.0).astype(ref.dtype)

    p = proj(0, d)
    for j in range(N_PAIRS):
        x = p[:, j * LANES:(j + 1) * LANES]
        sq = x * x
        ms = jnp.where(low_lanes, jnp.sum(jnp.where(low_lanes, sq, 0.0), axis=-1, keepdims=True),
                       jnp.sum(jnp.where(low_lanes, 0.0, sq), axis=-1, keepdims=True))
        q_out[:, j * LANES:(j + 1) * LANES] = (
            x * lax.rsqrt(ms * (1.0 / HEAD_DIM) + NORM_EPS) * qg_ref[...]).astype(q_out.dtype)
    def flattened(ref, p):
        for j in range(KV_WIDTH // LANES):
            scr_ref[j] = p[:, j * LANES:(j + 1) * LANES]
        n_rows = tm // CMP_STRIDE
        low = lax.broadcasted_iota(jnp.int32, (n_rows, LANES), 1) < HEAD_DIM
        for m in range(CMP_STRIDE // PAIR):
            for g in range(N_KV_GROUPS):
                a = scr_ref[g // PAIR, pl.ds(PAIR * m, n_rows, stride=CMP_STRIDE), :]
                b = scr_ref[g // PAIR, pl.ds(PAIR * m + 1, n_rows, stride=CMP_STRIDE), :]
                a = pltpu.roll(a, HEAD_DIM, axis=1) if g % PAIR else a
                b = b if g % PAIR else pltpu.roll(b, HEAD_DIM, axis=1)
                ref[0, g, :, m * LANES:(m + 1) * LANES] = jnp.where(low, a, b).astype(ref.dtype)

    flattened(kc_out, proj(d, KV_WIDTH))
    flattened(vc_out, proj(d + KV_WIDTH, KV_WIDTH))
    p = proj(d + 2 * KV_WIDTH, KV_WIDTH)
    for g in range(N_KV_GROUPS):
        x = jnp.where(low_lanes, slot(p, g), 0.0)
        ms = jnp.sum(x * x, axis=-1, keepdims=True) * (1.0 / HEAD_DIM)
        ks_out[0, g] = (x * lax.rsqrt(ms + NORM_EPS) * kg_ref[1:2, :] + blk_bias).astype(BF16)
    values(vs_out, proj(d + 3 * KV_WIDTH, KV_WIDTH))
    narrow(kw_out, proj(d + 4 * KV_WIDTH, KV_WIDTH), 2)
    values(vw_out, proj(d + 5 * KV_WIDTH, KV_WIDTH))
    z_out[...] = proj(d + 6 * KV_WIDTH, d).astype(z_out.dtype)
    gate_out[...] = proj(2 * d + 6 * KV_WIDTH, LANES)


def _nsa_in(x2d, nb, seq, g, w_in, q_gain, k_gain, tm=256):
    bt, d = x2d.shape
    tps = seq // tm
    width = w_in.shape[1]
    padded = d + 6 * KV_WIDTH + d + LANES
    w = jnp.pad(w_in, ((0, 0), (0, padded - width))).astype(BF16)
    row = lambda i: (i, 0)
    grp = lambda i: (i // tps, 0, i % tps, 0)
    kv_blk = pl.BlockSpec((1, N_KV_GROUPS, tm, HEAD_DIM), grp)
    v_blk = pl.BlockSpec((1, N_KV_GROUPS, tm, LANES), grp)
    half = CMP_STRIDE * HEAD_DIM
    u_blk = pl.BlockSpec((1, N_KV_GROUPS, tm // CMP_STRIDE, half), grp)
    u_b16 = jax.ShapeDtypeStruct((nb, N_KV_GROUPS, seq // CMP_STRIDE, half), BF16)
    k_b16 = jax.ShapeDtypeStruct((nb, N_KV_GROUPS, seq, HEAD_DIM), BF16)
    v_b16 = jax.ShapeDtypeStruct((nb, N_KV_GROUPS, seq, LANES), BF16)
    wide = jax.ShapeDtypeStruct((bt, d), BF16)
    assert seq // SLC_BLOCK <= HEAD_DIM
    qg = jnp.tile(q_gain * (LOG2_E * HEAD_DIM ** -0.5), PAIR).reshape(1, LANES)
    return pl.pallas_call(
        functools.partial(_nsa_in_kernel, tiles_per_seq=tps),
        grid=(bt // tm,),
        in_specs=[pl.BlockSpec((tm, d), row), _const_spec((1, d)), _const_spec((d, padded)),
                  _const_spec((1, LANES)), _const_spec((N_BRANCHES, LANES))],
        out_specs=[pl.BlockSpec((tm, d), row), u_blk, u_blk, v_blk, v_blk, kv_blk, v_blk,
                   pl.BlockSpec((tm, d), row), pl.BlockSpec((tm, LANES), row)],
        out_shape=[wide, u_b16, u_b16, v_b16, v_b16, k_b16, v_b16, wide,
                   jax.ShapeDtypeStruct((bt, LANES), jnp.float32)],
        scratch_shapes=[pltpu.VMEM((KV_WIDTH // LANES, tm, LANES), jnp.float32)],
        compiler_params=pltpu.CompilerParams(dimension_semantics=("parallel",),
                                             vmem_limit_bytes=VMEM_LIMIT),
    )(x2d, g.reshape(1, d), w, qg, jnp.pad(k_gain, ((0, 0), (0, LANES - HEAD_DIM))))


def _gelu_tanh(x):
    return 0.5 * x * (1.0 + jnp.tanh(0.7978845608028654 * (x + 0.044715 * (x * x * x))))


def _compress_kernel(uk_ref, uv_ref, pe_ref, w1_ref, b1_ref, w2_ref, kg_ref, kc_out, vc_out):
    half = CMP_STRIDE * HEAD_DIM
    for kv, (u_ref, out) in enumerate(((uk_ref, kc_out), (uv_ref, vc_out))):
        uh = u_ref[0, 0]
        rows = uh.shape[0]
        ha = _bdot(uh + pe_ref[kv, 0:1, :], w1_ref[kv, :half, :])
        hb = _bdot(uh + pe_ref[kv, 1:2, :], w1_ref[kv, half:, :])
        hid = _gelu_tanh(ha + pltpu.roll(hb, rows - 1, axis=0) + b1_ref[kv])
        y = _bdot(hid, w2_ref[kv])
        if kv == 0:
            y = y[:, :HEAD_DIM]
            out[0, 0] = (_rms(y) * kg_ref[0:1, :]).astype(out.dtype)
        else:
            low_lanes = lax.broadcasted_iota(jnp.int32, y.shape, 1) < HEAD_DIM
            out[0, 0] = jnp.where(low_lanes, y, 1.0).astype(out.dtype)


def _compress(kc_raw, vc_raw, pe, w1, b1, w2, k_gain):
    nb, ng, rows, half = kc_raw.shape
    n = half // CMP_STRIDE
    uk, uv = kc_raw, vc_raw
    whole = lambda b, g: (b, g, 0, 0)
    u_blk = pl.BlockSpec((1, 1, rows, half), whole)
    w2p = jnp.pad(w2, ((0, 0), (0, 0), (0, LANES - n))).astype(BF16)
    return pl.pallas_call(
        _compress_kernel,
        grid=(nb, ng),
        in_specs=[u_blk, u_blk, _const_spec((2, 2, half)), _const_spec((2, 2 * half, CMP_HIDDEN)),
                  _const_spec((2, 1, CMP_HIDDEN)), _const_spec((2, CMP_HIDDEN, LANES)),
                  _const_spec((N_BRANCHES, n))],
        out_specs=[pl.BlockSpec((1, 1, rows, n), whole), pl.BlockSpec((1, 1, rows, LANES), whole)],
        out_shape=[jax.ShapeDtypeStruct((nb, ng, rows, n), BF16),
                   jax.ShapeDtypeStruct((nb, ng, rows, LANES), BF16)],
        compiler_params=pltpu.CompilerParams(dimension_semantics=("parallel", "parallel"),
                                             vmem_limit_bytes=VMEM_LIMIT),
    )(uk, uv, pe.reshape(2, 2, half), w1.astype(BF16), b1.reshape(2, 1, CMP_HIDDEN), w2p, k_gain)


M_INIT = -1e20


LOG2_E = 1.4426950408889634


MAX_STATIC_BOUND = 50.0


def _softmax_step(s, bias, v_tile, carry, hg, qt, online):
    sb = [s[h * qt:(h + 1) * qt] for h in range(hg)]
    if bias is not None:
        sb = [x + bias for x in sb]
    if not online:
        return jnp.concatenate(
            [carry[h * qt:(h + 1) * qt] + jnp.dot(jnp.exp2(sb[h]).astype(BF16), v_tile,
                                                   preferred_element_type=jnp.float32)
             for h in range(hg)], axis=0)
    m, acc = carry
    m_new = jnp.maximum(m, jnp.concatenate(
        [jnp.max(x, axis=-1, keepdims=True) for x in sb], axis=0))
    p = jnp.concatenate(
        [jnp.exp2(sb[h] - m_new[h * qt:(h + 1) * qt]).astype(BF16) for h in range(hg)], axis=0)
    acc = jnp.exp2(m - m_new) * acc + jnp.dot(p, v_tile, preferred_element_type=jnp.float32)
    return m_new, acc


N_ATTN_INPUTS = 10


def _nsa_attn_kernel(*refs, qt, kt, seq, online, tile):
    t0 = pl.program_id(2) * qt if tile is None else tile * qt
    _nsa_attn_tile(*refs[:N_ATTN_INPUTS], refs[-1], t0=t0, qt=qt, kt=kt, seq=seq, online=online)


def _nsa_attn_tile(bound_ref, q_ref, gate_ref, bcast_ref, kc_ref, vc_ref, ks_ref, vs_ref,
                   kw_ref, vw_ref, o_ref, *, t0, qt, kt, seq, online):
    hg = HEADS_PER_GROUP
    static = isinstance(t0, int)
    n_slc = seq // SLC_BLOCK
    n_cmp_rows = kc_ref.shape[2]
    if static:
        n_cmp_rows = min(n_cmp_rows, -(-((t0 + qt) // CMP_STRIDE) // LANES) * LANES)
    rows = hg * qt

    def aligned(x, m):
        return x if isinstance(x, int) else pl.multiple_of(x, m)
    low_lanes = lax.broadcasted_iota(jnp.int32, (qt, LANES), 1) < HEAD_DIM
    keep_c, keep_s, keep_w = (0.0, 0.0, 0.0) if online else (
        -bound_ref[0], -bound_ref[1], -bound_ref[2])

    def swap_halves(x):
        return pltpu.roll(x, HEAD_DIM, axis=1)

    q = q_ref[...].astype(jnp.float32)
    q_n = []
    for h in range(hg):
        two = q[:, (h // PAIR) * LANES:(h // PAIR + 1) * LANES]
        q_n.append(jnp.where(low_lanes, swap_halves(two) if h % PAIR else two, 0.0))
    qs = jnp.concatenate([x[:, :HEAD_DIM] for x in q_n], axis=0).astype(BF16)
    t_rows = t0 + lax.broadcasted_iota(jnp.int32, (qt, 1), 0)

    w0 = max(t0 - WINDOW, 0) if static else pl.multiple_of(jnp.maximum(t0 - WINDOW, 0), qt)
    span = t0 + qt - w0 if static else WINDOW + qt
    s = _bdot_nt(qs, kc_ref[0, 0, :n_cmp_rows, :])
    s_win = _bdot_nt(qs, kw_ref[0, 0, pl.ds(w0, span), :])

    cmp_end = lax.broadcasted_iota(jnp.int32, (1, n_cmp_rows), 1) * CMP_STRIDE + (CMP_BLOCK - 1)
    cbias = jnp.where(cmp_end <= t_rows, keep_c, NEG_INF)
    p_heads = []
    for h in range(hg):
        sb = s[h * qt:(h + 1) * qt] + cbias
        if online:
            sb = sb - jnp.maximum(jnp.max(sb, axis=-1, keepdims=True), M_INIT)
        e = jnp.exp2(sb)
        p_heads.append(e / jnp.maximum(jnp.sum(e, axis=-1, keepdims=True), 1e-37))
    o_cmp = jnp.dot(jnp.concatenate([p.astype(BF16) for p in p_heads], axis=0),
                    vc_ref[0, 0, :n_cmp_rows, :], preferred_element_type=jnp.float32)

    p_sum = p_heads[0]
    for h in range(1, hg):
        p_sum = p_sum + p_heads[h]
    jn = lax.broadcasted_iota(jnp.int32, (n_slc, n_cmp_rows), 0) * SLC_BLOCK
    nn = lax.broadcasted_iota(jnp.int32, (n_slc, n_cmp_rows), 1) * CMP_STRIDE
    ov_t = jnp.where((nn < jn + SLC_BLOCK) & (nn + (CMP_BLOCK - 1) >= jn), 1.0, 0.0).astype(BF16)
    p_hi = p_sum.astype(BF16)
    p_lo = (p_sum - p_hi.astype(p_sum.dtype)).astype(BF16)
    imp_t = (lax.dot_general(ov_t, p_hi, (((1,), (1,)), ((), ())), preferred_element_type=jnp.float32)
             + lax.dot_general(ov_t, p_lo, (((1,), (1,)), ((), ())), preferred_element_type=jnp.float32))

    jb = lax.broadcasted_iota(jnp.int32, (n_slc, qt), 0)
    tq = t0 + lax.broadcasted_iota(jnp.int32, (n_slc, qt), 1)
    dist = jnp.right_shift(tq, SLC_SHIFT) - jb
    forced = (jb == 0) | ((dist >= 0) & (dist < N_LOCAL_BLOCKS))
    score = jnp.where(dist >= 0, imp_t + jnp.where(forced, FORCE_BONUS, 0.0), -jnp.inf)
    sub = 8
    n_cand = (t0 + qt - 1) // SLC_BLOCK + 1 if static else n_slc
    if n_cand <= SLC_TOPK:
        n_cand = 0
    groups = [score[v * sub:(v + 1) * sub] for v in range(n_slc // sub)]
    jrow = lax.broadcasted_iota(jnp.int32, (sub, qt), 0)
    cnt = [jnp.zeros((sub, qt), jnp.float32) for _ in groups]
    for j2 in range(n_cand):
        row = score[j2:j2 + 1, :]
        for v, sv in enumerate(groups[:-(-n_cand // sub)]):
            if v * sub > j2:
                inc = jnp.where(row >= sv, 1.0, 0.0)
            elif (v + 1) * sub - 1 <= j2:
                inc = jnp.where(row > sv, 1.0, 0.0)
            else:
                inc = jnp.where(jrow + v * sub > j2, jnp.where(row >= sv, 1.0, 0.0),
                                jnp.where(row > sv, 1.0, 0.0))
            cnt[v] = cnt[v] + inc
    cnt = jnp.concatenate(cnt, axis=0)
    blk_bias_t = jnp.where(dist >= 0, jnp.where(cnt < min(SLC_TOPK, n_slc), keep_s, NEG_INF),
                           NEG_INF)
    parts = [jnp.zeros((HEAD_DIM, qt), jnp.float32), blk_bias_t]
    if n_slc < HEAD_DIM:
        parts.append(jnp.zeros((HEAD_DIM - n_slc, qt), jnp.float32))
    blk_bias = jnp.concatenate(parts, axis=0).T
    q_aug = jnp.concatenate([x + blk_bias for x in q_n], axis=0).astype(BF16)

    lag = t_rows - (w0 + lax.broadcasted_iota(jnp.int32, (1, span), 1))
    wbias = jnp.where(lag >= 0, jnp.where(lag < WINDOW, keep_w, NEG_INF), NEG_INF)
    v_win = vw_ref[0, 0, pl.ds(w0, span), :]
    acc_win = []
    for h in range(hg):
        sb = s_win[h * qt:(h + 1) * qt] + wbias
        if online:
            sb = sb - jnp.max(sb, axis=-1, keepdims=True)
        acc_win.append(jnp.dot(jnp.exp2(sb).astype(BF16), v_win,
                               preferred_element_type=jnp.float32))
    acc_win = jnp.concatenate(acc_win, axis=0)

    def slc_tile(k0, width, bias, carry):
        k_tile = ks_ref[0, 0, pl.ds(k0, width), :]
        s = jnp.concatenate([_bdot_nt(q_aug[h * qt:(h + 1) * qt], k_tile) for h in range(hg)],
                            axis=0)
        return _softmax_step(s, bias, vs_ref[0, 0, pl.ds(k0, width), :], carry, hg, qt, online)

    def slc_body(i, carry):
        return slc_tile(aligned(i * kt, kt), kt, None, carry)

    carry = jnp.zeros((rows, LANES), jnp.float32)
    if online:
        carry = (jnp.full((rows, 1), M_INIT, jnp.float32), carry)
    n_full = t0 // kt
    if static:
        for i in range(n_full):
            carry = slc_body(i, carry)
    else:
        carry = lax.fori_loop(0, n_full // 2,
                              lambda j, c: slc_body(2 * j + 1, slc_body(2 * j, c)), carry)
        carry = lax.cond(n_full % 2 == 1, lambda c: slc_body(n_full - 1, c), lambda c: c, carry)
    k0 = aligned(n_full * kt, kt)
    last = t0 + qt - k0 if static else kt
    causal = jnp.where(k0 + lax.broadcasted_iota(jnp.int32, (1, last), 1) <= t_rows, 0.0, NEG_INF)
    acc_slc = slc_tile(k0, last, causal, carry)
    if online:
        acc_slc = acc_slc[1]

    gates = 1.0 / (1.0 + jnp.exp(-gate_ref[...]))
    spread = jnp.dot(gates.astype(BF16), bcast_ref[0],
                     preferred_element_type=jnp.float32)

    def pair(acc, j, normalise):
        a0 = acc[(2 * j) * qt:(2 * j + 1) * qt]
        a1 = acc[(2 * j + 1) * qt:(2 * j + 2) * qt]
        num = jnp.where(low_lanes, a0, swap_halves(a1))
        if not normalise:
            return num
        return num / jnp.where(low_lanes, swap_halves(a0), a1)

    for j in range(hg // PAIR):
        width = PAIR * HEAD_DIM
        o = jnp.zeros((qt, width), jnp.float32)
        for br, (acc, normalise) in enumerate(((o_cmp, False), (acc_slc, True), (acc_win, True))):
            col = (br * (hg // PAIR) + j) * width
            o = o + spread[:, col:col + width] * pair(acc, j, normalise)
        o_ref[:, j * width:(j + 1) * width] = o.astype(o_ref.dtype)


def _nsa_attn(q, gate, q_gain, k_gain, kc, vc, ks, vs, kw, vw, qt=256, kt=512):
    bt, d = q.shape
    nb, ng, seq, n = kw.shape
    kt = min(kt, seq)
    assert seq >= WINDOW + qt and seq % kt == 0 and kt % qt == 0
    nq = seq // qt
    hg = HEADS_PER_GROUP
    width = hg * n
    rows_c = kc.shape[2]
    col = jnp.arange(N_BRANCHES * width) // n
    src = (col // hg) * N_HEADS + jnp.arange(ng)[:, None] * hg + col % hg
    bcast = (jnp.arange(LANES)[None, :, None] == src[:, None, :]).astype(BF16)
    bounds = (1.01 * LOG2_E * n ** 0.5) * jnp.max(jnp.abs(q_gain)) * jnp.max(jnp.abs(k_gain), axis=1)
    out_sds = jax.ShapeDtypeStruct((bt, d), BF16)

    def specs(tile):
        pick = (lambda b, g, i: i) if tile is None else (lambda b, g: tile)
        at = lambda f: (lambda *ids: f(ids[0], ids[1], pick(*ids)))
        keys = seq if tile is None else (tile + 1) * qt
        cmp_rows = rows_c if tile is None else min(rows_c, -(-(keys // CMP_STRIDE) // LANES) * LANES)
        whole = at(lambda b, g, i: (b, g, 0, 0))
        q_blk = pl.BlockSpec((qt, width), at(lambda b, g, i: (b * nq + i, g)))
        in_specs = [pl.BlockSpec(memory_space=pltpu.SMEM), q_blk,
                    pl.BlockSpec((qt, LANES), at(lambda b, g, i: (b * nq + i, 0))),
                    pl.BlockSpec((1, LANES, N_BRANCHES * width), at(lambda b, g, i: (g, 0, 0))),
                    pl.BlockSpec((1, 1, cmp_rows, n), whole),
                    pl.BlockSpec((1, 1, cmp_rows, LANES), whole),
                    pl.BlockSpec((1, 1, keys, LANES), whole), pl.BlockSpec((1, 1, keys, LANES), whole),
                    pl.BlockSpec((1, 1, keys, n), whole), pl.BlockSpec((1, 1, keys, LANES), whole)]
        return in_specs, q_blk

    def generic(*args):
        in_specs, q_blk = specs(None)
        return pl.pallas_call(
            functools.partial(_nsa_attn_kernel, qt=qt, kt=kt, seq=seq, online=True, tile=None),
            grid=(nb, ng, nq), in_specs=in_specs, out_specs=q_blk, out_shape=out_sds,
            compiler_params=pltpu.CompilerParams(
                dimension_semantics=("parallel", "parallel", "arbitrary"),
                vmem_limit_bytes=VMEM_LIMIT),
        )(*args)

    def per_tile(*args):
        o = None
        for tile in range(nq):
            in_specs, q_blk = specs(tile)
            operands = list(args)
            if o is not None:
                in_specs.append(pl.BlockSpec(memory_space=pl.ANY))
                operands.append(o)
            o = pl.pallas_call(
                functools.partial(_nsa_attn_kernel, qt=qt, kt=kt, seq=seq, online=False, tile=tile),
                grid=(nb, ng), in_specs=in_specs, out_specs=q_blk, out_shape=out_sds,
                input_output_aliases={} if tile == 0 else {N_ATTN_INPUTS: 0},
                compiler_params=pltpu.CompilerParams(
                    dimension_semantics=("parallel", "parallel"), vmem_limit_bytes=VMEM_LIMIT),
            )(*operands)
        return o

    return lax.cond(jnp.max(bounds) <= MAX_STATIC_BOUND, per_tile, generic,
                    bounds.astype(jnp.float32), q, gate, bcast, kc, vc, ks, vs, kw, vw)


def _nsa_layer(x2d, nb, seq, g, w_in, q_gain, k_gain, cmp_pe, cmp_w1, cmp_b1, cmp_w2, w_out):
    q, kc_raw, vc_raw, ks, vs, kw, vw, z, gate = _nsa_in(x2d, nb, seq, g, w_in, q_gain, k_gain)
    kc, vc = _compress(kc_raw, vc_raw, cmp_pe, cmp_w1, cmp_b1, cmp_w2, k_gain)
    o = _nsa_attn(q, gate, q_gain, k_gain, kc, vc, ks, vs, kw, vw)
    return _out_proj(x2d, o, z, w_out)


def kernel(x, norm_g, rwkv_mu, rwkv_w_in, rwkv_w0, rwkv_w1, rwkv_w2, rwkv_a0, rwkv_a1, rwkv_a2, rwkv_k_k, rwkv_k_a, rwkv_r_k, rwkv_lnx_g, rwkv_lnx_b, rwkv_w_out, nsa_w_in, nsa_q_gain, nsa_k_gain, nsa_cmp_pe, nsa_cmp_w1, nsa_cmp_b1, nsa_cmp_w2, nsa_w_out):
    b, t, d = x.shape
    x2d = x.reshape(b * t, d)
    x2d = _rwkv_layer(x2d, t, norm_g[0], rwkv_mu[0], rwkv_w_in[0], rwkv_w0[0], rwkv_w1[0],
                      rwkv_w2[0], rwkv_a0[0], rwkv_a1[0], rwkv_a2[0], rwkv_k_k[0], rwkv_k_a[0],
                      rwkv_r_k[0].reshape(-1), rwkv_lnx_g[0], rwkv_lnx_b[0], rwkv_w_out[0])
    x2d = _nsa_layer(x2d, b, t, norm_g[1], nsa_w_in[0], nsa_q_gain[0], nsa_k_gain[0],
                     nsa_cmp_pe[0], nsa_cmp_w1[0], nsa_cmp_b1[0], nsa_cmp_w2[0], nsa_w_out[0])
    return x2d.reshape(b, t, d)
```

```python
import functools

import jax
import jax.numpy as jnp
from jax import lax
from jax.experimental import pallas as pl
from jax.experimental.pallas import tpu as pltpu

D_MODEL = 1024
HEAD_DIM = 64
N_HEADS = D_MODEL // HEAD_DIM
NORM_EPS = 1e-6
LN_X_EPS = 64e-5
DECAY_LORA = 64
ICLR_LORA = 64
N_KV_GROUPS = 4
HEADS_PER_GROUP = N_HEADS // N_KV_GROUPS
KV_WIDTH = N_KV_GROUPS * HEAD_DIM
N_BRANCHES = 3
CMP_BLOCK = 32
CMP_STRIDE = 16
CMP_HIDDEN = 256
SLC_BLOCK = 64
SLC_SHIFT = 6
SLC_TOPK = 16
N_LOCAL_BLOCKS = 2
WINDOW = 512
FORCE_BONUS = 1e4
NEG_INF = -1e30

LANES = 128
MXU_DEPTH = 256
PAIR = LANES // HEAD_DIM
N_PAIRS = N_HEADS // PAIR
HEAD_SHIFT = 6
WKV_HEADS = LANES // HEAD_DIM
CHUNK = 64
CHUNK_SHIFT = 6
VMEM_LIMIT = 48 * 1024 * 1024

BF16 = jnp.bfloat16


def _rms(x, eps=NORM_EPS):
    return x * lax.rsqrt(jnp.mean(x * x, axis=-1, keepdims=True) + eps)


def _bdot(a, b):
    return jnp.dot(a.astype(BF16), b.astype(BF16), preferred_element_type=jnp.float32)


def _bdot_nt(a, b):
    return lax.dot_general(a.astype(BF16), b.astype(BF16), (((1,), (1,)), ((), ())),
                           preferred_element_type=jnp.float32)


def _const_spec(shape):
    nd = len(shape)
    return pl.BlockSpec(shape, lambda *_: (0,) * nd, pipeline_mode=pl.Buffered(1))


def _rwkv_in_kernel(x_ref, xp_ref, g_ref, mu_ref, win_ref, w0_ref, w1_ref, w2_ref, a0_ref,
                    a1_ref, a2_ref, kk_ref, ka_ref,
                    r_out, lw_out, cum_out, k_out, v_out, kk_out, a_out, z_out, *, tiles_per_seq):
    d = D_MODEL
    g = g_ref[...]
    h = _rms(x_ref[...]) * g
    hp = _rms(xp_ref[7:8, :]) * g
    hp = jnp.where(pl.program_id(0) % tiles_per_seq == 0, 0.0, hp)
    row = lax.broadcasted_iota(jnp.int32, h.shape, 0)
    hs = jnp.where(row == 0, hp, pltpu.roll(h, 1, axis=0))
    dh = hs - h

    def mix(c):
        return h + dh * mu_ref[c:c + 1, :]

    r_out[...] = _bdot(mix(0), win_ref[:, 0 * d:1 * d]).astype(r_out.dtype)
    k = _bdot(mix(1), win_ref[:, 1 * d:2 * d])
    v_out[...] = _bdot(mix(2), win_ref[:, 2 * d:3 * d]).astype(v_out.dtype)
    z_out[...] = _bdot(mix(3), win_ref[:, 3 * d:4 * d]).astype(z_out.dtype)

    u = w0_ref[...] + _bdot(jnp.tanh(_bdot(mix(4), w1_ref[...])), w2_ref[...])
    sp = jnp.maximum(-u, 0.0) + jnp.log(1.0 + jnp.exp(-jnp.abs(u)))
    lw = -jnp.exp(-sp - 0.5)
    lw_out[...] = lw
    rows = min(lw.shape[0], MXU_DEPTH)
    ri = lax.broadcasted_iota(jnp.int32, (rows, rows), 0)
    ci = lax.broadcasted_iota(jnp.int32, (rows, rows), 1)
    same_chunk = jnp.right_shift(ri, CHUNK_SHIFT) == jnp.right_shift(ci, CHUNK_SHIFT)
    tril = jnp.where(ci <= ri, jnp.where(same_chunk, 1.0, 0.0), 0.0).astype(BF16)
    for r0 in range(0, lw.shape[0], rows):
        hi, mid, lo = _split3(lw[r0:r0 + rows])
        cum_out[r0:r0 + rows, :] = (jnp.dot(tril, hi, preferred_element_type=jnp.float32)
                                    + jnp.dot(tril, mid, preferred_element_type=jnp.float32)
                                    + jnp.dot(tril, lo, preferred_element_type=jnp.float32))
    ua = a0_ref[...] + _bdot(_bdot(mix(5), a1_ref[...]), a2_ref[...])
    a = 1.0 / (1.0 + jnp.exp(-ua))
    a_out[...] = a.astype(a_out.dtype)
    kk_out[...] = (k * kk_ref[...]).astype(kk_out.dtype)
    k_out[...] = (k * (1.0 + (a - 1.0) * ka_ref[...])).astype(k_out.dtype)


def _rwkv_in(x2d, seq, g, mu, w_in, w0, w1, w2, a0, a1, a2, k_k, k_a, tm=512):
    bt, d = x2d.shape
    tiles_per_seq = seq // tm
    row = lambda i: (i, 0)
    prev = lambda i: (jnp.maximum(i * (tm // 8) - 1, 0), 0)
    f32_sds = jax.ShapeDtypeStruct((bt, d), jnp.float32)
    b16_sds = jax.ShapeDtypeStruct((bt, d), BF16)
    return pl.pallas_call(
        functools.partial(_rwkv_in_kernel, tiles_per_seq=tiles_per_seq),
        grid=(bt // tm,),
        in_specs=[pl.BlockSpec((tm, d), row), pl.BlockSpec((8, d), prev),
                  _const_spec((1, d)), _const_spec((6, d)), _const_spec((d, 4 * d)),
                  _const_spec((1, d)), _const_spec((d, DECAY_LORA)), _const_spec((DECAY_LORA, d)),
                  _const_spec((1, d)), _const_spec((d, ICLR_LORA)), _const_spec((ICLR_LORA, d)),
                  _const_spec((1, d)), _const_spec((1, d))],
        out_specs=[pl.BlockSpec((tm, d), row)] * 8,
        out_shape=[b16_sds, f32_sds, f32_sds] + [b16_sds] * 5,
        compiler_params=pltpu.CompilerParams(dimension_semantics=("parallel",),
                                             vmem_limit_bytes=VMEM_LIMIT),
    )(x2d, x2d, g.reshape(1, d), mu, w_in.astype(BF16), w0.reshape(1, d), w1.astype(BF16),
      w2.astype(BF16), a0.reshape(1, d), a1.astype(BF16), a2.astype(BF16), k_k.reshape(1, d),
      k_a.reshape(1, d))


def _split3(x):
    hi = x.astype(BF16)
    r1 = x - hi.astype(x.dtype)
    mid = r1.astype(BF16)
    lo = (r1 - mid.astype(x.dtype)).astype(BF16)
    return hi, mid, lo


def _wkv_kernel(r_ref, lw_ref, cum_ref, k_ref, v_ref, kk_ref, a_ref, rk_ref, lng_ref, lnb_ref,
                o_ref, state_ref):
    @pl.when(pl.program_id(1) == 0)
    def _():
        state_ref[...] = jnp.zeros_like(state_ref)

    c = CHUNK
    nh = WKV_HEADS
    uw = nh * HEAD_DIM
    pairs = range(N_HEADS // nh)
    sls = [slice(p * uw, (p + 1) * uw) for p in pairs]
    m0 = lax.broadcasted_iota(jnp.int32, (c, LANES), 1) < HEAD_DIM
    lane_head = jnp.right_shift(lax.broadcasted_iota(jnp.int32, (c, uw), 1), HEAD_SHIFT)
    head_masks = [lane_head == h for h in range(nh)]

    def head_sum(x):
        out = []
        for j in range(uw // LANES):
            xj = x[:, j * LANES:(j + 1) * LANES]
            s0 = jnp.sum(jnp.where(m0, xj, 0.0), axis=-1, keepdims=True)
            s1 = jnp.sum(jnp.where(m0, 0.0, xj), axis=-1, keepdims=True)
            out.append(jnp.where(m0, s0, s1))
        return jnp.concatenate(out, axis=1)

    def stack(y):
        return jnp.concatenate([jnp.where(m, y, 0.0) for m in head_masks], axis=0)

    def each(f, *lists):
        return [f(*xs) for xs in zip(*lists)]

    n_rows, n_chunks = r_ref.shape[0], r_ref.shape[1] // c
    f32 = cum_ref.dtype
    w = nh * c
    row2 = lax.broadcasted_iota(jnp.int32, (c, w), 0)
    col2 = jnp.bitwise_and(lax.broadcasted_iota(jnp.int32, (c, w), 1), c - 1)
    strict = col2 < row2
    incl = col2 <= row2
    eye2 = jnp.where(col2 == row2, 1.0, 0.0)
    diag = (jnp.right_shift(lax.broadcasted_iota(jnp.int32, (uw, uw), 0), HEAD_SHIFT)
            == jnp.right_shift(lax.broadcasted_iota(jnp.int32, (uw, uw), 1), HEAD_SHIFT))

    def prep_stages(ci, d):
        items = [(bi, slice(ci * c, (ci + 1) * c), s) for bi in range(n_rows) for s in sls]
        d.update(items=items, r=[], k=[], v=[], cum=[], b=[], ar=[], kb_t=[])
        for i, it in enumerate(items):
            r, k, cum = r_ref[it].astype(f32), k_ref[it].astype(f32), cum_ref[it]
            kkraw = kk_ref[it].astype(f32)
            kk = kkraw / jnp.maximum(jnp.sqrt(head_sum(kkraw * kkraw)), 1e-12)
            b = kk * a_ref[it].astype(f32)
            g_inv = jnp.exp(-cum)
            at = -kk * jnp.exp(cum - lw_ref[it])
            d['ar'].append(jnp.concatenate([at, r * jnp.exp(cum)], axis=0))
            d['kb_t'].append(jnp.concatenate([stack(k * g_inv), stack(b * g_inv)], axis=0))
            for name, val in (('r', r), ('k', k), ('v', v_ref[it].astype(f32)), ('cum', cum),
                              ('b', b)):
                d[name].append(val)
            if i % 2:
                yield

    def head_stages(d):
        v = d['v']
        big = each(_bdot_nt, d['ar'], d['kb_t'])
        yield
        x = [jnp.where(strict, g[:c, w:], 0.0) for g in big]
        t = [eye2 + xi for xi in x]
        x = each(lambda xi: _bdot(xi, stack(xi)), x)
        yield
        for _ in range(4):
            tx = each(lambda ti, xi: _bdot(jnp.concatenate([ti, xi], axis=0), stack(xi)), t, x)
            t = each(lambda ti, r_: ti + r_[:c], t, tx)
            x = [r_[c:] for r_ in tx]
            yield
        t = each(lambda ti, xi: ti + _bdot(ti, stack(xi)), t, x)
        yield
        av = each(lambda g, vi: _bdot(
            jnp.concatenate([jnp.where(strict, g[:c, :w], 0.0),
                             jnp.where(incl, g[c:, :w], 0.0)], axis=0), stack(vi)), big, v)
        d.update(big=big, t=t, av=av)
        yield

    def tail_stages(d, box):
        state = box[0]
        ar_s = each(_bdot_nt, d['ar'], state)
        yield
        u = each(lambda ti, s, avi: _bdot(ti, stack(s[:c] + avi[:c])), d['t'], ar_s, d['av'])
        yield
        o = each(lambda g, s, avi, ui: s[c:] + avi[c:] + _bdot(
            jnp.where(incl, g[c:, w:], 0.0), stack(ui)), d['big'], ar_s, d['av'], u)
        new_state = []
        for st, ui, vi, ki, bi_, cumi in zip(state, u, d['v'], d['k'], d['b'], d['cum']):
            cum_last = cumi[c - 1:c, :]
            g_rem = jnp.exp(cum_last - cumi)
            vu = jnp.concatenate([vi, ui], axis=0)
            kb = jnp.concatenate([ki * g_rem, bi_ * g_rem], axis=0)
            upd = _bdot(vu.T, kb)
            new_state.append(st * jnp.exp(cum_last) + jnp.where(diag, upd, 0.0))
        box[0] = new_state
        yield
        for it, oi, ri, ki, vi in zip(d['items'], o, d['r'], d['k'], d['v']):
            s = it[2]
            mean = head_sum(oi) * (1.0 / HEAD_DIM)
            dev = oi - mean
            var = head_sum(dev * dev) * (1.0 / HEAD_DIM)
            y = dev * lax.rsqrt(var + LN_X_EPS) * lng_ref[:, s] + lnb_ref[:, s]
            o_ref[it] = (y + head_sum(ri * ki * rk_ref[:, s]) * vi).astype(o_ref.dtype)
        yield

    n_state = n_rows * len(sls)
    box = [[state_ref[j] for j in range(n_state)]]
    chunks = [dict() for _ in range(n_chunks)]
    for step in range(n_chunks + 2):
        live = {}
        if 0 < step <= n_chunks:
            live['head'] = head_stages(chunks[step - 1])
        if step < n_chunks:
            live['prep'] = prep_stages(step, chunks[step])
        if step > 1:
            live['tail'] = tail_stages(chunks[step - 2], box)
        n = 0
        while live:
            n += 1
            for name in ('head', 'prep', 'tail'):
                if name not in live or (name == 'tail' and n % 2 and len(live) > 1):
                    continue
                if next(live[name], StopIteration) is StopIteration:
                    del live[name]
    for j in range(n_state):
        state_ref[j] = box[0][j]


def _wkv(r, lw, cum, k, v, kkraw, a, seq, r_k, lnx_g, lnx_b, chunks_per_step=4, rows_per_step=2):
    bt, d = r.shape
    nb = bt // seq
    rows = chunks_per_step * CHUNK
    blk = pl.BlockSpec((rows_per_step, rows, d), lambda b, c: (b, c, 0))
    uw = WKV_HEADS * HEAD_DIM
    streams = [x.reshape(nb, seq, d) for x in (r, lw, cum, k, v, kkraw, a)]
    return pl.pallas_call(
        _wkv_kernel,
        grid=(nb // rows_per_step, seq // rows),
        in_specs=[blk] * 7 + [_const_spec((1, d))] * 3,
        out_specs=blk,
        out_shape=jax.ShapeDtypeStruct((nb, seq, d), BF16),
        scratch_shapes=[pltpu.VMEM((rows_per_step * (N_HEADS // WKV_HEADS), uw, uw), jnp.float32)],
        compiler_params=pltpu.CompilerParams(dimension_semantics=("parallel", "arbitrary"),
                                             vmem_limit_bytes=VMEM_LIMIT),
    )(*streams, r_k.reshape(1, d), lnx_g.reshape(1, d), lnx_b.reshape(1, d)).reshape(bt, d)


def _out_proj_kernel(x_ref, o_ref, z_ref, w_ref, y_ref):
    x = x_ref[...]
    z = z_ref[...].astype(x.dtype)
    y = o_ref[...].astype(x.dtype) * (z / (1.0 + jnp.exp(-z)))
    y_ref[...] = x + _bdot(y, w_ref[...])


def _out_proj(x2d, o, z, w_out, tm=1024):
    bt, d = x2d.shape
    row = lambda i: (i, 0)
    blk = pl.BlockSpec((tm, d), row)
    return pl.pallas_call(
        _out_proj_kernel,
        grid=(bt // tm,),
        in_specs=[blk, blk, blk, _const_spec((d, d))],
        out_specs=blk,
        out_shape=jax.ShapeDtypeStruct((bt, d), jnp.float32),
        compiler_params=pltpu.CompilerParams(dimension_semantics=("parallel",),
                                             vmem_limit_bytes=VMEM_LIMIT),
    )(x2d, o, z, w_out.astype(BF16))


def _rwkv_layer(x2d, seq, g, mu, w_in, w0, w1, w2, a0, a1, a2, k_k, k_a, r_k, lnx_g, lnx_b, w_out):
    r, lw, cum, k, v, kkraw, a, z = _rwkv_in(x2d, seq, g, mu, w_in, w0, w1, w2, a0, a1, a2, k_k,
                                             k_a)
    o = _wkv(r, lw, cum, k, v, kkraw, a, seq, r_k, lnx_g, lnx_b)
    return _out_proj(x2d, o, z, w_out)


def _nsa_in_kernel(x_ref, g_ref, w_ref, qg_ref, kg_ref, q_out, kc_out, vc_out, ks_out, vs_out,
                   kw_out, vw_out, z_out, gate_out, scr_ref, *, tiles_per_seq):
    d = D_MODEL
    h = (_rms(x_ref[...]) * g_ref[...]).astype(BF16)
    tm = h.shape[0]

    def proj(off, width):
        return jnp.dot(h, w_ref[:, off:off + width], preferred_element_type=jnp.float32)

    lane = lax.broadcasted_iota(jnp.int32, (tm, LANES), 1)
    low_lanes = lane < HEAD_DIM
    tok = (pl.program_id(0) % tiles_per_seq) * tm + lax.broadcasted_iota(jnp.int32, (tm, LANES), 0)
    blk_bias = jnp.where(lane - HEAD_DIM == jnp.right_shift(tok, SLC_SHIFT), 1.0, 0.0)

    def slot(p, g):
        two = p[:, (g // PAIR) * LANES:(g // PAIR + 1) * LANES]
        return pltpu.roll(two, HEAD_DIM, axis=1) if g % PAIR else two

    def narrow(ref, p, gain_row):
        for g in range(N_KV_GROUPS):
            seg = p[:, g * HEAD_DIM:(g + 1) * HEAD_DIM]
            if gain_row is not None:
                seg = _rms(seg) * kg_ref[gain_row:gain_row + 1, :HEAD_DIM]
            ref[0, g] = seg.astype(ref.dtype)

    def values(ref, p):
        for g in range(N_KV_GROUPS):
            ref[0, g] = jnp.where(low_lanes, slot(p, g), 1.0).astype(ref.dtype)

    p = proj(0, d)
    for j in range(N_PAIRS):
        x = p[:, j * LANES:(j + 1) * LANES]
        sq = x * x
        ms = jnp.where(low_lanes, jnp.sum(jnp.where(low_lanes, sq, 0.0), axis=-1, keepdims=True),
                       jnp.sum(jnp.where(low_lanes, 0.0, sq), axis=-1, keepdims=True))
        q_out[:, j * LANES:(j + 1) * LANES] = (
            x * lax.rsqrt(ms * (1.0 / HEAD_DIM) + NORM_EPS) * qg_ref[...]).astype(q_out.dtype)
    def flattened(ref, p):
        for j in range(KV_WIDTH // LANES):
            scr_ref[j] = p[:, j * LANES:(j + 1) * LANES]
        n_rows = tm // CMP_STRIDE
        low = lax.broadcasted_iota(jnp.int32, (n_rows, LANES), 1) < HEAD_DIM
        for m in range(CMP_STRIDE // PAIR):
            first = [scr_ref[j, pl.ds(PAIR * m, n_rows, stride=CMP_STRIDE), :]
                     for j in range(KV_WIDTH // LANES)]
            second = [scr_ref[j, pl.ds(PAIR * m + 1, n_rows, stride=CMP_STRIDE), :]
                      for j in range(KV_WIDTH // LANES)]
            for g in range(N_KV_GROUPS):
                a, b = first[g // PAIR], second[g // PAIR]
                a = pltpu.roll(a, HEAD_DIM, axis=1) if g % PAIR else a
                b = b if g % PAIR else pltpu.roll(b, HEAD_DIM, axis=1)
                ref[0, g, :, m * LANES:(m + 1) * LANES] = jnp.where(low, a, b).astype(ref.dtype)

    flattened(kc_out, proj(d, KV_WIDTH))
    flattened(vc_out, proj(d + KV_WIDTH, KV_WIDTH))
    p = proj(d + 2 * KV_WIDTH, KV_WIDTH)
    for g in range(N_KV_GROUPS):
        x = jnp.where(low_lanes, slot(p, g), 0.0)
        ms = jnp.sum(x * x, axis=-1, keepdims=True) * (1.0 / HEAD_DIM)
        ks_out[0, g] = (x * lax.rsqrt(ms + NORM_EPS) * kg_ref[1:2, :] + blk_bias).astype(BF16)
    values(vs_out, proj(d + 3 * KV_WIDTH, KV_WIDTH))
    narrow(kw_out, proj(d + 4 * KV_WIDTH, KV_WIDTH), 2)
    values(vw_out, proj(d + 5 * KV_WIDTH, KV_WIDTH))
    z_out[...] = proj(d + 6 * KV_WIDTH, d).astype(z_out.dtype)
    gate_out[...] = proj(2 * d + 6 * KV_WIDTH, LANES)


def _nsa_in(x2d, nb, seq, g, w_in, q_gain, k_gain, tm=256):
    bt, d = x2d.shape
    tps = seq // tm
    width = w_in.shape[1]
    padded = d + 6 * KV_WIDTH + d + LANES
    w = jnp.pad(w_in, ((0, 0), (0, padded - width))).astype(BF16)
    row = lambda i: (i, 0)
    grp = lambda i: (i // tps, 0, i % tps, 0)
    kv_blk = pl.BlockSpec((1, N_KV_GROUPS, tm, HEAD_DIM), grp)
    v_blk = pl.BlockSpec((1, N_KV_GROUPS, tm, LANES), grp)
    half = CMP_STRIDE * HEAD_DIM
    u_blk = pl.BlockSpec((1, N_KV_GROUPS, tm // CMP_STRIDE, half), grp)
    u_b16 = jax.ShapeDtypeStruct((nb, N_KV_GROUPS, seq // CMP_STRIDE, half), BF16)
    k_b16 = jax.ShapeDtypeStruct((nb, N_KV_GROUPS, seq, HEAD_DIM), BF16)
    v_b16 = jax.ShapeDtypeStruct((nb, N_KV_GROUPS, seq, LANES), BF16)
    wide = jax.ShapeDtypeStruct((bt, d), BF16)
    assert seq // SLC_BLOCK <= HEAD_DIM
    qg = jnp.tile(q_gain * (LOG2_E * HEAD_DIM ** -0.5), PAIR).reshape(1, LANES)
    return pl.pallas_call(
        functools.partial(_nsa_in_kernel, tiles_per_seq=tps),
        grid=(bt // tm,),
        in_specs=[pl.BlockSpec((tm, d), row), _const_spec((1, d)), _const_spec((d, padded)),
                  _const_spec((1, LANES)), _const_spec((N_BRANCHES, LANES))],
        out_specs=[pl.BlockSpec((tm, d), row), u_blk, u_blk, v_blk, v_blk, kv_blk, v_blk,
                   pl.BlockSpec((tm, d), row), pl.BlockSpec((tm, LANES), row)],
        out_shape=[wide, u_b16, u_b16, v_b16, v_b16, k_b16, v_b16, wide,
                   jax.ShapeDtypeStruct((bt, LANES), jnp.float32)],
        scratch_shapes=[pltpu.VMEM((KV_WIDTH // LANES, tm, LANES), jnp.float32)],
        compiler_params=pltpu.CompilerParams(dimension_semantics=("parallel",),
                                             vmem_limit_bytes=VMEM_LIMIT),
    )(x2d, g.reshape(1, d), w, qg, jnp.pad(k_gain, ((0, 0), (0, LANES - HEAD_DIM))))


def _gelu_tanh(x):
    return 0.5 * x * (1.0 + jnp.tanh(0.7978845608028654 * (x + 0.044715 * (x * x * x))))


def _compress_kernel(uk_ref, uv_ref, pe_ref, w1_ref, b1_ref, w2_ref, kg_ref, kc_out, vc_out):
    half = CMP_STRIDE * HEAD_DIM
    for kv, (u_ref, out) in enumerate(((uk_ref, kc_out), (uv_ref, vc_out))):
        uh = u_ref[0, 0]
        rows = uh.shape[0]
        ha = _bdot(uh + pe_ref[kv, 0:1, :], w1_ref[kv, :half, :])
        hb = _bdot(uh + pe_ref[kv, 1:2, :], w1_ref[kv, half:, :])
        hid = _gelu_tanh(ha + pltpu.roll(hb, rows - 1, axis=0) + b1_ref[kv])
        y = _bdot(hid, w2_ref[kv])
        if kv == 0:
            y = y[:, :HEAD_DIM]
            out[0, 0] = (_rms(y) * kg_ref[0:1, :]).astype(out.dtype)
        else:
            low_lanes = lax.broadcasted_iota(jnp.int32, y.shape, 1) < HEAD_DIM
            out[0, 0] = jnp.where(low_lanes, y, 1.0).astype(out.dtype)


def _compress(kc_raw, vc_raw, pe, w1, b1, w2, k_gain):
    nb, ng, rows, half = kc_raw.shape
    n = half // CMP_STRIDE
    uk, uv = kc_raw, vc_raw
    whole = lambda b, g: (b, g, 0, 0)
    u_blk = pl.BlockSpec((1, 1, rows, half), whole)
    w2p = jnp.pad(w2, ((0, 0), (0, 0), (0, LANES - n))).astype(BF16)
    return pl.pallas_call(
        _compress_kernel,
        grid=(nb, ng),
        in_specs=[u_blk, u_blk, _const_spec((2, 2, half)), _const_spec((2, 2 * half, CMP_HIDDEN)),
                  _const_spec((2, 1, CMP_HIDDEN)), _const_spec((2, CMP_HIDDEN, LANES)),
                  _const_spec((N_BRANCHES, n))],
        out_specs=[pl.BlockSpec((1, 1, rows, n), whole), pl.BlockSpec((1, 1, rows, LANES), whole)],
        out_shape=[jax.ShapeDtypeStruct((nb, ng, rows, n), BF16),
                   jax.ShapeDtypeStruct((nb, ng, rows, LANES), BF16)],
        compiler_params=pltpu.CompilerParams(dimension_semantics=("parallel", "parallel"),
                                             vmem_limit_bytes=VMEM_LIMIT),
    )(uk, uv, pe.reshape(2, 2, half), w1.astype(BF16), b1.reshape(2, 1, CMP_HIDDEN), w2p, k_gain)


M_INIT = -1e20


LOG2_E = 1.4426950408889634


MAX_STATIC_BOUND = 50.0


def _softmax_step(s, bias, v_tile, carry, hg, qt, online):
    sb = [s[h * qt:(h + 1) * qt] for h in range(hg)]
    if bias is not None:
        sb = [x + bias for x in sb]
    if not online:
        return jnp.concatenate(
            [carry[h * qt:(h + 1) * qt] + jnp.dot(jnp.exp2(sb[h]).astype(BF16), v_tile,
                                                   preferred_element_type=jnp.float32)
             for h in range(hg)], axis=0)
    m, acc = carry
    m_new = jnp.maximum(m, jnp.concatenate(
        [jnp.max(x, axis=-1, keepdims=True) for x in sb], axis=0))
    p = jnp.concatenate(
        [jnp.exp2(sb[h] - m_new[h * qt:(h + 1) * qt]).astype(BF16) for h in range(hg)], axis=0)
    acc = jnp.exp2(m - m_new) * acc + jnp.dot(p, v_tile, preferred_element_type=jnp.float32)
    return m_new, acc


N_ATTN_INPUTS = 10


def _nsa_attn_kernel(*refs, qt, kt, seq, online, tile):
    t0 = pl.program_id(2) * qt if tile is None else tile * qt
    _nsa_attn_tile(*refs[:N_ATTN_INPUTS], refs[-1], t0=t0, qt=qt, kt=kt, seq=seq, online=online)


def _nsa_attn_tile(bound_ref, q_ref, gate_ref, bcast_ref, kc_ref, vc_ref, ks_ref, vs_ref,
                   kw_ref, vw_ref, o_ref, *, t0, qt, kt, seq, online):
    hg = HEADS_PER_GROUP
    static = isinstance(t0, int)
    n_slc = seq // SLC_BLOCK
    n_cmp_rows = kc_ref.shape[2]
    if static:
        n_cmp_rows = min(n_cmp_rows, -(-((t0 + qt) // CMP_STRIDE) // LANES) * LANES)
    rows = hg * qt

    def aligned(x, m):
        return x if isinstance(x, int) else pl.multiple_of(x, m)
    low_lanes = lax.broadcasted_iota(jnp.int32, (qt, LANES), 1) < HEAD_DIM
    keep_c, keep_s, keep_w = (0.0, 0.0, 0.0) if online else (
        -bound_ref[0], -bound_ref[1], -bound_ref[2])

    def swap_halves(x):
        return pltpu.roll(x, HEAD_DIM, axis=1)

    q = q_ref[...].astype(jnp.float32)
    q_n = []
    for h in range(hg):
        two = q[:, (h // PAIR) * LANES:(h // PAIR + 1) * LANES]
        q_n.append(jnp.where(low_lanes, swap_halves(two) if h % PAIR else two, 0.0))
    qs = jnp.concatenate([x[:, :HEAD_DIM] for x in q_n], axis=0).astype(BF16)
    t_rows = t0 + lax.broadcasted_iota(jnp.int32, (qt, 1), 0)

    w0 = max(t0 - WINDOW, 0) if static else pl.multiple_of(jnp.maximum(t0 - WINDOW, 0), qt)
    span = t0 + qt - w0 if static else WINDOW + qt
    s = _bdot_nt(qs, kc_ref[0, 0, :n_cmp_rows, :])
    s_win = _bdot_nt(qs, kw_ref[0, 0, pl.ds(w0, span), :])

    cmp_end = lax.broadcasted_iota(jnp.int32, (1, n_cmp_rows), 1) * CMP_STRIDE + (CMP_BLOCK - 1)
    cbias = jnp.where(cmp_end <= t_rows, keep_c, NEG_INF)
    p_heads = []
    for h in range(hg):
        sb = s[h * qt:(h + 1) * qt] + cbias
        if online:
            sb = sb - jnp.maximum(jnp.max(sb, axis=-1, keepdims=True), M_INIT)
        e = jnp.exp2(sb)
        p_heads.append(e / jnp.maximum(jnp.sum(e, axis=-1, keepdims=True), 1e-37))
    o_cmp = jnp.dot(jnp.concatenate([p.astype(BF16) for p in p_heads], axis=0),
                    vc_ref[0, 0, :n_cmp_rows, :], preferred_element_type=jnp.float32)

    p_sum = p_heads[0]
    for h in range(1, hg):
        p_sum = p_sum + p_heads[h]
    jn = lax.broadcasted_iota(jnp.int32, (n_slc, n_cmp_rows), 0) * SLC_BLOCK
    nn = lax.broadcasted_iota(jnp.int32, (n_slc, n_cmp_rows), 1) * CMP_STRIDE
    ov_t = jnp.where((nn < jn + SLC_BLOCK) & (nn + (CMP_BLOCK - 1) >= jn), 1.0, 0.0).astype(BF16)
    p_hi = p_sum.astype(BF16)
    p_lo = (p_sum - p_hi.astype(p_sum.dtype)).astype(BF16)
    imp_t = (lax.dot_general(ov_t, p_hi, (((1,), (1,)), ((), ())), preferred_element_type=jnp.float32)
             + lax.dot_general(ov_t, p_lo, (((1,), (1,)), ((), ())), preferred_element_type=jnp.float32))

    jb = lax.broadcasted_iota(jnp.int32, (n_slc, qt), 0)
    tq = t0 + lax.broadcasted_iota(jnp.int32, (n_slc, qt), 1)
    dist = jnp.right_shift(tq, SLC_SHIFT) - jb
    forced = (jb == 0) | ((dist >= 0) & (dist < N_LOCAL_BLOCKS))
    score = jnp.where(dist >= 0, imp_t + jnp.where(forced, FORCE_BONUS, 0.0), -jnp.inf)
    sub = 8
    n_cand = (t0 + qt - 1) // SLC_BLOCK + 1 if static else n_slc
    if n_cand <= SLC_TOPK:
        n_cand = 0
    groups = [score[v * sub:(v + 1) * sub] for v in range(n_slc // sub)]
    jrow = lax.broadcasted_iota(jnp.int32, (sub, qt), 0)
    cnt = [jnp.zeros((sub, qt), jnp.float32) for _ in groups]
    for j2 in range(n_cand):
        row = score[j2:j2 + 1, :]
        for v, sv in enumerate(groups[:-(-n_cand // sub)]):
            if v * sub > j2:
                inc = jnp.where(row >= sv, 1.0, 0.0)
            elif (v + 1) * sub - 1 <= j2:
                inc = jnp.where(row > sv, 1.0, 0.0)
            else:
                inc = jnp.where(jrow + v * sub > j2, jnp.where(row >= sv, 1.0, 0.0),
                                jnp.where(row > sv, 1.0, 0.0))
            cnt[v] = cnt[v] + inc
    cnt = jnp.concatenate(cnt, axis=0)
    blk_bias_t = jnp.where(dist >= 0, jnp.where(cnt < min(SLC_TOPK, n_slc), keep_s, NEG_INF),
                           NEG_INF)
    parts = [jnp.zeros((HEAD_DIM, qt), jnp.float32), blk_bias_t]
    if n_slc < HEAD_DIM:
        parts.append(jnp.zeros((HEAD_DIM - n_slc, qt), jnp.float32))
    blk_bias = jnp.concatenate(parts, axis=0).T
    q_aug = jnp.concatenate([x + blk_bias for x in q_n], axis=0).astype(BF16)

    lag = t_rows - (w0 + lax.broadcasted_iota(jnp.int32, (1, span), 1))
    wbias = jnp.where(lag >= 0, jnp.where(lag < WINDOW, keep_w, NEG_INF), NEG_INF)
    v_win = vw_ref[0, 0, pl.ds(w0, span), :]
    acc_win = []
    for h in range(hg):
        sb = s_win[h * qt:(h + 1) * qt] + wbias
        if online:
            sb = sb - jnp.max(sb, axis=-1, keepdims=True)
        acc_win.append(jnp.dot(jnp.exp2(sb).astype(BF16), v_win,
                               preferred_element_type=jnp.float32))
    acc_win = jnp.concatenate(acc_win, axis=0)

    def slc_tile(k0, width, bias, carry):
        k_tile = ks_ref[0, 0, pl.ds(k0, width), :]
        s = jnp.concatenate([_bdot_nt(q_aug[h * qt:(h + 1) * qt], k_tile) for h in range(hg)],
                            axis=0)
        return _softmax_step(s, bias, vs_ref[0, 0, pl.ds(k0, width), :], carry, hg, qt, online)

    def slc_body(i, carry):
        return slc_tile(aligned(i * kt, kt), kt, None, carry)

    carry = jnp.zeros((rows, LANES), jnp.float32)
    if online:
        carry = (jnp.full((rows, 1), M_INIT, jnp.float32), carry)
    n_full = t0 // kt
    if static:
        for i in range(n_full):
            carry = slc_body(i, carry)
    else:
        carry = lax.fori_loop(0, n_full // 2,
                              lambda j, c: slc_body(2 * j + 1, slc_body(2 * j, c)), carry)
        carry = lax.cond(n_full % 2 == 1, lambda c: slc_body(n_full - 1, c), lambda c: c, carry)
    k0 = aligned(n_full * kt, kt)
    last = t0 + qt - k0 if static else kt
    causal = jnp.where(k0 + lax.broadcasted_iota(jnp.int32, (1, last), 1) <= t_rows, 0.0, NEG_INF)
    acc_slc = slc_tile(k0, last, causal, carry)
    if online:
        acc_slc = acc_slc[1]

    gates = 1.0 / (1.0 + jnp.exp(-gate_ref[...]))
    spread = jnp.dot(gates.astype(BF16), bcast_ref[0],
                     preferred_element_type=jnp.float32)

    def pair(acc, j, normalise):
        a0 = acc[(2 * j) * qt:(2 * j + 1) * qt]
        a1 = acc[(2 * j + 1) * qt:(2 * j + 2) * qt]
        num = jnp.where(low_lanes, a0, swap_halves(a1))
        if not normalise:
            return num
        return num / jnp.where(low_lanes, swap_halves(a0), a1)

    for j in range(hg // PAIR):
        width = PAIR * HEAD_DIM
        o = jnp.zeros((qt, width), jnp.float32)
        for br, (acc, normalise) in enumerate(((o_cmp, False), (acc_slc, True), (acc_win, True))):
            col = (br * (hg // PAIR) + j) * width
            o = o + spread[:, col:col + width] * pair(acc, j, normalise)
        o_ref[:, j * width:(j + 1) * width] = o.astype(o_ref.dtype)


def _nsa_attn(q, gate, q_gain, k_gain, kc, vc, ks, vs, kw, vw, qt=256, kt=512):
    bt, d = q.shape
    nb, ng, seq, n = kw.shape
    kt = min(kt, seq)
    assert seq >= WINDOW + qt and seq % kt == 0 and kt % qt == 0
    nq = seq // qt
    hg = HEADS_PER_GROUP
    width = hg * n
    rows_c = kc.shape[2]
    col = jnp.arange(N_BRANCHES * width) // n
    src = (col // hg) * N_HEADS + jnp.arange(ng)[:, None] * hg + col % hg
    bcast = (jnp.arange(LANES)[None, :, None] == src[:, None, :]).astype(BF16)
    bounds = (1.01 * LOG2_E * n ** 0.5) * jnp.max(jnp.abs(q_gain)) * jnp.max(jnp.abs(k_gain), axis=1)
    out_sds = jax.ShapeDtypeStruct((bt, d), BF16)

    def specs(tile):
        pick = (lambda b, g, i: i) if tile is None else (lambda b, g: tile)
        at = lambda f: (lambda *ids: f(ids[0], ids[1], pick(*ids)))
        keys = seq if tile is None else (tile + 1) * qt
        cmp_rows = rows_c if tile is None else min(rows_c, -(-(keys // CMP_STRIDE) // LANES) * LANES)
        whole = at(lambda b, g, i: (b, g, 0, 0))
        q_blk = pl.BlockSpec((qt, width), at(lambda b, g, i: (b * nq + i, g)))
        in_specs = [pl.BlockSpec(memory_space=pltpu.SMEM), q_blk,
                    pl.BlockSpec((qt, LANES), at(lambda b, g, i: (b * nq + i, 0))),
                    pl.BlockSpec((1, LANES, N_BRANCHES * width), at(lambda b, g, i: (g, 0, 0))),
                    pl.BlockSpec((1, 1, cmp_rows, n), whole),
                    pl.BlockSpec((1, 1, cmp_rows, LANES), whole),
                    pl.BlockSpec((1, 1, keys, LANES), whole), pl.BlockSpec((1, 1, keys, LANES), whole),
                    pl.BlockSpec((1, 1, keys, n), whole), pl.BlockSpec((1, 1, keys, LANES), whole)]
        return in_specs, q_blk

    def generic(*args):
        in_specs, q_blk = specs(None)
        return pl.pallas_call(
            functools.partial(_nsa_attn_kernel, qt=qt, kt=kt, seq=seq, online=True, tile=None),
            grid=(nb, ng, nq), in_specs=in_specs, out_specs=q_blk, out_shape=out_sds,
            compiler_params=pltpu.CompilerParams(
                dimension_semantics=("parallel", "parallel", "arbitrary"),
                vmem_limit_bytes=VMEM_LIMIT),
        )(*args)

    def per_tile(*args):
        o = None
        for tile in range(nq):
            in_specs, q_blk = specs(tile)
            operands = list(args)
            if o is not None:
                in_specs.append(pl.BlockSpec(memory_space=pl.ANY))
                operands.append(o)
            o = pl.pallas_call(
                functools.partial(_nsa_attn_kernel, qt=qt, kt=kt, seq=seq, online=False, tile=tile),
                grid=(nb, ng), in_specs=in_specs, out_specs=q_blk, out_shape=out_sds,
                input_output_aliases={} if tile == 0 else {N_ATTN_INPUTS: 0},
                compiler_params=pltpu.CompilerParams(
                    dimension_semantics=("parallel", "parallel"), vmem_limit_bytes=VMEM_LIMIT),
            )(*operands)
        return o

    return lax.cond(jnp.max(bounds) <= MAX_STATIC_BOUND, per_tile, generic,
                    bounds.astype(jnp.float32), q, gate, bcast, kc, vc, ks, vs, kw, vw)


def _nsa_layer(x2d, nb, seq, g, w_in, q_gain, k_gain, cmp_pe, cmp_w1, cmp_b1, cmp_w2, w_out):
    q, kc_raw, vc_raw, ks, vs, kw, vw, z, gate = _nsa_in(x2d, nb, seq, g, w_in, q_gain, k_gain)
    kc, vc = _compress(kc_raw, vc_raw, cmp_pe, cmp_w1, cmp_b1, cmp_w2, k_gain)
    o = _nsa_attn(q, gate, q_gain, k_gain, kc, vc, ks, vs, kw, vw)
    return _out_proj(x2d, o, z, w_out)


def kernel(x, norm_g, rwkv_mu, rwkv_w_in, rwkv_w0, rwkv_w1, rwkv_w2, rwkv_a0, rwkv_a1, rwkv_a2, rwkv_k_k, rwkv_k_a, rwkv_r_k, rwkv_lnx_g, rwkv_lnx_b, rwkv_w_out, nsa_w_in, nsa_q_gain, nsa_k_gain, nsa_cmp_pe, nsa_cmp_w1, nsa_cmp_b1, nsa_cmp_w2, nsa_w_out):
    b, t, d = x.shape
    x2d = x.reshape(b * t, d)
    x2d = _rwkv_layer(x2d, t, norm_g[0], rwkv_mu[0], rwkv_w_in[0], rwkv_w0[0], rwkv_w1[0],
                      rwkv_w2[0], rwkv_a0[0], rwkv_a1[0], rwkv_a2[0], rwkv_k_k[0], rwkv_k_a[0],
                      rwkv_r_k[0].reshape(-1), rwkv_lnx_g[0], rwkv_lnx_b[0], rwkv_w_out[0])
    x2d = _nsa_layer(x2d, b, t, norm_g[1], nsa_w_in[0], nsa_q_gain[0], nsa_k_gain[0],
                     nsa_cmp_pe[0], nsa_cmp_w1[0], nsa_cmp_b1[0], nsa_cmp_w2[0], nsa_w_out[0])
    return x2d.reshape(b, t, d)
```

```python
import functools

import jax
import jax.numpy as jnp
from jax import lax
from jax.experimental import pallas as pl
from jax.experimental.pallas import tpu as pltpu

D_MODEL = 1024
HEAD_DIM = 64
N_HEADS = D_MODEL // HEAD_DIM
NORM_EPS = 1e-6
LN_X_EPS = 64e-5
DECAY_LORA = 64
ICLR_LORA = 64
N_KV_GROUPS = 4
HEADS_PER_GROUP = N_HEADS // N_KV_GROUPS
KV_WIDTH = N_KV_GROUPS * HEAD_DIM
N_BRANCHES = 3
CMP_BLOCK = 32
CMP_STRIDE = 16
CMP_HIDDEN = 256
SLC_BLOCK = 64
SLC_SHIFT = 6
SLC_TOPK = 16
N_LOCAL_BLOCKS = 2
WINDOW = 512
FORCE_BONUS = 1e4
NEG_INF = -1e30

LANES = 128
MXU_DEPTH = 256
PAIR = LANES // HEAD_DIM
N_PAIRS = N_HEADS // PAIR
HEAD_SHIFT = 6
WKV_HEADS = LANES // HEAD_DIM
CHUNK = 64
CHUNK_SHIFT = 6
VMEM_LIMIT = 48 * 1024 * 1024

BF16 = jnp.bfloat16


def _rms(x, eps=NORM_EPS):
    return x * lax.rsqrt(jnp.mean(x * x, axis=-1, keepdims=True) + eps)


def _bdot(a, b):
    return jnp.dot(a.astype(BF16), b.astype(BF16), preferred_element_type=jnp.float32)


def _bdot_nt(a, b):
    return lax.dot_general(a.astype(BF16), b.astype(BF16), (((1,), (1,)), ((), ())),
                           preferred_element_type=jnp.float32)


def _const_spec(shape):
    nd = len(shape)
    return pl.BlockSpec(shape, lambda *_: (0,) * nd, pipeline_mode=pl.Buffered(1))


def _rwkv_in_kernel(x_ref, xp_ref, g_ref, mu_ref, win_ref, w0_ref, w1_ref, w2_ref, a0_ref,
                    a1_ref, a2_ref, kk_ref, ka_ref,
                    r_out, lw_out, cum_out, k_out, v_out, kk_out, a_out, z_out, *, tiles_per_seq):
    d = D_MODEL
    g = g_ref[...]
    h = _rms(x_ref[...]) * g
    hp = _rms(xp_ref[7:8, :]) * g
    hp = jnp.where(pl.program_id(0) % tiles_per_seq == 0, 0.0, hp)
    row = lax.broadcasted_iota(jnp.int32, h.shape, 0)
    hs = jnp.where(row == 0, hp, pltpu.roll(h, 1, axis=0))
    dh = hs - h

    def mix(c):
        return h + dh * mu_ref[c:c + 1, :]

    r_out[...] = _bdot(mix(0), win_ref[:, 0 * d:1 * d]).astype(r_out.dtype)
    k = _bdot(mix(1), win_ref[:, 1 * d:2 * d])
    v_out[...] = _bdot(mix(2), win_ref[:, 2 * d:3 * d]).astype(v_out.dtype)
    z_out[...] = _bdot(mix(3), win_ref[:, 3 * d:4 * d]).astype(z_out.dtype)

    u = w0_ref[...] + _bdot(jnp.tanh(_bdot(mix(4), w1_ref[...])), w2_ref[...])
    sp = jnp.maximum(-u, 0.0) + jnp.log(1.0 + jnp.exp(-jnp.abs(u)))
    lw = -jnp.exp(-sp - 0.5)
    lw_out[...] = lw
    rows = min(lw.shape[0], MXU_DEPTH)
    ri = lax.broadcasted_iota(jnp.int32, (rows, rows), 0)
    ci = lax.broadcasted_iota(jnp.int32, (rows, rows), 1)
    same_chunk = jnp.right_shift(ri, CHUNK_SHIFT) == jnp.right_shift(ci, CHUNK_SHIFT)
    tril = jnp.where(ci <= ri, jnp.where(same_chunk, 1.0, 0.0), 0.0).astype(BF16)
    for r0 in range(0, lw.shape[0], rows):
        hi, lo = _split2(lw[r0:r0 + rows])
        cum_out[r0:r0 + rows, :] = (jnp.dot(tril, hi, preferred_element_type=jnp.float32)
                                    + jnp.dot(tril, lo, preferred_element_type=jnp.float32))
    ua = a0_ref[...] + _bdot(_bdot(mix(5), a1_ref[...]), a2_ref[...])
    a = 1.0 / (1.0 + jnp.exp(-ua))
    a_out[...] = a.astype(a_out.dtype)
    kk_out[...] = (k * kk_ref[...]).astype(kk_out.dtype)
    k_out[...] = (k * (1.0 + (a - 1.0) * ka_ref[...])).astype(k_out.dtype)


def _rwkv_in(x2d, seq, g, mu, w_in, w0, w1, w2, a0, a1, a2, k_k, k_a, tm=512):
    bt, d = x2d.shape
    tiles_per_seq = seq // tm
    row = lambda i: (i, 0)
    prev = lambda i: (jnp.maximum(i * (tm // 8) - 1, 0), 0)
    f32_sds = jax.ShapeDtypeStruct((bt, d), jnp.float32)
    b16_sds = jax.ShapeDtypeStruct((bt, d), BF16)
    return pl.pallas_call(
        functools.partial(_rwkv_in_kernel, tiles_per_seq=tiles_per_seq),
        grid=(bt // tm,),
        in_specs=[pl.BlockSpec((tm, d), row), pl.BlockSpec((8, d), prev),
                  _const_spec((1, d)), _const_spec((6, d)), _const_spec((d, 4 * d)),
                  _const_spec((1, d)), _const_spec((d, DECAY_LORA)), _const_spec((DECAY_LORA, d)),
                  _const_spec((1, d)), _const_spec((d, ICLR_LORA)), _const_spec((ICLR_LORA, d)),
                  _const_spec((1, d)), _const_spec((1, d))],
        out_specs=[pl.BlockSpec((tm, d), row)] * 8,
        out_shape=[b16_sds, f32_sds, f32_sds] + [b16_sds] * 5,
        compiler_params=pltpu.CompilerParams(dimension_semantics=("parallel",),
                                             vmem_limit_bytes=VMEM_LIMIT),
    )(x2d, x2d, g.reshape(1, d), mu, w_in.astype(BF16), w0.reshape(1, d), w1.astype(BF16),
      w2.astype(BF16), a0.reshape(1, d), a1.astype(BF16), a2.astype(BF16), k_k.reshape(1, d),
      k_a.reshape(1, d))


def _split2(x):
    hi = x.astype(BF16)
    return hi, (x - hi.astype(x.dtype)).astype(BF16)


def _wkv_kernel(r_ref, lw_ref, cum_ref, k_ref, v_ref, kk_ref, a_ref, rk_ref, lng_ref, lnb_ref,
                o_ref, state_ref):
    @pl.when(pl.program_id(1) == 0)
    def _():
        state_ref[...] = jnp.zeros_like(state_ref)

    c = CHUNK
    nh = WKV_HEADS
    uw = nh * HEAD_DIM
    pairs = range(N_HEADS // nh)
    sls = [slice(p * uw, (p + 1) * uw) for p in pairs]
    m0 = lax.broadcasted_iota(jnp.int32, (c, LANES), 1) < HEAD_DIM
    lane_head = jnp.right_shift(lax.broadcasted_iota(jnp.int32, (c, uw), 1), HEAD_SHIFT)
    head_masks = [lane_head == h for h in range(nh)]

    def head_sum(x):
        out = []
        for j in range(uw // LANES):
            xj = x[:, j * LANES:(j + 1) * LANES]
            s0 = jnp.sum(jnp.where(m0, xj, 0.0), axis=-1, keepdims=True)
            s1 = jnp.sum(jnp.where(m0, 0.0, xj), axis=-1, keepdims=True)
            out.append(jnp.where(m0, s0, s1))
        return jnp.concatenate(out, axis=1)

    def stack(y):
        return jnp.concatenate([jnp.where(m, y, 0.0) for m in head_masks], axis=0)

    def each(f, *lists):
        return [f(*xs) for xs in zip(*lists)]

    n_rows, n_chunks = r_ref.shape[0], r_ref.shape[1] // c
    f32 = cum_ref.dtype
    w = nh * c
    row2 = lax.broadcasted_iota(jnp.int32, (c, w), 0)
    col2 = jnp.bitwise_and(lax.broadcasted_iota(jnp.int32, (c, w), 1), c - 1)
    strict = col2 < row2
    incl = col2 <= row2
    eye2 = jnp.where(col2 == row2, 1.0, 0.0)
    diag = (jnp.right_shift(lax.broadcasted_iota(jnp.int32, (uw, uw), 0), HEAD_SHIFT)
            == jnp.right_shift(lax.broadcasted_iota(jnp.int32, (uw, uw), 1), HEAD_SHIFT))

    def prep_stages(ci, d):
        items = [(bi, slice(ci * c, (ci + 1) * c), s) for bi in range(n_rows) for s in sls]
        d.update(items=items, r=[], k=[], v=[], cum=[], b=[], ar=[], kb_t=[])
        for i, it in enumerate(items):
            r, k, cum = r_ref[it].astype(f32), k_ref[it].astype(f32), cum_ref[it]
            kkraw = kk_ref[it].astype(f32)
            kk = kkraw / jnp.maximum(jnp.sqrt(head_sum(kkraw * kkraw)), 1e-12)
            b = kk * a_ref[it].astype(f32)
            g_inv = jnp.exp(-cum)
            at = -kk * jnp.exp(cum - lw_ref[it])
            d['ar'].append(jnp.concatenate([at, r * jnp.exp(cum)], axis=0))
            d['kb_t'].append(jnp.concatenate([stack(k * g_inv), stack(b * g_inv)], axis=0))
            for name, val in (('r', r), ('k', k), ('v', v_ref[it].astype(f32)), ('cum', cum),
                              ('b', b)):
                d[name].append(val)
            if i % 2:
                yield

    def head_stages(d):
        v = d['v']
        big = each(_bdot_nt, d['ar'], d['kb_t'])
        yield
        x = [jnp.where(strict, g[:c, w:], 0.0) for g in big]
        t = [eye2 + xi for xi in x]
        x = each(lambda xi: _bdot(xi, stack(xi)), x)
        yield
        for _ in range(4):
            tx = each(lambda ti, xi: _bdot(jnp.concatenate([ti, xi], axis=0), stack(xi)), t, x)
            t = each(lambda ti, r_: ti + r_[:c], t, tx)
            x = [r_[c:] for r_ in tx]
            yield
        t = each(lambda ti, xi: ti + _bdot(ti, stack(xi)), t, x)
        yield
        av = each(lambda g, vi: _bdot(
            jnp.concatenate([jnp.where(strict, g[:c, :w], 0.0),
                             jnp.where(incl, g[c:, :w], 0.0)], axis=0), stack(vi)), big, v)
        d.update(big=big, t=t, av=av)
        yield

    def tail_stages(d, box):
        state = box[0]
        ar_s = each(_bdot_nt, d['ar'], state)
        yield
        u = each(lambda ti, s, avi: _bdot(ti, stack(s[:c] + avi[:c])), d['t'], ar_s, d['av'])
        yield
        o = each(lambda g, s, avi, ui: s[c:] + avi[c:] + _bdot(
            jnp.where(incl, g[c:, w:], 0.0), stack(ui)), d['big'], ar_s, d['av'], u)
        new_state = []
        for st, ui, vi, ki, bi_, cumi in zip(state, u, d['v'], d['k'], d['b'], d['cum']):
            cum_last = cumi[c - 1:c, :]
            g_rem = jnp.exp(cum_last - cumi)
            vu = jnp.concatenate([vi, ui], axis=0)
            kb = jnp.concatenate([ki * g_rem, bi_ * g_rem], axis=0)
            upd = _bdot(vu.T, kb)
            new_state.append(st * jnp.exp(cum_last) + jnp.where(diag, upd, 0.0))
        box[0] = new_state
        yield
        for it, oi, ri, ki, vi in zip(d['items'], o, d['r'], d['k'], d['v']):
            s = it[2]
            mean = head_sum(oi) * (1.0 / HEAD_DIM)
            dev = oi - mean
            var = head_sum(dev * dev) * (1.0 / HEAD_DIM)
            y = dev * lax.rsqrt(var + LN_X_EPS) * lng_ref[:, s] + lnb_ref[:, s]
            o_ref[it] = (y + head_sum(ri * ki * rk_ref[:, s]) * vi).astype(o_ref.dtype)
        yield

    n_state = n_rows * len(sls)
    box = [[state_ref[j] for j in range(n_state)]]
    chunks = [dict() for _ in range(n_chunks)]
    for step in range(n_chunks + 2):
        live = {}
        if 0 < step <= n_chunks:
            live['head'] = head_stages(chunks[step - 1])
        if step < n_chunks:
            live['prep'] = prep_stages(step, chunks[step])
        if step > 1:
            live['tail'] = tail_stages(chunks[step - 2], box)
        n = 0
        while live:
            n += 1
            for name in ('head', 'prep', 'tail'):
                if name not in live or (name == 'tail' and n % 2 and len(live) > 1):
                    continue
                if next(live[name], StopIteration) is StopIteration:
                    del live[name]
    for j in range(n_state):
        state_ref[j] = box[0][j]


def _wkv(r, lw, cum, k, v, kkraw, a, seq, r_k, lnx_g, lnx_b, chunks_per_step=4, rows_per_step=2):
    bt, d = r.shape
    nb = bt // seq
    rows = chunks_per_step * CHUNK
    blk = pl.BlockSpec((rows_per_step, rows, d), lambda b, c: (b, c, 0))
    uw = WKV_HEADS * HEAD_DIM
    streams = [x.reshape(nb, seq, d) for x in (r, lw, cum, k, v, kkraw, a)]
    return pl.pallas_call(
        _wkv_kernel,
        grid=(nb // rows_per_step, seq // rows),
        in_specs=[blk] * 7 + [_const_spec((1, d))] * 3,
        out_specs=blk,
        out_shape=jax.ShapeDtypeStruct((nb, seq, d), BF16),
        scratch_shapes=[pltpu.VMEM((rows_per_step * (N_HEADS // WKV_HEADS), uw, uw), jnp.float32)],
        compiler_params=pltpu.CompilerParams(dimension_semantics=("parallel", "arbitrary"),
                                             vmem_limit_bytes=VMEM_LIMIT),
    )(*streams, r_k.reshape(1, d), lnx_g.reshape(1, d), lnx_b.reshape(1, d)).reshape(bt, d)


def _out_proj_kernel(x_ref, o_ref, z_ref, w_ref, y_ref):
    x = x_ref[...]
    z = z_ref[...].astype(x.dtype)
    y = o_ref[...].astype(x.dtype) * (z / (1.0 + jnp.exp(-z)))
    y_ref[...] = x + _bdot(y, w_ref[...])


def _out_proj(x2d, o, z, w_out, tm=1024):
    bt, d = x2d.shape
    row = lambda i: (i, 0)
    blk = pl.BlockSpec((tm, d), row)
    return pl.pallas_call(
        _out_proj_kernel,
        grid=(bt // tm,),
        in_specs=[blk, blk, blk, _const_spec((d, d))],
        out_specs=blk,
        out_shape=jax.ShapeDtypeStruct((bt, d), jnp.float32),
        compiler_params=pltpu.CompilerParams(dimension_semantics=("parallel",),
                                             vmem_limit_bytes=VMEM_LIMIT),
    )(x2d, o, z, w_out.astype(BF16))


def _rwkv_layer(x2d, seq, g, mu, w_in, w0, w1, w2, a0, a1, a2, k_k, k_a, r_k, lnx_g, lnx_b, w_out):
    r, lw, cum, k, v, kkraw, a, z = _rwkv_in(x2d, seq, g, mu, w_in, w0, w1, w2, a0, a1, a2, k_k,
                                             k_a)
    o = _wkv(r, lw, cum, k, v, kkraw, a, seq, r_k, lnx_g, lnx_b)
    return _out_proj(x2d, o, z, w_out)


def _nsa_in_kernel(x_ref, g_ref, w_ref, qg_ref, kg_ref, q_out, kc_out, vc_out, ks_out, vs_out,
                   kw_out, vw_out, z_out, gate_out, scr_ref, *, tiles_per_seq):
    d = D_MODEL
    h = (_rms(x_ref[...]) * g_ref[...]).astype(BF16)
    tm = h.shape[0]

    def proj(off, width):
        return jnp.dot(h, w_ref[:, off:off + width], preferred_element_type=jnp.float32)

    lane = lax.broadcasted_iota(jnp.int32, (tm, LANES), 1)
    low_lanes = lane < HEAD_DIM
    tok = (pl.program_id(0) % tiles_per_seq) * tm + lax.broadcasted_iota(jnp.int32, (tm, LANES), 0)
    blk_bias = jnp.where(lane - HEAD_DIM == jnp.right_shift(tok, SLC_SHIFT), 1.0, 0.0)

    def slot(p, g):
        two = p[:, (g // PAIR) * LANES:(g // PAIR + 1) * LANES]
        return pltpu.roll(two, HEAD_DIM, axis=1) if g % PAIR else two

    def narrow(ref, p, gain_row):
        for g in range(N_KV_GROUPS):
            seg = p[:, g * HEAD_DIM:(g + 1) * HEAD_DIM]
            if gain_row is not None:
                seg = _rms(seg) * kg_ref[gain_row:gain_row + 1, :HEAD_DIM]
            ref[0, g] = seg.astype(ref.dtype)

    def values(ref, p):
        for g in range(N_KV_GROUPS):
            ref[0, g] = jnp.where(low_lanes, slot(p, g), 1.0).astype(ref.dtype)

    p = proj(0, d)
    for j in range(N_PAIRS):
        x = p[:, j * LANES:(j + 1) * LANES]
        sq = x * x
        ms = jnp.where(low_lanes, jnp.sum(jnp.where(low_lanes, sq, 0.0), axis=-1, keepdims=True),
                       jnp.sum(jnp.where(low_lanes, 0.0, sq), axis=-1, keepdims=True))
        q_out[:, j * LANES:(j + 1) * LANES] = (
            x * lax.rsqrt(ms * (1.0 / HEAD_DIM) + NORM_EPS) * qg_ref[...]).astype(q_out.dtype)
    def flattened(ref, p):
        for j in range(KV_WIDTH // LANES):
            scr_ref[j] = p[:, j * LANES:(j + 1) * LANES]
        n_rows = tm // CMP_STRIDE
        low = lax.broadcasted_iota(jnp.int32, (n_rows, LANES), 1) < HEAD_DIM
        for m in range(CMP_STRIDE // PAIR):
            first = [scr_ref[j, pl.ds(PAIR * m, n_rows, stride=CMP_STRIDE), :]
                     for j in range(KV_WIDTH // LANES)]
            second = [scr_ref[j, pl.ds(PAIR * m + 1, n_rows, stride=CMP_STRIDE), :]
                      for j in range(KV_WIDTH // LANES)]
            for g in range(N_KV_GROUPS):
                a, b = first[g // PAIR], second[g // PAIR]
                a = pltpu.roll(a, HEAD_DIM, axis=1) if g % PAIR else a
                b = b if g % PAIR else pltpu.roll(b, HEAD_DIM, axis=1)
                ref[0, g, :, m * LANES:(m + 1) * LANES] = jnp.where(low, a, b).astype(ref.dtype)

    flattened(kc_out, proj(d, KV_WIDTH))
    flattened(vc_out, proj(d + KV_WIDTH, KV_WIDTH))
    p = proj(d + 2 * KV_WIDTH, KV_WIDTH)
    for g in range(N_KV_GROUPS):
        x = jnp.where(low_lanes, slot(p, g), 0.0)
        ms = jnp.sum(x * x, axis=-1, keepdims=True) * (1.0 / HEAD_DIM)
        ks_out[0, g] = (x * lax.rsqrt(ms + NORM_EPS) * kg_ref[1:2, :] + blk_bias).astype(BF16)
    values(vs_out, proj(d + 3 * KV_WIDTH, KV_WIDTH))
    narrow(kw_out, proj(d + 4 * KV_WIDTH, KV_WIDTH), 2)
    values(vw_out, proj(d + 5 * KV_WIDTH, KV_WIDTH))
    z_out[...] = proj(d + 6 * KV_WIDTH, d).astype(z_out.dtype)
    gate_out[...] = proj(2 * d + 6 * KV_WIDTH, LANES)


def _nsa_in(x2d, nb, seq, g, w_in, q_gain, k_gain, tm=256):
    bt, d = x2d.shape
    tps = seq // tm
    width = w_in.shape[1]
    padded = d + 6 * KV_WIDTH + d + LANES
    w = jnp.pad(w_in, ((0, 0), (0, padded - width))).astype(BF16)
    row = lambda i: (i, 0)
    grp = lambda i: (i // tps, 0, i % tps, 0)
    kv_blk = pl.BlockSpec((1, N_KV_GROUPS, tm, HEAD_DIM), grp)
    v_blk = pl.BlockSpec((1, N_KV_GROUPS, tm, LANES), grp)
    half = CMP_STRIDE * HEAD_DIM
    u_blk = pl.BlockSpec((1, N_KV_GROUPS, tm // CMP_STRIDE, half), grp)
    u_b16 = jax.ShapeDtypeStruct((nb, N_KV_GROUPS, seq // CMP_STRIDE, half), BF16)
    k_b16 = jax.ShapeDtypeStruct((nb, N_KV_GROUPS, seq, HEAD_DIM), BF16)
    v_b16 = jax.ShapeDtypeStruct((nb, N_KV_GROUPS, seq, LANES), BF16)
    wide = jax.ShapeDtypeStruct((bt, d), BF16)
    assert seq // SLC_BLOCK <= HEAD_DIM
    qg = jnp.tile(q_gain * (LOG2_E * HEAD_DIM ** -0.5), PAIR).reshape(1, LANES)
    return pl.pallas_call(
        functools.partial(_nsa_in_kernel, tiles_per_seq=tps),
        grid=(bt // tm,),
        in_specs=[pl.BlockSpec((tm, d), row), _const_spec((1, d)), _const_spec((d, padded)),
                  _const_spec((1, LANES)), _const_spec((N_BRANCHES, LANES))],
        out_specs=[pl.BlockSpec((tm, d), row), u_blk, u_blk, v_blk, v_blk, kv_blk, v_blk,
                   pl.BlockSpec((tm, d), row), pl.BlockSpec((tm, LANES), row)],
        out_shape=[wide, u_b16, u_b16, v_b16, v_b16, k_b16, v_b16, wide,
                   jax.ShapeDtypeStruct((bt, LANES), jnp.float32)],
        scratch_shapes=[pltpu.VMEM((KV_WIDTH // LANES, tm, LANES), jnp.float32)],
        compiler_params=pltpu.CompilerParams(dimension_semantics=("parallel",),
                                             vmem_limit_bytes=VMEM_LIMIT),
    )(x2d, g.reshape(1, d), w, qg, jnp.pad(k_gain, ((0, 0), (0, LANES - HEAD_DIM))))


def _gelu_tanh(x):
    return 0.5 * x * (1.0 + jnp.tanh(0.7978845608028654 * (x + 0.044715 * (x * x * x))))


def _compress_kernel(uk_ref, uv_ref, pe_ref, w1_ref, b1_ref, w2_ref, kg_ref, kc_out, vc_out):
    half = CMP_STRIDE * HEAD_DIM
    for kv, (u_ref, out) in enumerate(((uk_ref, kc_out), (uv_ref, vc_out))):
        uh = u_ref[0, 0]
        rows = uh.shape[0]
        ha = _bdot(uh + pe_ref[kv, 0:1, :], w1_ref[kv, :half, :])
        hb = _bdot(uh + pe_ref[kv, 1:2, :], w1_ref[kv, half:, :])
        hid = _gelu_tanh(ha + pltpu.roll(hb, rows - 1, axis=0) + b1_ref[kv])
        y = _bdot(hid, w2_ref[kv])
        if kv == 0:
            y = y[:, :HEAD_DIM]
            out[0, 0] = (_rms(y) * kg_ref[0:1, :]).astype(out.dtype)
        else:
            low_lanes = lax.broadcasted_iota(jnp.int32, y.shape, 1) < HEAD_DIM
            out[0, 0] = jnp.where(low_lanes, y, 1.0).astype(out.dtype)


def _compress(kc_raw, vc_raw, pe, w1, b1, w2, k_gain):
    nb, ng, rows, half = kc_raw.shape
    n = half // CMP_STRIDE
    uk, uv = kc_raw, vc_raw
    whole = lambda b, g: (b, g, 0, 0)
    u_blk = pl.BlockSpec((1, 1, rows, half), whole)
    w2p = jnp.pad(w2, ((0, 0), (0, 0), (0, LANES - n))).astype(BF16)
    return pl.pallas_call(
        _compress_kernel,
        grid=(nb, ng),
        in_specs=[u_blk, u_blk, _const_spec((2, 2, half)), _const_spec((2, 2 * half, CMP_HIDDEN)),
                  _const_spec((2, 1, CMP_HIDDEN)), _const_spec((2, CMP_HIDDEN, LANES)),
                  _const_spec((N_BRANCHES, n))],
        out_specs=[pl.BlockSpec((1, 1, rows, n), whole), pl.BlockSpec((1, 1, rows, LANES), whole)],
        out_shape=[jax.ShapeDtypeStruct((nb, ng, rows, n), BF16),
                   jax.ShapeDtypeStruct((nb, ng, rows, LANES), BF16)],
        compiler_params=pltpu.CompilerParams(dimension_semantics=("parallel", "parallel"),
                                             vmem_limit_bytes=VMEM_LIMIT),
    )(uk, uv, pe.reshape(2, 2, half), w1.astype(BF16), b1.reshape(2, 1, CMP_HIDDEN), w2p, k_gain)


M_INIT = -1e20


LOG2_E = 1.4426950408889634


MAX_STATIC_BOUND = 50.0


def _softmax_step(s, bias, v_tile, carry, hg, qt, online):
    sb = [s[h * qt:(h + 1) * qt] for h in range(hg)]
    if bias is not None:
        sb = [x + bias for x in sb]
    if not online:
        return jnp.concatenate(
            [carry[h * qt:(h + 1) * qt] + jnp.dot(jnp.exp2(sb[h]).astype(BF16), v_tile,
                                                   preferred_element_type=jnp.float32)
             for h in range(hg)], axis=0)
    m, acc = carry
    m_new = jnp.maximum(m, jnp.concatenate(
        [jnp.max(x, axis=-1, keepdims=True) for x in sb], axis=0))
    p = jnp.concatenate(
        [jnp.exp2(sb[h] - m_new[h * qt:(h + 1) * qt]).astype(BF16) for h in range(hg)], axis=0)
    acc = jnp.exp2(m - m_new) * acc + jnp.dot(p, v_tile, preferred_element_type=jnp.float32)
    return m_new, acc


N_ATTN_INPUTS = 10


def _nsa_attn_kernel(*refs, qt, kt, seq, online, tile):
    t0 = pl.program_id(2) * qt if tile is None else tile * qt
    _nsa_attn_tile(*refs[:N_ATTN_INPUTS], refs[-1], t0=t0, qt=qt, kt=kt, seq=seq, online=online)


def _nsa_attn_tile(bound_ref, q_ref, gate_ref, bcast_ref, kc_ref, vc_ref, ks_ref, vs_ref,
                   kw_ref, vw_ref, o_ref, *, t0, qt, kt, seq, online):
    hg = HEADS_PER_GROUP
    static = isinstance(t0, int)
    n_slc = seq // SLC_BLOCK
    n_cmp_rows = kc_ref.shape[2]
    if static:
        n_cmp_rows = min(n_cmp_rows, -(-((t0 + qt) // CMP_STRIDE) // LANES) * LANES)
    rows = hg * qt

    def aligned(x, m):
        return x if isinstance(x, int) else pl.multiple_of(x, m)
    low_lanes = lax.broadcasted_iota(jnp.int32, (qt, LANES), 1) < HEAD_DIM
    keep_c, keep_s, keep_w = (0.0, 0.0, 0.0) if online else (
        -bound_ref[0], -bound_ref[1], -bound_ref[2])

    def swap_halves(x):
        return pltpu.roll(x, HEAD_DIM, axis=1)

    q = q_ref[...].astype(jnp.float32)
    q_n = []
    for h in range(hg):
        two = q[:, (h // PAIR) * LANES:(h // PAIR + 1) * LANES]
        q_n.append(jnp.where(low_lanes, swap_halves(two) if h % PAIR else two, 0.0))
    qs = jnp.concatenate([x[:, :HEAD_DIM] for x in q_n], axis=0).astype(BF16)
    t_rows = t0 + lax.broadcasted_iota(jnp.int32, (qt, 1), 0)

    w0 = max(t0 - WINDOW, 0) if static else pl.multiple_of(jnp.maximum(t0 - WINDOW, 0), qt)
    span = t0 + qt - w0 if static else WINDOW + qt
    s = _bdot_nt(qs, kc_ref[0, 0, :n_cmp_rows, :])
    s_win = _bdot_nt(qs, kw_ref[0, 0, pl.ds(w0, span), :])

    cmp_end = lax.broadcasted_iota(jnp.int32, (1, n_cmp_rows), 1) * CMP_STRIDE + (CMP_BLOCK - 1)
    cbias = jnp.where(cmp_end <= t_rows, keep_c, NEG_INF)
    p_heads = []
    for h in range(hg):
        sb = s[h * qt:(h + 1) * qt] + cbias
        if online:
            sb = sb - jnp.maximum(jnp.max(sb, axis=-1, keepdims=True), M_INIT)
        e = jnp.exp2(sb)
        p_heads.append(e / jnp.maximum(jnp.sum(e, axis=-1, keepdims=True), 1e-37))
    o_cmp = jnp.dot(jnp.concatenate([p.astype(BF16) for p in p_heads], axis=0),
                    vc_ref[0, 0, :n_cmp_rows, :], preferred_element_type=jnp.float32)

    p_sum = p_heads[0]
    for h in range(1, hg):
        p_sum = p_sum + p_heads[h]
    jn = lax.broadcasted_iota(jnp.int32, (n_slc, n_cmp_rows), 0) * SLC_BLOCK
    nn = lax.broadcasted_iota(jnp.int32, (n_slc, n_cmp_rows), 1) * CMP_STRIDE
    ov_t = jnp.where((nn < jn + SLC_BLOCK) & (nn + (CMP_BLOCK - 1) >= jn), 1.0, 0.0).astype(BF16)
    p_hi = p_sum.astype(BF16)
    p_lo = (p_sum - p_hi.astype(p_sum.dtype)).astype(BF16)
    imp_t = (lax.dot_general(ov_t, p_hi, (((1,), (1,)), ((), ())), preferred_element_type=jnp.float32)
             + lax.dot_general(ov_t, p_lo, (((1,), (1,)), ((), ())), preferred_element_type=jnp.float32))

    jb = lax.broadcasted_iota(jnp.int32, (n_slc, qt), 0)
    tq = t0 + lax.broadcasted_iota(jnp.int32, (n_slc, qt), 1)
    dist = jnp.right_shift(tq, SLC_SHIFT) - jb
    forced = (jb == 0) | ((dist >= 0) & (dist < N_LOCAL_BLOCKS))
    score = jnp.where(dist >= 0, imp_t + jnp.where(forced, FORCE_BONUS, 0.0), -jnp.inf)
    sub = 8
    n_cand = (t0 + qt - 1) // SLC_BLOCK + 1 if static else n_slc
    if n_cand <= SLC_TOPK:
        n_cand = 0
    groups = [score[v * sub:(v + 1) * sub] for v in range(n_slc // sub)]
    jrow = lax.broadcasted_iota(jnp.int32, (sub, qt), 0)
    cnt = [jnp.zeros((sub, qt), jnp.float32) for _ in groups]
    for j2 in range(n_cand):
        row = score[j2:j2 + 1, :]
        for v, sv in enumerate(groups[:-(-n_cand // sub)]):
            if v * sub > j2:
                inc = jnp.where(row >= sv, 1.0, 0.0)
            elif (v + 1) * sub - 1 <= j2:
                inc = jnp.where(row > sv, 1.0, 0.0)
            else:
                inc = jnp.where(jrow + v * sub > j2, jnp.where(row >= sv, 1.0, 0.0),
                                jnp.where(row > sv, 1.0, 0.0))
            cnt[v] = cnt[v] + inc
    cnt = jnp.concatenate(cnt, axis=0)
    blk_bias_t = jnp.where(dist >= 0, jnp.where(cnt < min(SLC_TOPK, n_slc), keep_s, NEG_INF),
                           NEG_INF)
    parts = [jnp.zeros((HEAD_DIM, qt), jnp.float32), blk_bias_t]
    if n_slc < HEAD_DIM:
        parts.append(jnp.zeros((HEAD_DIM - n_slc, qt), jnp.float32))
    blk_bias = jnp.concatenate(parts, axis=0).T
    q_aug = jnp.concatenate([x + blk_bias for x in q_n], axis=0).astype(BF16)

    lag = t_rows - (w0 + lax.broadcasted_iota(jnp.int32, (1, span), 1))
    wbias = jnp.where(lag >= 0, jnp.where(lag < WINDOW, keep_w, NEG_INF), NEG_INF)
    v_win = vw_ref[0, 0, pl.ds(w0, span), :]
    acc_win = []
    for h in range(hg):
        sb = s_win[h * qt:(h + 1) * qt] + wbias
        if online:
            sb = sb - jnp.max(sb, axis=-1, keepdims=True)
        acc_win.append(jnp.dot(jnp.exp2(sb).astype(BF16), v_win,
                               preferred_element_type=jnp.float32))
    acc_win = jnp.concatenate(acc_win, axis=0)

    def slc_tile(k0, width, bias, carry):
        k_tile = ks_ref[0, 0, pl.ds(k0, width), :]
        s = jnp.concatenate([_bdot_nt(q_aug[h * qt:(h + 1) * qt], k_tile) for h in range(hg)],
                            axis=0)
        return _softmax_step(s, bias, vs_ref[0, 0, pl.ds(k0, width), :], carry, hg, qt, online)

    def slc_body(i, carry):
        return slc_tile(aligned(i * kt, kt), kt, None, carry)

    carry = jnp.zeros((rows, LANES), jnp.float32)
    if online:
        carry = (jnp.full((rows, 1), M_INIT, jnp.float32), carry)
    n_full = t0 // kt
    if static:
        for i in range(n_full):
            carry = slc_body(i, carry)
    else:
        carry = lax.fori_loop(0, n_full // 2,
                              lambda j, c: slc_body(2 * j + 1, slc_body(2 * j, c)), carry)
        carry = lax.cond(n_full % 2 == 1, lambda c: slc_body(n_full - 1, c), lambda c: c, carry)
    k0 = aligned(n_full * kt, kt)
    last = t0 + qt - k0 if static else kt
    causal = jnp.where(k0 + lax.broadcasted_iota(jnp.int32, (1, last), 1) <= t_rows, 0.0, NEG_INF)
    acc_slc = slc_tile(k0, last, causal, carry)
    if online:
        acc_slc = acc_slc[1]

    gates = 1.0 / (1.0 + jnp.exp(-gate_ref[...]))
    spread = jnp.dot(gates.astype(BF16), bcast_ref[0],
                     preferred_element_type=jnp.float32)

    def pair(acc, j, normalise):
        a0 = acc[(2 * j) * qt:(2 * j + 1) * qt]
        a1 = acc[(2 * j + 1) * qt:(2 * j + 2) * qt]
        num = jnp.where(low_lanes, a0, swap_halves(a1))
        if not normalise:
            return num
        return num / jnp.where(low_lanes, swap_halves(a0), a1)

    for j in range(hg // PAIR):
        width = PAIR * HEAD_DIM
        o = jnp.zeros((qt, width), jnp.float32)
        for br, (acc, normalise) in enumerate(((o_cmp, False), (acc_slc, True), (acc_win, True))):
            col = (br * (hg // PAIR) + j) * width
            o = o + spread[:, col:col + width] * pair(acc, j, normalise)
        o_ref[:, j * width:(j + 1) * width] = o.astype(o_ref.dtype)


def _nsa_attn(q, gate, q_gain, k_gain, kc, vc, ks, vs, kw, vw, qt=256, kt=512):
    bt, d = q.shape
    nb, ng, seq, n = kw.shape
    kt = min(kt, seq)
    assert seq >= WINDOW + qt and seq % kt == 0 and kt % qt == 0
    nq = seq // qt
    hg = HEADS_PER_GROUP
    width = hg * n
    rows_c = kc.shape[2]
    col = jnp.arange(N_BRANCHES * width) // n
    src = (col // hg) * N_HEADS + jnp.arange(ng)[:, None] * hg + col % hg
    bcast = (jnp.arange(LANES)[None, :, None] == src[:, None, :]).astype(BF16)
    bounds = (1.01 * LOG2_E * n ** 0.5) * jnp.max(jnp.abs(q_gain)) * jnp.max(jnp.abs(k_gain), axis=1)
    out_sds = jax.ShapeDtypeStruct((bt, d), BF16)

    def specs(tile):
        pick = (lambda b, g, i: i) if tile is None else (lambda b, g: tile)
        at = lambda f: (lambda *ids: f(ids[0], ids[1], pick(*ids)))
        keys = seq if tile is None else (tile + 1) * qt
        cmp_rows = rows_c if tile is None else min(rows_c, -(-(keys // CMP_STRIDE) // LANES) * LANES)
        whole = at(lambda b, g, i: (b, g, 0, 0))
        q_blk = pl.BlockSpec((qt, width), at(lambda b, g, i: (b * nq + i, g)))
        in_specs = [pl.BlockSpec(memory_space=pltpu.SMEM), q_blk,
                    pl.BlockSpec((qt, LANES), at(lambda b, g, i: (b * nq + i, 0))),
                    pl.BlockSpec((1, LANES, N_BRANCHES * width), at(lambda b, g, i: (g, 0, 0))),
                    pl.BlockSpec((1, 1, cmp_rows, n), whole),
                    pl.BlockSpec((1, 1, cmp_rows, LANES), whole),
                    pl.BlockSpec((1, 1, keys, LANES), whole), pl.BlockSpec((1, 1, keys, LANES), whole),
                    pl.BlockSpec((1, 1, keys, n), whole), pl.BlockSpec((1, 1, keys, LANES), whole)]
        return in_specs, q_blk

    def generic(*args):
        in_specs, q_blk = specs(None)
        return pl.pallas_call(
            functools.partial(_nsa_attn_kernel, qt=qt, kt=kt, seq=seq, online=True, tile=None),
            grid=(nb, ng, nq), in_specs=in_specs, out_specs=q_blk, out_shape=out_sds,
            compiler_params=pltpu.CompilerParams(
                dimension_semantics=("parallel", "parallel", "arbitrary"),
                vmem_limit_bytes=VMEM_LIMIT),
        )(*args)

    def per_tile(*args):
        o = None
        for tile in range(nq):
            in_specs, q_blk = specs(tile)
            operands = list(args)
            if o is not None:
                in_specs.append(pl.BlockSpec(memory_space=pl.ANY))
                operands.append(o)
            o = pl.pallas_call(
                functools.partial(_nsa_attn_kernel, qt=qt, kt=kt, seq=seq, online=False, tile=tile),
                grid=(nb, ng), in_specs=in_specs, out_specs=q_blk, out_shape=out_sds,
                input_output_aliases={} if tile == 0 else {N_ATTN_INPUTS: 0},
                compiler_params=pltpu.CompilerParams(
                    dimension_semantics=("parallel", "parallel"), vmem_limit_bytes=VMEM_LIMIT),
            )(*operands)
        return o

    return lax.cond(jnp.max(bounds) <= MAX_STATIC_BOUND, per_tile, generic,
                    bounds.astype(jnp.float32), q, gate, bcast, kc, vc, ks, vs, kw, vw)


def _nsa_layer(x2d, nb, seq, g, w_in, q_gain, k_gain, cmp_pe, cmp_w1, cmp_b1, cmp_w2, w_out):
    q, kc_raw, vc_raw, ks, vs, kw, vw, z, gate = _nsa_in(x2d, nb, seq, g, w_in, q_gain, k_gain)
    kc, vc = _compress(kc_raw, vc_raw, cmp_pe, cmp_w1, cmp_b1, cmp_w2, k_gain)
    o = _nsa_attn(q, gate, q_gain, k_gain, kc, vc, ks, vs, kw, vw)
    return _out_proj(x2d, o, z, w_out)


def kernel(x, norm_g, rwkv_mu, rwkv_w_in, rwkv_w0, rwkv_w1, rwkv_w2, rwkv_a0, rwkv_a1, rwkv_a2, rwkv_k_k, rwkv_k_a, rwkv_r_k, rwkv_lnx_g, rwkv_lnx_b, rwkv_w_out, nsa_w_in, nsa_q_gain, nsa_k_gain, nsa_cmp_pe, nsa_cmp_w1, nsa_cmp_b1, nsa_cmp_w2, nsa_w_out):
    b, t, d = x.shape
    x2d = x.reshape(b * t, d)
    x2d = _rwkv_layer(x2d, t, norm_g[0], rwkv_mu[0], rwkv_w_in[0], rwkv_w0[0], rwkv_w1[0],
                      rwkv_w2[0], rwkv_a0[0], rwkv_a1[0], rwkv_a2[0], rwkv_k_k[0], rwkv_k_a[0],
                      rwkv_r_k[0].reshape(-1), rwkv_lnx_g[0], rwkv_lnx_b[0], rwkv_w_out[0])
    x2d = _nsa_layer(x2d, b, t, norm_g[1], nsa_w_in[0], nsa_q_gain[0], nsa_k_gain[0],
                     nsa_cmp_pe[0], nsa_cmp_w1[0], nsa_cmp_b1[0], nsa_cmp_w2[0], nsa_w_out[0])
    return x2d.reshape(b, t, d)
```

```python
import functools

import jax
import jax.numpy as jnp
from jax import lax
from jax.experimental import pallas as pl
from jax.experimental.pallas import tpu as pltpu

D_MODEL = 1024
HEAD_DIM = 64
N_HEADS = D_MODEL // HEAD_DIM
NORM_EPS = 1e-6
LN_X_EPS = 64e-5
DECAY_LORA = 64
ICLR_LORA = 64
N_KV_GROUPS = 4
HEADS_PER_GROUP = N_HEADS // N_KV_GROUPS
KV_WIDTH = N_KV_GROUPS * HEAD_DIM
N_BRANCHES = 3
CMP_BLOCK = 32
CMP_STRIDE = 16
CMP_HIDDEN = 256
SLC_BLOCK = 64
SLC_SHIFT = 6
SLC_TOPK = 16
N_LOCAL_BLOCKS = 2
WINDOW = 512
FORCE_BONUS = 1e4
NEG_INF = -1e30

LANES = 128
MXU_DEPTH = 256
PAIR = LANES // HEAD_DIM
N_PAIRS = N_HEADS // PAIR
HEAD_SHIFT = 6
WKV_HEADS = LANES // HEAD_DIM
CHUNK = 64
CHUNK_SHIFT = 6
VMEM_LIMIT = 48 * 1024 * 1024

BF16 = jnp.bfloat16


def _rms(x, eps=NORM_EPS):
    return x * lax.rsqrt(jnp.mean(x * x, axis=-1, keepdims=True) + eps)


def _bdot(a, b):
    return jnp.dot(a.astype(BF16), b.astype(BF16), preferred_element_type=jnp.float32)


def _bdot_nt(a, b):
    return lax.dot_general(a.astype(BF16), b.astype(BF16), (((1,), (1,)), ((), ())),
                           preferred_element_type=jnp.float32)


def _const_spec(shape):
    nd = len(shape)
    return pl.BlockSpec(shape, lambda *_: (0,) * nd, pipeline_mode=pl.Buffered(1))


def _rwkv_in_kernel(x_ref, xp_ref, g_ref, mu_ref, win_ref, w0_ref, w1_ref, w2_ref, a0_ref,
                    a1_ref, a2_ref, kk_ref, ka_ref,
                    r_out, lw_out, cum_out, k_out, v_out, kk_out, a_out, z_out, *, tiles_per_seq):
    d = D_MODEL
    g = g_ref[...]
    h = _rms(x_ref[...]) * g
    hp = _rms(xp_ref[7:8, :]) * g
    hp = jnp.where(pl.program_id(0) % tiles_per_seq == 0, 0.0, hp)
    row = lax.broadcasted_iota(jnp.int32, h.shape, 0)
    hs = jnp.where(row == 0, hp, pltpu.roll(h, 1, axis=0))
    dh = hs - h

    def mix(c):
        return h + dh * mu_ref[c:c + 1, :]

    r_out[...] = _bdot(mix(0), win_ref[:, 0 * d:1 * d]).astype(r_out.dtype)
    k = _bdot(mix(1), win_ref[:, 1 * d:2 * d])
    v_out[...] = _bdot(mix(2), win_ref[:, 2 * d:3 * d]).astype(v_out.dtype)
    z_out[...] = _bdot(mix(3), win_ref[:, 3 * d:4 * d]).astype(z_out.dtype)

    u = w0_ref[...] + _bdot(jnp.tanh(_bdot(mix(4), w1_ref[...])), w2_ref[...])
    sp = jnp.maximum(-u, 0.0) + jnp.log(1.0 + jnp.exp(-jnp.abs(u)))
    lw = -jnp.exp(-sp - 0.5)
    lw_out[...] = lw
    rows = min(lw.shape[0], MXU_DEPTH)
    ri = lax.broadcasted_iota(jnp.int32, (rows, rows), 0)
    ci = lax.broadcasted_iota(jnp.int32, (rows, rows), 1)
    same_chunk = jnp.right_shift(ri, CHUNK_SHIFT) == jnp.right_shift(ci, CHUNK_SHIFT)
    tril = jnp.where(ci <= ri, jnp.where(same_chunk, 1.0, 0.0), 0.0).astype(BF16)
    for r0 in range(0, lw.shape[0], rows):
        hi, lo = _split2(lw[r0:r0 + rows])
        cum_out[r0:r0 + rows, :] = (jnp.dot(tril, hi, preferred_element_type=jnp.float32)
                                    + jnp.dot(tril, lo, preferred_element_type=jnp.float32))
    ua = a0_ref[...] + _bdot(_bdot(mix(5), a1_ref[...]), a2_ref[...])
    a = 1.0 / (1.0 + jnp.exp(-ua))
    a_out[...] = a.astype(a_out.dtype)
    kk_out[...] = (k * kk_ref[...]).astype(kk_out.dtype)
    k_out[...] = (k * (1.0 + (a - 1.0) * ka_ref[...])).astype(k_out.dtype)


def _rwkv_in(x2d, seq, g, mu, w_in, w0, w1, w2, a0, a1, a2, k_k, k_a, tm=512):
    bt, d = x2d.shape
    tiles_per_seq = seq // tm
    row = lambda i: (i, 0)
    prev = lambda i: (jnp.maximum(i * (tm // 8) - 1, 0), 0)
    f32_sds = jax.ShapeDtypeStruct((bt, d), jnp.float32)
    b16_sds = jax.ShapeDtypeStruct((bt, d), BF16)
    return pl.pallas_call(
        functools.partial(_rwkv_in_kernel, tiles_per_seq=tiles_per_seq),
        grid=(bt // tm,),
        in_specs=[pl.BlockSpec((tm, d), row), pl.BlockSpec((8, d), prev),
                  _const_spec((1, d)), _const_spec((6, d)), _const_spec((d, 4 * d)),
                  _const_spec((1, d)), _const_spec((d, DECAY_LORA)), _const_spec((DECAY_LORA, d)),
                  _const_spec((1, d)), _const_spec((d, ICLR_LORA)), _const_spec((ICLR_LORA, d)),
                  _const_spec((1, d)), _const_spec((1, d))],
        out_specs=[pl.BlockSpec((tm, d), row)] * 8,
        out_shape=[b16_sds, f32_sds, f32_sds] + [b16_sds] * 5,
        compiler_params=pltpu.CompilerParams(dimension_semantics=("parallel",),
                                             vmem_limit_bytes=VMEM_LIMIT),
    )(x2d, x2d, g.reshape(1, d), mu, w_in.astype(BF16), w0.reshape(1, d), w1.astype(BF16),
      w2.astype(BF16), a0.reshape(1, d), a1.astype(BF16), a2.astype(BF16), k_k.reshape(1, d),
      k_a.reshape(1, d))


def _split2(x):
    hi = x.astype(BF16)
    return hi, (x - hi.astype(x.dtype)).astype(BF16)


def _wkv_kernel(r_ref, lw_ref, cum_ref, k_ref, v_ref, kk_ref, a_ref, rk_ref, lng_ref, lnb_ref,
                o_ref, state_ref):
    @pl.when(pl.program_id(1) == 0)
    def _():
        state_ref[...] = jnp.zeros_like(state_ref)

    c = CHUNK
    nh = WKV_HEADS
    uw = nh * HEAD_DIM
    pairs = range(N_HEADS // nh)
    sls = [slice(p * uw, (p + 1) * uw) for p in pairs]
    m0 = lax.broadcasted_iota(jnp.int32, (c, LANES), 1) < HEAD_DIM
    lane_head = jnp.right_shift(lax.broadcasted_iota(jnp.int32, (c, uw), 1), HEAD_SHIFT)
    head_masks = [lane_head == h for h in range(nh)]

    def head_sum(x):
        out = []
        for j in range(uw // LANES):
            xj = x[:, j * LANES:(j + 1) * LANES]
            s0 = jnp.sum(jnp.where(m0, xj, 0.0), axis=-1, keepdims=True)
            s1 = jnp.sum(jnp.where(m0, 0.0, xj), axis=-1, keepdims=True)
            out.append(jnp.where(m0, s0, s1))
        return jnp.concatenate(out, axis=1)

    def stack(y):
        return jnp.concatenate([jnp.where(m, y, 0.0) for m in head_masks], axis=0)

    def each(f, *lists):
        return [f(*xs) for xs in zip(*lists)]

    n_rows, n_chunks = r_ref.shape[0], r_ref.shape[1] // c
    f32 = cum_ref.dtype
    w = nh * c
    row2 = lax.broadcasted_iota(jnp.int32, (c, w), 0)
    col2 = jnp.bitwise_and(lax.broadcasted_iota(jnp.int32, (c, w), 1), c - 1)
    strict = col2 < row2
    incl = col2 <= row2
    eye2 = jnp.where(col2 == row2, 1.0, 0.0)
    diag = (jnp.right_shift(lax.broadcasted_iota(jnp.int32, (uw, uw), 0), HEAD_SHIFT)
            == jnp.right_shift(lax.broadcasted_iota(jnp.int32, (uw, uw), 1), HEAD_SHIFT))

    def prep_stages(ci, d):
        items = [(bi, slice(ci * c, (ci + 1) * c), s) for bi in range(n_rows) for s in sls]
        d.update(items=items, r=[], k=[], v=[], cum=[], b=[], ar=[], kb_t=[])
        for i, it in enumerate(items):
            r, k, cum = r_ref[it].astype(f32), k_ref[it].astype(f32), cum_ref[it]
            kkraw = kk_ref[it].astype(f32)
            kk = kkraw / jnp.maximum(jnp.sqrt(head_sum(kkraw * kkraw)), 1e-12)
            b = kk * a_ref[it].astype(f32)
            g_inv = jnp.exp(-cum)
            at = -kk * jnp.exp(cum - lw_ref[it])
            d['ar'].append(jnp.concatenate([at, r * jnp.exp(cum)], axis=0))
            d['kb_t'].append(jnp.concatenate([stack(k * g_inv), stack(b * g_inv)], axis=0))
            for name, val in (('r', r), ('k', k), ('v', v_ref[it].astype(f32)), ('cum', cum),
                              ('b', b)):
                d[name].append(val)
            if i % 2:
                yield

    def head_stages(d):
        v = d['v']
        big = each(_bdot_nt, d['ar'], d['kb_t'])
        yield
        x = [jnp.where(strict, g[:c, w:], 0.0) for g in big]
        t = [eye2 + xi for xi in x]
        x = each(lambda xi: _bdot(xi, stack(xi)), x)
        yield
        for _ in range(4):
            tx = each(lambda ti, xi: _bdot(jnp.concatenate([ti, xi], axis=0), stack(xi)), t, x)
            t = each(lambda ti, r_: ti + r_[:c], t, tx)
            x = [r_[c:] for r_ in tx]
            yield
        t = each(lambda ti, xi: ti + _bdot(ti, stack(xi)), t, x)
        yield
        av = each(lambda g, vi: _bdot(
            jnp.concatenate([jnp.where(strict, g[:c, :w], 0.0),
                             jnp.where(incl, g[c:, :w], 0.0)], axis=0), stack(vi)), big, v)
        d.update(big=big, t=t, av=av)
        yield

    def tail_stages(d, box):
        state = box[0]
        ar_s = each(_bdot_nt, d['ar'], state)
        yield
        u = each(lambda ti, s, avi: _bdot(ti, stack(s[:c] + avi[:c])), d['t'], ar_s, d['av'])
        yield
        o = each(lambda g, s, avi, ui: s[c:] + avi[c:] + _bdot(
            jnp.where(incl, g[c:, w:], 0.0), stack(ui)), d['big'], ar_s, d['av'], u)
        new_state = []
        for st, ui, vi, ki, bi_, cumi in zip(state, u, d['v'], d['k'], d['b'], d['cum']):
            cum_last = cumi[c - 1:c, :]
            g_rem = jnp.exp(cum_last - cumi)
            vu = jnp.concatenate([vi, ui], axis=0)
            kb = jnp.concatenate([ki * g_rem, bi_ * g_rem], axis=0)
            upd = _bdot(vu.T, kb)
            new_state.append(st * jnp.exp(cum_last) + jnp.where(diag, upd, 0.0))
        box[0] = new_state
        yield
        for it, oi, ri, ki, vi in zip(d['items'], o, d['r'], d['k'], d['v']):
            s = it[2]
            mean = head_sum(oi) * (1.0 / HEAD_DIM)
            dev = oi - mean
            var = head_sum(dev * dev) * (1.0 / HEAD_DIM)
            y = dev * lax.rsqrt(var + LN_X_EPS) * lng_ref[:, s] + lnb_ref[:, s]
            o_ref[it] = (y + head_sum(ri * ki * rk_ref[:, s]) * vi).astype(o_ref.dtype)
        yield

    n_state = n_rows * len(sls)
    box = [[state_ref[j] for j in range(n_state)]]
    chunks = [dict() for _ in range(n_chunks)]
    for step in range(n_chunks + 2):
        live = {}
        if 0 < step <= n_chunks:
            live['head'] = head_stages(chunks[step - 1])
        if step < n_chunks:
            live['prep'] = prep_stages(step, chunks[step])
        if step > 1:
            live['tail'] = tail_stages(chunks[step - 2], box)
        n = 0
        while live:
            n += 1
            for name in ('head', 'prep', 'tail'):
                if name not in live or (name == 'tail' and n % 2 and len(live) > 1):
                    continue
                if next(live[name], StopIteration) is StopIteration:
                    del live[name]
    for j in range(n_state):
        state_ref[j] = box[0][j]


def _wkv(r, lw, cum, k, v, kkraw, a, seq, r_k, lnx_g, lnx_b, chunks_per_step=4, rows_per_step=2):
    bt, d = r.shape
    nb = bt // seq
    rows = chunks_per_step * CHUNK
    blk = pl.BlockSpec((rows_per_step, rows, d), lambda b, c: (b, c, 0))
    uw = WKV_HEADS * HEAD_DIM
    streams = [x.reshape(nb, seq, d) for x in (r, lw, cum, k, v, kkraw, a)]
    return pl.pallas_call(
        _wkv_kernel,
        grid=(nb // rows_per_step, seq // rows),
        in_specs=[blk] * 7 + [_const_spec((1, d))] * 3,
        out_specs=blk,
        out_shape=jax.ShapeDtypeStruct((nb, seq, d), BF16),
        scratch_shapes=[pltpu.VMEM((rows_per_step * (N_HEADS // WKV_HEADS), uw, uw), jnp.float32)],
        compiler_params=pltpu.CompilerParams(dimension_semantics=("parallel", "arbitrary"),
                                             vmem_limit_bytes=VMEM_LIMIT),
    )(*streams, r_k.reshape(1, d), lnx_g.reshape(1, d), lnx_b.reshape(1, d)).reshape(bt, d)


def _out_proj_kernel(x_ref, o_ref, z_ref, w_ref, y_ref):
    x = x_ref[...]
    z = z_ref[...].astype(x.dtype)
    y = o_ref[...].astype(x.dtype) * (z / (1.0 + jnp.exp(-z)))
    y_ref[...] = x + _bdot(y, w_ref[...])


def _out_proj(x2d, o, z, w_out, tm=1024):
    bt, d = x2d.shape
    row = lambda i: (i, 0)
    blk = pl.BlockSpec((tm, d), row)
    return pl.pallas_call(
        _out_proj_kernel,
        grid=(bt // tm,),
        in_specs=[blk, blk, blk, _const_spec((d, d))],
        out_specs=blk,
        out_shape=jax.ShapeDtypeStruct((bt, d), jnp.float32),
        compiler_params=pltpu.CompilerParams(dimension_semantics=("parallel",),
                                             vmem_limit_bytes=VMEM_LIMIT),
    )(x2d, o, z, w_out.astype(BF16))


def _rwkv_mixer(x2d, seq, g, mu, w_in, w0, w1, w2, a0, a1, a2, k_k, k_a, r_k, lnx_g, lnx_b):
    r, lw, cum, k, v, kkraw, a, z = _rwkv_in(x2d, seq, g, mu, w_in, w0, w1, w2, a0, a1, a2, k_k,
                                             k_a)
    return _wkv(r, lw, cum, k, v, kkraw, a, seq, r_k, lnx_g, lnx_b), z


def _rwkv_layer(x2d, seq, g, mu, w_in, w0, w1, w2, a0, a1, a2, k_k, k_a, r_k, lnx_g, lnx_b, w_out):
    o, z = _rwkv_mixer(x2d, seq, g, mu, w_in, w0, w1, w2, a0, a1, a2, k_k, k_a, r_k, lnx_g, lnx_b)
    return _out_proj(x2d, o, z, w_out)


def _nsa_in_kernel(*refs, tiles_per_seq, fused):
    if fused:
        x_ref, po_ref, pz_ref, pw_ref = refs[:4]
        x1_out, refs = refs[8], refs[4:8] + refs[9:]
        pz = pz_ref[...].astype(jnp.float32)
        x_in = x_ref[...] + _bdot(po_ref[...].astype(jnp.float32) * (pz / (1.0 + jnp.exp(-pz))),
                                  pw_ref[...])
        x1_out[...] = x_in
    else:
        x_in, refs = refs[0][...], refs[1:]
    (g_ref, w_ref, qg_ref, kg_ref, q_out, kc_out, vc_out, ks_out, vs_out, kw_out, vw_out, z_out,
     gate_out, scr_ref) = refs
    d = D_MODEL
    h = (_rms(x_in) * g_ref[...]).astype(BF16)
    tm = h.shape[0]

    def proj(off, width):
        return jnp.dot(h, w_ref[:, off:off + width], preferred_element_type=jnp.float32)

    lane = lax.broadcasted_iota(jnp.int32, (tm, LANES), 1)
    low_lanes = lane < HEAD_DIM
    tok = (pl.program_id(0) % tiles_per_seq) * tm + lax.broadcasted_iota(jnp.int32, (tm, LANES), 0)
    blk_bias = jnp.where(lane - HEAD_DIM == jnp.right_shift(tok, SLC_SHIFT), 1.0, 0.0)

    def slot(p, g):
        two = p[:, (g // PAIR) * LANES:(g // PAIR + 1) * LANES]
        return pltpu.roll(two, HEAD_DIM, axis=1) if g % PAIR else two

    def narrow(ref, p, gain_row):
        for g in range(N_KV_GROUPS):
            seg = p[:, g * HEAD_DIM:(g + 1) * HEAD_DIM]
            if gain_row is not None:
                seg = _rms(seg) * kg_ref[gain_row:gain_row + 1, :HEAD_DIM]
            ref[0, g] = seg.astype(ref.dtype)

    def values(ref, p):
        for g in range(N_KV_GROUPS):
            ref[0, g] = jnp.where(low_lanes, slot(p, g), 1.0).astype(ref.dtype)

    p = proj(0, d)
    for j in range(N_PAIRS):
        x = p[:, j * LANES:(j + 1) * LANES]
        sq = x * x
        ms = jnp.where(low_lanes, jnp.sum(jnp.where(low_lanes, sq, 0.0), axis=-1, keepdims=True),
                       jnp.sum(jnp.where(low_lanes, 0.0, sq), axis=-1, keepdims=True))
        q_out[:, j * LANES:(j + 1) * LANES] = (
            x * lax.rsqrt(ms * (1.0 / HEAD_DIM) + NORM_EPS) * qg_ref[...]).astype(q_out.dtype)
    def flattened(ref, p):
        for j in range(KV_WIDTH // LANES):
            scr_ref[j] = p[:, j * LANES:(j + 1) * LANES]
        n_rows = tm // CMP_STRIDE
        low = lax.broadcasted_iota(jnp.int32, (n_rows, LANES), 1) < HEAD_DIM
        for m in range(CMP_STRIDE // PAIR):
            first = [scr_ref[j, pl.ds(PAIR * m, n_rows, stride=CMP_STRIDE), :]
                     for j in range(KV_WIDTH // LANES)]
            second = [scr_ref[j, pl.ds(PAIR * m + 1, n_rows, stride=CMP_STRIDE), :]
                      for j in range(KV_WIDTH // LANES)]
            for g in range(N_KV_GROUPS):
                a, b = first[g // PAIR], second[g // PAIR]
                a = pltpu.roll(a, HEAD_DIM, axis=1) if g % PAIR else a
                b = b if g % PAIR else pltpu.roll(b, HEAD_DIM, axis=1)
                ref[0, g, :, m * LANES:(m + 1) * LANES] = jnp.where(low, a, b).astype(ref.dtype)

    flattened(kc_out, proj(d, KV_WIDTH))
    flattened(vc_out, proj(d + KV_WIDTH, KV_WIDTH))
    p = proj(d + 2 * KV_WIDTH, KV_WIDTH)
    for g in range(N_KV_GROUPS):
        x = jnp.where(low_lanes, slot(p, g), 0.0)
        ms = jnp.sum(x * x, axis=-1, keepdims=True) * (1.0 / HEAD_DIM)
        ks_out[0, g] = (x * lax.rsqrt(ms + NORM_EPS) * kg_ref[1:2, :] + blk_bias).astype(BF16)
    values(vs_out, proj(d + 3 * KV_WIDTH, KV_WIDTH))
    narrow(kw_out, proj(d + 4 * KV_WIDTH, KV_WIDTH), 2)
    values(vw_out, proj(d + 5 * KV_WIDTH, KV_WIDTH))
    z_out[...] = proj(d + 6 * KV_WIDTH, d).astype(z_out.dtype)
    gate_out[...] = proj(2 * d + 6 * KV_WIDTH, LANES)


def _nsa_in(x2d, nb, seq, g, w_in, q_gain, k_gain, prev=None, tm=256):
    bt, d = x2d.shape
    tps = seq // tm
    width = w_in.shape[1]
    padded = d + 6 * KV_WIDTH + d + LANES
    w = jnp.pad(w_in, ((0, 0), (0, padded - width))).astype(BF16)
    row = lambda i: (i, 0)
    grp = lambda i: (i // tps, 0, i % tps, 0)
    kv_blk = pl.BlockSpec((1, N_KV_GROUPS, tm, HEAD_DIM), grp)
    v_blk = pl.BlockSpec((1, N_KV_GROUPS, tm, LANES), grp)
    half = CMP_STRIDE * HEAD_DIM
    u_blk = pl.BlockSpec((1, N_KV_GROUPS, tm // CMP_STRIDE, half), grp)
    u_b16 = jax.ShapeDtypeStruct((nb, N_KV_GROUPS, seq // CMP_STRIDE, half), BF16)
    k_b16 = jax.ShapeDtypeStruct((nb, N_KV_GROUPS, seq, HEAD_DIM), BF16)
    v_b16 = jax.ShapeDtypeStruct((nb, N_KV_GROUPS, seq, LANES), BF16)
    wide = jax.ShapeDtypeStruct((bt, d), BF16)
    assert seq // SLC_BLOCK <= HEAD_DIM
    qg = jnp.tile(q_gain * (LOG2_E * HEAD_DIM ** -0.5), PAIR).reshape(1, LANES)
    wide_blk = pl.BlockSpec((tm, d), row)
    fused = prev is not None
    pre_in, pre_specs, pre_out, pre_shape = [], [], [], []
    if fused:
        pre_in = [prev[0], prev[1], prev[2].astype(BF16)]
        pre_specs = [wide_blk, wide_blk, _const_spec((d, d))]
        pre_out, pre_shape = [wide_blk], [jax.ShapeDtypeStruct((bt, d), jnp.float32)]
    outs = pl.pallas_call(
        functools.partial(_nsa_in_kernel, tiles_per_seq=tps, fused=fused),
        grid=(bt // tm,),
        in_specs=[wide_blk] + pre_specs + [_const_spec((1, d)), _const_spec((d, padded)),
                                           _const_spec((1, LANES)), _const_spec((N_BRANCHES, LANES))],
        out_specs=pre_out + [wide_blk, u_blk, u_blk, v_blk, v_blk, kv_blk, v_blk,
                             wide_blk, pl.BlockSpec((tm, LANES), row)],
        out_shape=pre_shape + [wide, u_b16, u_b16, v_b16, v_b16, k_b16, v_b16, wide,
                               jax.ShapeDtypeStruct((bt, LANES), jnp.float32)],
        scratch_shapes=[pltpu.VMEM((KV_WIDTH // LANES, tm, LANES), jnp.float32)],
        compiler_params=pltpu.CompilerParams(dimension_semantics=("parallel",),
                                             vmem_limit_bytes=VMEM_LIMIT),
    )(x2d, *pre_in, g.reshape(1, d), w, qg, jnp.pad(k_gain, ((0, 0), (0, LANES - HEAD_DIM))))
    return (outs[0], outs[1:]) if fused else (x2d, outs)


def _gelu_tanh(x):
    return 0.5 * x * (1.0 + jnp.tanh(0.7978845608028654 * (x + 0.044715 * (x * x * x))))


def _compress_kernel(uk_ref, uv_ref, pe_ref, w1_ref, b1_ref, w2_ref, kg_ref, kc_out, vc_out):
    half = CMP_STRIDE * HEAD_DIM
    for kv, (u_ref, out) in enumerate(((uk_ref, kc_out), (uv_ref, vc_out))):
        uh = u_ref[0, 0]
        rows = uh.shape[0]
        ha = _bdot(uh + pe_ref[kv, 0:1, :], w1_ref[kv, :half, :])
        hb = _bdot(uh + pe_ref[kv, 1:2, :], w1_ref[kv, half:, :])
        hid = _gelu_tanh(ha + pltpu.roll(hb, rows - 1, axis=0) + b1_ref[kv])
        y = _bdot(hid, w2_ref[kv])
        if kv == 0:
            y = y[:, :HEAD_DIM]
            out[0, 0] = (_rms(y) * kg_ref[0:1, :]).astype(out.dtype)
        else:
            low_lanes = lax.broadcasted_iota(jnp.int32, y.shape, 1) < HEAD_DIM
            out[0, 0] = jnp.where(low_lanes, y, 1.0).astype(out.dtype)


def _compress(kc_raw, vc_raw, pe, w1, b1, w2, k_gain):
    nb, ng, rows, half = kc_raw.shape
    n = half // CMP_STRIDE
    uk, uv = kc_raw, vc_raw
    whole = lambda b, g: (b, g, 0, 0)
    u_blk = pl.BlockSpec((1, 1, rows, half), whole)
    w2p = jnp.pad(w2, ((0, 0), (0, 0), (0, LANES - n))).astype(BF16)
    return pl.pallas_call(
        _compress_kernel,
        grid=(nb, ng),
        in_specs=[u_blk, u_blk, _const_spec((2, 2, half)), _const_spec((2, 2 * half, CMP_HIDDEN)),
                  _const_spec((2, 1, CMP_HIDDEN)), _const_spec((2, CMP_HIDDEN, LANES)),
                  _const_spec((N_BRANCHES, n))],
        out_specs=[pl.BlockSpec((1, 1, rows, n), whole), pl.BlockSpec((1, 1, rows, LANES), whole)],
        out_shape=[jax.ShapeDtypeStruct((nb, ng, rows, n), BF16),
                   jax.ShapeDtypeStruct((nb, ng, rows, LANES), BF16)],
        compiler_params=pltpu.CompilerParams(dimension_semantics=("parallel", "parallel"),
                                             vmem_limit_bytes=VMEM_LIMIT),
    )(uk, uv, pe.reshape(2, 2, half), w1.astype(BF16), b1.reshape(2, 1, CMP_HIDDEN), w2p, k_gain)


M_INIT = -1e20


LOG2_E = 1.4426950408889634


MAX_STATIC_BOUND = 50.0


def _softmax_step(s, bias, v_tile, carry, hg, qt, online):
    sb = [s[h * qt:(h + 1) * qt] for h in range(hg)]
    if bias is not None:
        sb = [x + bias for x in sb]
    if not online:
        return jnp.concatenate(
            [carry[h * qt:(h + 1) * qt] + jnp.dot(jnp.exp2(sb[h]).astype(BF16), v_tile,
                                                   preferred_element_type=jnp.float32)
             for h in range(hg)], axis=0)
    m, acc = carry
    m_new = jnp.maximum(m, jnp.concatenate(
        [jnp.max(x, axis=-1, keepdims=True) for x in sb], axis=0))
    p = jnp.concatenate(
        [jnp.exp2(sb[h] - m_new[h * qt:(h + 1) * qt]).astype(BF16) for h in range(hg)], axis=0)
    acc = jnp.exp2(m - m_new) * acc + jnp.dot(p, v_tile, preferred_element_type=jnp.float32)
    return m_new, acc


N_ATTN_INPUTS = 10


def _nsa_attn_kernel(*refs, qt, kt, seq, online, tile):
    t0 = pl.program_id(2) * qt if tile is None else tile * qt
    _nsa_attn_tile(*refs[:N_ATTN_INPUTS], refs[-1], t0=t0, qt=qt, kt=kt, seq=seq, online=online)


def _nsa_attn_tile(bound_ref, q_ref, gate_ref, bcast_ref, kc_ref, vc_ref, ks_ref, vs_ref,
                   kw_ref, vw_ref, o_ref, *, t0, qt, kt, seq, online):
    hg = HEADS_PER_GROUP
    static = isinstance(t0, int)
    n_slc = seq // SLC_BLOCK
    n_cmp_rows = kc_ref.shape[2]
    if static:
        n_cmp_rows = min(n_cmp_rows, -(-((t0 + qt) // CMP_STRIDE) // LANES) * LANES)
    rows = hg * qt

    def aligned(x, m):
        return x if isinstance(x, int) else pl.multiple_of(x, m)
    low_lanes = lax.broadcasted_iota(jnp.int32, (qt, LANES), 1) < HEAD_DIM
    keep_c, keep_s, keep_w = (0.0, 0.0, 0.0) if online else (
        -bound_ref[0], -bound_ref[1], -bound_ref[2])

    def swap_halves(x):
        return pltpu.roll(x, HEAD_DIM, axis=1)

    q = q_ref[...].astype(jnp.float32)
    q_n = []
    for h in range(hg):
        two = q[:, (h // PAIR) * LANES:(h // PAIR + 1) * LANES]
        q_n.append(jnp.where(low_lanes, swap_halves(two) if h % PAIR else two, 0.0))
    qs = jnp.concatenate([x[:, :HEAD_DIM] for x in q_n], axis=0).astype(BF16)
    t_rows = t0 + lax.broadcasted_iota(jnp.int32, (qt, 1), 0)

    w0 = max(t0 - WINDOW, 0) if static else pl.multiple_of(jnp.maximum(t0 - WINDOW, 0), qt)
    span = t0 + qt - w0 if static else WINDOW + qt
    s = _bdot_nt(qs, kc_ref[0, 0, :n_cmp_rows, :])
    s_win = _bdot_nt(qs, kw_ref[0, 0, pl.ds(w0, span), :])

    cmp_end = lax.broadcasted_iota(jnp.int32, (1, n_cmp_rows), 1) * CMP_STRIDE + (CMP_BLOCK - 1)
    cbias = jnp.where(cmp_end <= t_rows, keep_c, NEG_INF)
    p_heads = []
    for h in range(hg):
        sb = s[h * qt:(h + 1) * qt] + cbias
        if online:
            sb = sb - jnp.maximum(jnp.max(sb, axis=-1, keepdims=True), M_INIT)
        e = jnp.exp2(sb)
        p_heads.append(e / jnp.maximum(jnp.sum(e, axis=-1, keepdims=True), 1e-37))
    o_cmp = jnp.dot(jnp.concatenate([p.astype(BF16) for p in p_heads], axis=0),
                    vc_ref[0, 0, :n_cmp_rows, :], preferred_element_type=jnp.float32)

    p_sum = p_heads[0]
    for h in range(1, hg):
        p_sum = p_sum + p_heads[h]
    jn = lax.broadcasted_iota(jnp.int32, (n_slc, n_cmp_rows), 0) * SLC_BLOCK
    nn = lax.broadcasted_iota(jnp.int32, (n_slc, n_cmp_rows), 1) * CMP_STRIDE
    ov_t = jnp.where((nn < jn + SLC_BLOCK) & (nn + (CMP_BLOCK - 1) >= jn), 1.0, 0.0).astype(BF16)
    p_hi = p_sum.astype(BF16)
    p_lo = (p_sum - p_hi.astype(p_sum.dtype)).astype(BF16)
    imp_t = (lax.dot_general(ov_t, p_hi, (((1,), (1,)), ((), ())), preferred_element_type=jnp.float32)
             + lax.dot_general(ov_t, p_lo, (((1,), (1,)), ((), ())), preferred_element_type=jnp.float32))

    jb = lax.broadcasted_iota(jnp.int32, (n_slc, qt), 0)
    tq = t0 + lax.broadcasted_iota(jnp.int32, (n_slc, qt), 1)
    dist = jnp.right_shift(tq, SLC_SHIFT) - jb
    forced = (jb == 0) | ((dist >= 0) & (dist < N_LOCAL_BLOCKS))
    score = jnp.where(dist >= 0, imp_t + jnp.where(forced, FORCE_BONUS, 0.0), -jnp.inf)
    sub = 8
    n_cand = (t0 + qt - 1) // SLC_BLOCK + 1 if static else n_slc
    if n_cand <= SLC_TOPK:
        n_cand = 0
    groups = [score[v * sub:(v + 1) * sub] for v in range(n_slc // sub)]
    jrow = lax.broadcasted_iota(jnp.int32, (sub, qt), 0)
    cnt = [jnp.zeros((sub, qt), jnp.float32) for _ in groups]
    for j2 in range(n_cand):
        row = score[j2:j2 + 1, :]
        for v, sv in enumerate(groups[:-(-n_cand // sub)]):
            if v * sub > j2:
                inc = jnp.where(row >= sv, 1.0, 0.0)
            elif (v + 1) * sub - 1 <= j2:
                inc = jnp.where(row > sv, 1.0, 0.0)
            else:
                inc = jnp.where(jrow + v * sub > j2, jnp.where(row >= sv, 1.0, 0.0),
                                jnp.where(row > sv, 1.0, 0.0))
            cnt[v] = cnt[v] + inc
    cnt = jnp.concatenate(cnt, axis=0)
    blk_bias_t = jnp.where(dist >= 0, jnp.where(cnt < min(SLC_TOPK, n_slc), keep_s, NEG_INF),
                           NEG_INF)
    parts = [jnp.zeros((HEAD_DIM, qt), jnp.float32), blk_bias_t]
    if n_slc < HEAD_DIM:
        parts.append(jnp.zeros((HEAD_DIM - n_slc, qt), jnp.float32))
    blk_bias = jnp.concatenate(parts, axis=0).T
    q_aug = jnp.concatenate([x + blk_bias for x in q_n], axis=0).astype(BF16)

    lag = t_rows - (w0 + lax.broadcasted_iota(jnp.int32, (1, span), 1))
    wbias = jnp.where(lag >= 0, jnp.where(lag < WINDOW, keep_w, NEG_INF), NEG_INF)
    v_win = vw_ref[0, 0, pl.ds(w0, span), :]
    acc_win = []
    for h in range(hg):
        sb = s_win[h * qt:(h + 1) * qt] + wbias
        if online:
            sb = sb - jnp.max(sb, axis=-1, keepdims=True)
        acc_win.append(jnp.dot(jnp.exp2(sb).astype(BF16), v_win,
                               preferred_element_type=jnp.float32))
    acc_win = jnp.concatenate(acc_win, axis=0)

    def slc_tile(k0, width, bias, carry):
        k_tile = ks_ref[0, 0, pl.ds(k0, width), :]
        s = jnp.concatenate([_bdot_nt(q_aug[h * qt:(h + 1) * qt], k_tile) for h in range(hg)],
                            axis=0)
        return _softmax_step(s, bias, vs_ref[0, 0, pl.ds(k0, width), :], carry, hg, qt, online)

    def slc_body(i, carry):
        return slc_tile(aligned(i * kt, kt), kt, None, carry)

    carry = jnp.zeros((rows, LANES), jnp.float32)
    if online:
        carry = (jnp.full((rows, 1), M_INIT, jnp.float32), carry)
    n_full = t0 // kt
    if static:
        for i in range(n_full):
            carry = slc_body(i, carry)
    else:
        carry = lax.fori_loop(0, n_full // 2,
                              lambda j, c: slc_body(2 * j + 1, slc_body(2 * j, c)), carry)
        carry = lax.cond(n_full % 2 == 1, lambda c: slc_body(n_full - 1, c), lambda c: c, carry)
    k0 = aligned(n_full * kt, kt)
    last = t0 + qt - k0 if static else kt
    causal = jnp.where(k0 + lax.broadcasted_iota(jnp.int32, (1, last), 1) <= t_rows, 0.0, NEG_INF)
    acc_slc = slc_tile(k0, last, causal, carry)
    if online:
        acc_slc = acc_slc[1]

    gates = 1.0 / (1.0 + jnp.exp(-gate_ref[...]))
    spread = jnp.dot(gates.astype(BF16), bcast_ref[0],
                     preferred_element_type=jnp.float32)

    def pair(acc, j, normalise):
        a0 = acc[(2 * j) * qt:(2 * j + 1) * qt]
        a1 = acc[(2 * j + 1) * qt:(2 * j + 2) * qt]
        num = jnp.where(low_lanes, a0, swap_halves(a1))
        if not normalise:
            return num
        return num / jnp.where(low_lanes, swap_halves(a0), a1)

    for j in range(hg // PAIR):
        width = PAIR * HEAD_DIM
        o = jnp.zeros((qt, width), jnp.float32)
        for br, (acc, normalise) in enumerate(((o_cmp, False), (acc_slc, True), (acc_win, True))):
            col = (br * (hg // PAIR) + j) * width
            o = o + spread[:, col:col + width] * pair(acc, j, normalise)
        o_ref[:, j * width:(j + 1) * width] = o.astype(o_ref.dtype)


def _nsa_attn(q, gate, q_gain, k_gain, kc, vc, ks, vs, kw, vw, qt=256, kt=512):
    bt, d = q.shape
    nb, ng, seq, n = kw.shape
    kt = min(kt, seq)
    assert seq >= WINDOW + qt and seq % kt == 0 and kt % qt == 0
    nq = seq // qt
    hg = HEADS_PER_GROUP
    width = hg * n
    rows_c = kc.shape[2]
    col = jnp.arange(N_BRANCHES * width) // n
    src = (col // hg) * N_HEADS + jnp.arange(ng)[:, None] * hg + col % hg
    bcast = (jnp.arange(LANES)[None, :, None] == src[:, None, :]).astype(BF16)
    bounds = (1.01 * LOG2_E * n ** 0.5) * jnp.max(jnp.abs(q_gain)) * jnp.max(jnp.abs(k_gain), axis=1)
    out_sds = jax.ShapeDtypeStruct((bt, d), BF16)

    def specs(tile):
        pick = (lambda b, g, i: i) if tile is None else (lambda b, g: tile)
        at = lambda f: (lambda *ids: f(ids[0], ids[1], pick(*ids)))
        keys = seq if tile is None else (tile + 1) * qt
        cmp_rows = rows_c if tile is None else min(rows_c, -(-(keys // CMP_STRIDE) // LANES) * LANES)
        whole = at(lambda b, g, i: (b, g, 0, 0))
        q_blk = pl.BlockSpec((qt, width), at(lambda b, g, i: (b * nq + i, g)))
        in_specs = [pl.BlockSpec(memory_space=pltpu.SMEM), q_blk,
                    pl.BlockSpec((qt, LANES), at(lambda b, g, i: (b * nq + i, 0))),
                    pl.BlockSpec((1, LANES, N_BRANCHES * width), at(lambda b, g, i: (g, 0, 0))),
                    pl.BlockSpec((1, 1, cmp_rows, n), whole),
                    pl.BlockSpec((1, 1, cmp_rows, LANES), whole),
                    pl.BlockSpec((1, 1, keys, LANES), whole), pl.BlockSpec((1, 1, keys, LANES), whole),
                    pl.BlockSpec((1, 1, keys, n), whole), pl.BlockSpec((1, 1, keys, LANES), whole)]
        return in_specs, q_blk

    def generic(*args):
        in_specs, q_blk = specs(None)
        return pl.pallas_call(
            functools.partial(_nsa_attn_kernel, qt=qt, kt=kt, seq=seq, online=True, tile=None),
            grid=(nb, ng, nq), in_specs=in_specs, out_specs=q_blk, out_shape=out_sds,
            compiler_params=pltpu.CompilerParams(
                dimension_semantics=("parallel", "parallel", "arbitrary"),
                vmem_limit_bytes=VMEM_LIMIT),
        )(*args)

    def per_tile(*args):
        o = None
        for tile in range(nq):
            in_specs, q_blk = specs(tile)
            operands = list(args)
            if o is not None:
                in_specs.append(pl.BlockSpec(memory_space=pl.ANY))
                operands.append(o)
            o = pl.pallas_call(
                functools.partial(_nsa_attn_kernel, qt=qt, kt=kt, seq=seq, online=False, tile=tile),
                grid=(nb, ng), in_specs=in_specs, out_specs=q_blk, out_shape=out_sds,
                input_output_aliases={} if tile == 0 else {N_ATTN_INPUTS: 0},
                compiler_params=pltpu.CompilerParams(
                    dimension_semantics=("parallel", "parallel"), vmem_limit_bytes=VMEM_LIMIT),
            )(*operands)
        return o

    return lax.cond(jnp.max(bounds) <= MAX_STATIC_BOUND, per_tile, generic,
                    bounds.astype(jnp.float32), q, gate, bcast, kc, vc, ks, vs, kw, vw)


def _nsa_layer(x2d, nb, seq, g, w_in, q_gain, k_gain, cmp_pe, cmp_w1, cmp_b1, cmp_w2, w_out,
               prev=None):
    x2d, (q, kc_raw, vc_raw, ks, vs, kw, vw, z, gate) = _nsa_in(x2d, nb, seq, g, w_in, q_gain,
                                                                k_gain, prev)
    kc, vc = _compress(kc_raw, vc_raw, cmp_pe, cmp_w1, cmp_b1, cmp_w2, k_gain)
    o = _nsa_attn(q, gate, q_gain, k_gain, kc, vc, ks, vs, kw, vw)
    return _out_proj(x2d, o, z, w_out)


def kernel(x, norm_g, rwkv_mu, rwkv_w_in, rwkv_w0, rwkv_w1, rwkv_w2, rwkv_a0, rwkv_a1, rwkv_a2, rwkv_k_k, rwkv_k_a, rwkv_r_k, rwkv_lnx_g, rwkv_lnx_b, rwkv_w_out, nsa_w_in, nsa_q_gain, nsa_k_gain, nsa_cmp_pe, nsa_cmp_w1, nsa_cmp_b1, nsa_cmp_w2, nsa_w_out):
    b, t, d = x.shape
    x2d = x.reshape(b * t, d)
    o, z = _rwkv_mixer(x2d, t, norm_g[0], rwkv_mu[0], rwkv_w_in[0], rwkv_w0[0], rwkv_w1[0],
                       rwkv_w2[0], rwkv_a0[0], rwkv_a1[0], rwkv_a2[0], rwkv_k_k[0], rwkv_k_a[0],
                       rwkv_r_k[0].reshape(-1), rwkv_lnx_g[0], rwkv_lnx_b[0])
    x2d = _nsa_layer(x2d, b, t, norm_g[1], nsa_w_in[0], nsa_q_gain[0], nsa_k_gain[0],
                     nsa_cmp_pe[0], nsa_cmp_w1[0], nsa_cmp_b1[0], nsa_cmp_w2[0], nsa_w_out[0],
                     prev=(o, z, rwkv_w_out[0]))
    return x2d.reshape(b, t, d)
```
